```python
import jax, jax.numpy as jnp
from jax import lax
import numpy as np

D_MODEL = 2048
BATCH = 8
SEQ = 2048
DEPTH = 2

EPS = 1e-6
NEG_INF = -1e30
FORCED_SCORE = 1e4
SSM_D_INNER = D_MODEL
SSM_HEAD_DIM = 64
SSM_HEADS = SSM_D_INNER // SSM_HEAD_DIM
SSM_GROUPS = 8
SSM_STATE = 128
SSM_CONV = 4
SSM_CHUNK = 128
SSM_XBC = SSM_D_INNER + 2 * SSM_GROUPS * SSM_STATE
SC_WIDTH = D_MODEL
SC_CONV = 3
NSA_HEADS = 16
NSA_HEAD_DIM = 128
NSA_KV_HEADS = 4
NSA_GROUP = NSA_HEADS // NSA_KV_HEADS
CMP_STRIDE = 16
CMP_LEN = 2 * CMP_STRIDE
SEL_BLOCK = 64
SEL_TOPK = 8
WINDOW = 512
Q_BLOCK = SEL_BLOCK
N_BRANCH = 3
D_FF = 5632
N_EXPERTS = 8
TOP_K = 2

SPLIT_SIZES = (
    SSM_D_INNER,
    SSM_XBC,
    SSM_HEADS,
    3 * SC_WIDTH,
    NSA_HEADS * NSA_HEAD_DIM,
    6 * NSA_KV_HEADS * NSA_HEAD_DIM,
    N_BRANCH * NSA_HEADS,
    N_BRANCH * D_MODEL,
)
IN_COLS = sum(SPLIT_SIZES)
SPLIT_POINTS = tuple(sum(SPLIT_SIZES[:i + 1]) for i in range(len(SPLIT_SIZES) - 1))

kernel_name = 'hybrid_ssd_shortconv_nsa_moe'


def rmsnorm(x, g):
    xf = x.astype(jnp.float32)
    y = xf * lax.rsqrt(jnp.mean(xf * xf, axis=-1, keepdims=True) + EPS)
    return (y * g.astype(jnp.float32)).astype(x.dtype)


def causal_dwconv(x, w):
    width, ch = w.shape
    return lax.conv_general_dilated(
        x, w[:, None, :].astype(x.dtype), window_strides=(1,), padding=((width - 1, 0),),
        dimension_numbers=('NWC', 'WIO', 'NWC'), feature_group_count=ch)


def alibi_slopes():
    h = jnp.arange(1, NSA_HEADS + 1, dtype=jnp.float32)
    return jnp.exp2(-8.0 * h / NSA_HEADS).reshape(NSA_KV_HEADS, NSA_GROUP)


def ssd_mixer(xbc, z, dt_raw, conv_w, conv_b, dt_bias, a_log, d_skip, norm_g):
    bsz, seq, _ = xbc.shape
    nc = seq // SSM_CHUNK
    L, G, R, P, N = SSM_CHUNK, SSM_GROUPS, SSM_HEADS // SSM_GROUPS, SSM_HEAD_DIM, SSM_STATE
    xbc = jax.nn.silu(causal_dwconv(xbc, conv_w) + conv_b)
    xs, b_in, c_out = jnp.split(xbc, (SSM_D_INNER, SSM_D_INNER + G * N), axis=-1)
    xs = xs.reshape(bsz, nc, L, G, R, P)
    b_in = b_in.reshape(bsz, nc, L, G, N)
    c_out = c_out.reshape(bsz, nc, L, G, N)
    dt = jax.nn.softplus((dt_raw + dt_bias).astype(jnp.float32)).reshape(bsz, nc, L, G, R)
    a = -jnp.exp(a_log.astype(jnp.float32)).reshape(G, R)
    a_cum = jnp.cumsum(jnp.moveaxis(dt * a, 2, -1), axis=-1)
    xdt = xs * dt[..., None].astype(xs.dtype)
    causal = jnp.tril(jnp.ones((L, L), dtype=bool))
    decay = jnp.exp(jnp.where(causal, a_cum[..., :, None] - a_cum[..., None, :], -jnp.inf))
    cb = jnp.einsum('bclgn,bcsgn->bcgls', c_out, b_in)
    y_diag = jnp.einsum('bcgls,bcgrls,bcsgrp->bclgrp', cb, decay.astype(cb.dtype), xdt)
    decay_end = jnp.exp(a_cum[..., -1:] - a_cum)
    states = jnp.einsum('bcsgn,bcgrs,bcsgrp->bcgrpn', b_in, decay_end.astype(xdt.dtype), xdt)
    chunk_decay = jnp.exp(a_cum[..., -1]).astype(states.dtype)

    def step(h, inp):
        s_c, d_c = inp
        return h * d_c[..., None, None] + s_c, h

    h0 = jnp.zeros((bsz, G, R, P, N), states.dtype)
    _, h_prev = lax.scan(step, h0, (jnp.moveaxis(states, 1, 0), jnp.moveaxis(chunk_decay, 1, 0)))
    h_prev = jnp.moveaxis(h_prev, 0, 1)
    y_off = jnp.einsum('bclgn,bcgrpn,bcgrl->bclgrp', c_out, h_prev, jnp.exp(a_cum).astype(h_prev.dtype))
    y = y_diag + y_off + xs * d_skip.reshape(G, R)[:, :, None].astype(xs.dtype)
    y = y.reshape(bsz, seq, SSM_D_INNER)
    return rmsnorm(y * jax.nn.silu(z), norm_g)


def short_conv_mixer(bcx, conv_w):
    b_gate, c_gate, x_in = jnp.split(bcx, 3, axis=-1)
    return b_gate * causal_dwconv(c_gate * x_in, conv_w)


def compress_blocks(t, pos, w):
    bsz, seq, G, Dh = t.shape
    halves = t.reshape(bsz, seq // CMP_STRIDE, CMP_STRIDE, G, Dh)
    blocks = jnp.concatenate([halves[:, :-1], halves[:, 1:]], axis=2) + pos[:, None, :]
    return jnp.einsum('bnlgd,lde->bnge', blocks, w.reshape(CMP_LEN, Dh, Dh))


def alibi_attend(q, k, v, dist, valid, slopes, scale):
    s = jnp.einsum('bqgrd,bgkd->bgrqk', q, k).astype(jnp.float32) * scale - slopes[:, :, None, None] * dist.astype(jnp.float32)
    p = jax.nn.softmax(jnp.where(valid, s, NEG_INF), axis=-1)
    return jnp.einsum('bgrqk,bgkd->bqgrd', p.astype(v.dtype), v)


def nsa_mixer(q, kv, gate_raw, q_norm_g, k_norm_g, cmp_pos, w_cmp_k, w_cmp_v):
    bsz, seq, _ = q.shape
    G, R, Dh = NSA_KV_HEADS, NSA_GROUP, NSA_HEAD_DIM
    nqb = seq // Q_BLOCK
    slopes = alibi_slopes()
    scale = Dh ** -0.5
    q5 = rmsnorm(q.reshape(bsz, seq, G, R, Dh), q_norm_g)
    k_c, v_c, k_s, v_s, k_w, v_w = jnp.moveaxis(kv.reshape(bsz, seq, 6, G, Dh), 2, 0)
    qpos = jnp.arange(seq)

    k_cmp = rmsnorm(compress_blocks(k_c, cmp_pos[0], w_cmp_k), k_norm_g[0])
    v_cmp = compress_blocks(v_c, cmp_pos[1], w_cmp_v)
    n_cmp = k_cmp.shape[1]
    cmp_start = jnp.arange(n_cmp) * CMP_STRIDE
    valid_c = (cmp_start + (CMP_LEN - 1))[None, :] <= qpos[:, None]
    dist_c = (qpos[:, None] - cmp_start[None, :]).astype(jnp.float32) - (CMP_LEN - 1) / 2
    s_c = jnp.einsum('bqgrd,bngd->bgrqn', q5, k_cmp).astype(jnp.float32) * scale - slopes[:, :, None, None] * dist_c
    p_c = jnp.where(valid_c, jax.nn.softmax(jnp.where(valid_c, s_c, NEG_INF), axis=-1), 0.0)
    o_cmp = jnp.einsum('bgrqn,bngd->bqgrd', p_c.astype(v_cmp.dtype), v_cmp)

    sel_start = jnp.arange(nqb) * SEL_BLOCK
    overlap = jnp.clip(jnp.minimum(cmp_start[:, None] + CMP_LEN, sel_start[None, :] + SEL_BLOCK)
                       - jnp.maximum(cmp_start[:, None], sel_start[None, :]), 0).astype(jnp.float32) / CMP_LEN
    p_first = p_c[:, :, :, ::Q_BLOCK, :].sum(axis=2)
    imp = jnp.einsum('bgqn,nj->bgqj', p_first, overlap)
    qb_i = jnp.arange(nqb)[:, None]
    blk_j = jnp.arange(nqb)[None, :]
    forced = (blk_j == 0) | (blk_j == qb_i) | (blk_j == qb_i - 1)
    imp = jnp.where(forced, FORCED_SCORE, jnp.where(blk_j > qb_i, -FORCED_SCORE, imp))
    _, sel_idx = lax.top_k(imp, min(SEL_TOPK, nqb))

    k_sel_blk = jnp.moveaxis(rmsnorm(k_s, k_norm_g[1]).reshape(bsz, nqb, SEL_BLOCK, G, Dh), 3, 1)
    v_sel_blk = jnp.moveaxis(v_s.reshape(bsz, nqb, SEL_BLOCK, G, Dh), 3, 1)
    pad = ((0, 0), (WINDOW, 0), (0, 0), (0, 0))
    k_win_p = jnp.pad(rmsnorm(k_w, k_norm_g[2]), pad)
    v_win_p = jnp.pad(v_w, pad)
    b_ix = jnp.arange(bsz)[:, None, None]
    g_ix = jnp.arange(G)[None, :, None]
    in_blk = jnp.arange(SEL_BLOCK)
    win_off = jnp.arange(WINDOW + Q_BLOCK) - WINDOW

    def block_step(args):
        qb, q_blk, idx = args
        qpos_b = qb * Q_BLOCK + jnp.arange(Q_BLOCK)
        ks = k_sel_blk[b_ix, g_ix, idx].reshape(bsz, G, -1, Dh)
        vs = v_sel_blk[b_ix, g_ix, idx].reshape(bsz, G, -1, Dh)
        kpos_s = (idx[..., None] * SEL_BLOCK + in_blk).reshape(bsz, G, -1)
        dist_s = qpos_b[:, None] - kpos_s[:, :, None, None, :]
        o_sel = alibi_attend(q_blk, ks, vs, dist_s, dist_s >= 0, slopes, scale)
        kw = jnp.moveaxis(lax.dynamic_slice_in_dim(k_win_p, qb * Q_BLOCK, WINDOW + Q_BLOCK, axis=1), 2, 1)
        vw = jnp.moveaxis(lax.dynamic_slice_in_dim(v_win_p, qb * Q_BLOCK, WINDOW + Q_BLOCK, axis=1), 2, 1)
        kpos_w = qb * Q_BLOCK + win_off
        dist_w = qpos_b[:, None] - kpos_w[None, :]
        valid_w = (dist_w >= 0) & (dist_w < WINDOW) & (kpos_w[None, :] >= 0)
        o_win = alibi_attend(q_blk, kw, vw, dist_w, valid_w, slopes, scale)
        return o_sel, o_win

    q_blocks = jnp.moveaxis(q5.reshape(bsz, nqb, Q_BLOCK, G, R, Dh), 1, 0)
    idx_blocks = jnp.moveaxis(sel_idx, 2, 0)
    o_sel, o_win = lax.map(block_step, (jnp.arange(nqb), q_blocks, idx_blocks))
    o_sel = jnp.moveaxis(o_sel, 0, 1).reshape(bsz, seq, G, R, Dh)
    o_win = jnp.moveaxis(o_win, 0, 1).reshape(bsz, seq, G, R, Dh)

    g = jax.nn.sigmoid(gate_raw.reshape(bsz, seq, N_BRANCH, G, R).astype(jnp.float32)).astype(q.dtype)[..., None]
    o = g[:, :, 0] * o_cmp + g[:, :, 1] * o_sel + g[:, :, 2] * o_win
    return o.reshape(bsz, seq, NSA_HEADS * Dh)


def swiglu(h, w_gate, w_up, w_down):
    return (jax.nn.silu(h @ w_gate) * (h @ w_up)) @ w_down


def moe_ffn(h, router, w_gate, w_up, w_down):
    logits = jnp.einsum('bsd,de->bse', h, router).astype(jnp.float32)
    top_vals, top_idx = lax.top_k(logits, TOP_K)
    top_w = jax.nn.softmax(top_vals, axis=-1)
    combine = jnp.sum(jax.nn.one_hot(top_idx, N_EXPERTS, dtype=jnp.float32) * top_w[..., None], axis=-2).astype(h.dtype)
    out = jnp.zeros_like(h)
    for e in range(N_EXPERTS):
        out = out + combine[..., e:e + 1] * swiglu(h, w_gate[e], w_up[e], w_down[e])
    return out


def setup_inputs(seed: int = 0) -> dict:
    key = jax.random.key(seed)
    ks = jax.random.split(key, 27)
    n_dense = (DEPTH + 1) // 2
    n_moe = DEPTH // 2
    f32 = jnp.float32

    def nrm(k, shape, fan_in):
        return jax.random.normal(k, shape, f32) * (fan_in ** -0.5)

    def gain(k, shape):
        return 1.0 + 0.02 * jax.random.normal(k, shape, f32)

    dt = jnp.exp(jax.random.uniform(ks[5], (DEPTH, SSM_HEADS), f32, np.log(1e-3), np.log(1e-1)))
    dt_bias = dt + jnp.log(-jnp.expm1(-dt))
    return {
        'x': jax.random.normal(ks[0], (BATCH, SEQ, D_MODEL), f32),
        'norm_mix': gain(ks[1], (DEPTH, D_MODEL)),
        'w_in': nrm(ks[2], (DEPTH, D_MODEL, IN_COLS), D_MODEL),
        'ssm_conv_w': nrm(ks[3], (DEPTH, SSM_CONV, SSM_XBC), SSM_CONV),
        'ssm_conv_b': 0.02 * jax.random.normal(ks[4], (DEPTH, SSM_XBC), f32),
        'ssm_dt_bias': dt_bias,
        'ssm_a_log': jnp.log(jax.random.uniform(ks[6], (DEPTH, SSM_HEADS), f32, 1.0, 16.0)),
        'ssm_d': gain(ks[7], (DEPTH, SSM_HEADS)),
        'ssm_norm': gain(ks[8], (DEPTH, SSM_D_INNER)),
        'w_ssm_out': nrm(ks[9], (DEPTH, SSM_D_INNER, D_MODEL), SSM_D_INNER),
        'sc_conv_w': nrm(ks[10], (DEPTH, SC_CONV, SC_WIDTH), SC_CONV),
        'w_sc_out': nrm(ks[11], (DEPTH, SC_WIDTH, D_MODEL), SC_WIDTH),
        'q_norm': gain(ks[12], (DEPTH, NSA_HEAD_DIM)),
        'k_norm': gain(ks[13], (DEPTH, N_BRANCH, NSA_HEAD_DIM)),
        'cmp_pos': 0.1 * jax.random.normal(ks[14], (DEPTH, 2, CMP_LEN, NSA_HEAD_DIM), f32),
        'w_cmp_k': nrm(ks[15], (DEPTH, CMP_LEN * NSA_HEAD_DIM, NSA_HEAD_DIM), CMP_LEN * NSA_HEAD_DIM),
        'w_cmp_v': nrm(ks[16], (DEPTH, CMP_LEN * NSA_HEAD_DIM, NSA_HEAD_DIM), CMP_LEN * NSA_HEAD_DIM),
        'w_nsa_out': nrm(ks[17], (DEPTH, NSA_HEADS * NSA_HEAD_DIM, D_MODEL), NSA_HEADS * NSA_HEAD_DIM),
        'w_out': nrm(ks[18], (DEPTH, D_MODEL, D_MODEL), D_MODEL),
        'norm_ffn': gain(ks[19], (DEPTH, D_MODEL)),
        'ffn_w_gate': nrm(ks[20], (n_dense, D_MODEL, D_FF), D_MODEL),
        'ffn_w_up': nrm(ks[21], (n_dense, D_MODEL, D_FF), D_MODEL),
        'ffn_w_down': nrm(ks[22], (n_dense, D_FF, D_MODEL), D_FF),
        'moe_router': nrm(ks[23], (n_moe, D_MODEL, N_EXPERTS), D_MODEL),
        'moe_w_gate': nrm(ks[24], (n_moe, N_EXPERTS, D_MODEL, D_FF), D_MODEL),
        'moe_w_up': nrm(ks[25], (n_moe, N_EXPERTS, D_MODEL, D_FF), D_MODEL),
        'moe_w_down': nrm(ks[26], (n_moe, N_EXPERTS, D_FF, D_MODEL), D_FF),
    }


def reference(x, norm_mix, w_in, ssm_conv_w, ssm_conv_b, ssm_dt_bias, ssm_a_log, ssm_d, ssm_norm, w_ssm_out,
              sc_conv_w, w_sc_out, q_norm, k_norm, cmp_pos, w_cmp_k, w_cmp_v, w_nsa_out, w_out, norm_ffn,
              ffn_w_gate, ffn_w_up, ffn_w_down, moe_router, moe_w_gate, moe_w_up, moe_w_down):
    bsz, seq, _ = x.shape
    for layer in range(DEPTH):
        h = rmsnorm(x, norm_mix[layer])
        proj = h @ w_in[layer]
        z, xbc, dt_raw, sc_in, q, kv, nsa_gate, merge_gate = jnp.split(proj, SPLIT_POINTS, axis=-1)
        y_ssm = ssd_mixer(xbc, z, dt_raw, ssm_conv_w[layer], ssm_conv_b[layer], ssm_dt_bias[layer],
                          ssm_a_log[layer], ssm_d[layer], ssm_norm[layer])
        y_sc = short_conv_mixer(sc_in, sc_conv_w[layer])
        y_nsa = nsa_mixer(q, kv, nsa_gate, q_norm[layer], k_norm[layer], cmp_pos[layer],
                          w_cmp_k[layer], w_cmp_v[layer])
        g = jax.nn.sigmoid(merge_gate.reshape(bsz, seq, N_BRANCH, D_MODEL).astype(jnp.float32)).astype(x.dtype)
        mixed = (g[:, :, 0] * (y_ssm @ w_ssm_out[layer])
                 + g[:, :, 1] * (y_sc @ w_sc_out[layer])
                 + g[:, :, 2] * (y_nsa @ w_nsa_out[layer]))
        x = x + mixed @ w_out[layer]
        h = rmsnorm(x, norm_ffn[layer])
        i = layer // 2
        if layer % 2 == 0:
            x = x + swiglu(h, ffn_w_gate[i], ffn_w_up[i], ffn_w_down[i])
        else:
            x = x + moe_ffn(h, moe_router[i], moe_w_gate[i], moe_w_up[i], moe_w_down[i])
    return x
```

```python
import functools

import jax
import jax.numpy as jnp
from jax import lax
from jax.experimental import pallas as pl
from jax.experimental.pallas import tpu as pltpu

F32 = jnp.float32
BF16 = jnp.bfloat16
I32 = jnp.int32

D_MODEL = 2048
EPS = 1e-6
NEG_INF = -1e30
FORCED_SCORE = 1e4
SSM_HEAD_DIM = 64
SSM_HEADS = D_MODEL // SSM_HEAD_DIM
SSM_GROUPS = 8
SSM_HEADS_PER_GROUP = SSM_HEADS // SSM_GROUPS
SSM_STATE = 128
SSM_CONV = 4
SSM_CHUNK = 128
SSM_XBC = D_MODEL + 2 * SSM_GROUPS * SSM_STATE
SC_CONV = 3
NSA_HEADS = 16
NSA_HEAD_DIM = 128
NSA_KV_HEADS = 4
NSA_GROUP = NSA_HEADS // NSA_KV_HEADS
CMP_STRIDE = 16
CMP_LEN = 2 * CMP_STRIDE
SEL_BLOCK = 64
SEL_TOPK = 8
WINDOW = 512
N_BRANCH = 3
D_FF = 5632
N_EXPERTS = 8
TOP_K = 2

XBC_OFF = 0
Z_OFF = XBC_OFF + SSM_XBC
SC_OFF = Z_OFF + D_MODEL
Q_OFF = SC_OFF + 3 * D_MODEL
KV_OFF = Q_OFF + NSA_HEADS * NSA_HEAD_DIM
MG_OFF = KV_OFF + 6 * NSA_KV_HEADS * NSA_HEAD_DIM
PROJ_W = MG_OFF + N_BRANCH * D_MODEL
LANES = 128
SMALL_W = LANES
GATE_LANE0 = SSM_HEADS
_SRC_Z = 0
_SRC_XBC = D_MODEL
_SRC_DT = _SRC_XBC + SSM_XBC
_SRC_SC = _SRC_DT + SSM_HEADS
_SRC_Q = _SRC_SC + 3 * D_MODEL
_SRC_KV = _SRC_Q + NSA_HEADS * NSA_HEAD_DIM
_SRC_NG = _SRC_KV + 6 * NSA_KV_HEADS * NSA_HEAD_DIM
_SRC_MG = _SRC_NG + N_BRANCH * NSA_HEADS

VMEM_LIMIT = 56 * 1024 * 1024
MOE_TILE = 512
GATHER_ROWS = 256


def _params(sem, vmem=VMEM_LIMIT):
    return pltpu.CompilerParams(dimension_semantics=sem, vmem_limit_bytes=vmem)


def _sigmoid(x):
    return 1.0 / (1.0 + jnp.exp(-x))


def _silu(x):
    return x * _sigmoid(x)


def _rms_rows(x, g):
    return x * lax.rsqrt(jnp.mean(x * x, axis=-1, keepdims=True) + EPS) * g


def _norm_to_scratch(x_ref, g_ref, h_scr, rows, chunk=256):
    g = g_ref[...]

    def body(i, carry):
        r0 = pl.multiple_of(i * chunk, chunk)
        h_scr[pl.ds(r0, chunk), :] = _rms_rows(x_ref[pl.ds(r0, chunk), :], g).astype(BF16)
        return carry

    lax.fori_loop(0, rows // chunk, body, 0)


def _inproj_kernel(x_ref, g_ref, w_ref, ws_ref, proj_ref, small_ref, h_scr, *, tm):
    @pl.when(pl.program_id(1) == 0)
    def _():
        _norm_to_scratch(x_ref, g_ref, h_scr, tm)
        small_ref[...] = jnp.dot(h_scr[...], ws_ref[...], preferred_element_type=F32)

    proj_ref[...] = jnp.dot(h_scr[...], w_ref[...], preferred_element_type=F32).astype(BF16)


def _inproj(x2, g, w_big, w_small, tm=1024, tn=512):
    m = x2.shape[0]
    tm = min(tm, m)
    return pl.pallas_call(
        functools.partial(_inproj_kernel, tm=tm),
        grid=(m // tm, PROJ_W // tn),
        in_specs=[
            pl.BlockSpec((tm, D_MODEL), lambda i, j: (i, 0)),
            pl.BlockSpec((1, D_MODEL), lambda i, j: (0, 0)),
            pl.BlockSpec((D_MODEL, tn), lambda i, j: (0, j)),
            pl.BlockSpec((D_MODEL, SMALL_W), lambda i, j: (0, 0)),
        ],
        out_specs=[
            pl.BlockSpec((tm, tn), lambda i, j: (i, j)),
            pl.BlockSpec((tm, SMALL_W), lambda i, j: (i, 0)),
        ],
        out_shape=[
            jax.ShapeDtypeStruct((m, PROJ_W), BF16),
            jax.ShapeDtypeStruct((m, SMALL_W), F32),
        ],
        scratch_shapes=[pltpu.VMEM((tm, D_MODEL), BF16)],
        compiler_params=_params(("arbitrary", "arbitrary")),
        name="inproj",
    )(x2, g, w_big, w_small)


def _shift_rows(cur, prev_tail, s):
    rc = pltpu.roll(cur, s, 0)
    rp = pltpu.roll(prev_tail, s, 0)
    row = lax.broadcasted_iota(I32, (8, cur.shape[1]), 0)
    top = jnp.where(row < s, rp[0:8], rc[0:8])
    return jnp.concatenate([top, rc[8:]], axis=0)


def _ssd_kernel(xbc_ref, prev_ref, z_ref, small_ref, cw_ref, cb_ref, dtb_ref, alog_ref, dskip_ref,
                ng_ref, out_ref, state_scr, y_scr):
    L = SSM_CHUNK
    P = SSM_HEAD_DIM
    R = SSM_HEADS_PER_GROUP
    GW = R * P
    c = pl.program_id(1)

    @pl.when(c == 0)
    def _():
        state_scr[...] = jnp.zeros_like(state_scr)

    keep_prev = (c > 0).astype(F32)

    def conv_silu(lo, w):
        cur = xbc_ref[:, lo:lo + w].astype(F32)
        prev = prev_ref[:, lo:lo + w].astype(F32) * keep_prev
        acc = cur * cw_ref[SSM_CONV - 1:SSM_CONV, lo:lo + w] + cb_ref[:, lo:lo + w]
        for s in range(1, SSM_CONV):
            k = SSM_CONV - 1 - s
            acc = acc + _shift_rows(cur, prev, s) * cw_ref[k:k + 1, lo:lo + w]
        return _silu(acc)

    row = lax.broadcasted_iota(I32, (L, L), 0)
    col = lax.broadcasted_iota(I32, (L, L), 1)
    causal = row >= col
    tril = causal.astype(F32)

    pre = small_ref[...] + dtb_ref[...]
    dt = jnp.maximum(pre, 0.0) + jnp.log(1.0 + jnp.exp(-jnp.abs(pre)))
    dt = jnp.where(col < SSM_HEADS, dt, 0.0)
    a_neg = -jnp.exp(alog_ref[...])
    a_cum = jnp.dot(tril, dt * a_neg, precision=lax.Precision.HIGHEST, preferred_element_type=F32)
    a_cum_t = a_cum.T
    a_last = a_cum[L - 1:L, :]
    dec_end = jnp.exp(a_last - a_cum)
    dec_in = jnp.exp(a_cum)
    chunk_dec = jnp.exp(a_last)

    lane_g = lax.broadcasted_iota(I32, (L, GW), 1)
    lane_g1 = lax.broadcasted_iota(I32, (1, GW), 1)

    def expand(mat, g, lanes):
        out = mat[:, R * g + R - 1:R * g + R]
        for r in range(R - 2, -1, -1):
            out = jnp.where(lanes < (r + 1) * P, mat[:, R * g + r:R * g + r + 1], out)
        return out

    for g in range(SSM_GROUPS):
        xs = conv_silu(g * GW, GW)
        b_in = conv_silu(D_MODEL + g * SSM_STATE, SSM_STATE)
        c_out = conv_silu(D_MODEL + (SSM_GROUPS + g) * SSM_STATE, SSM_STATE)
        b_bf = b_in.astype(BF16)
        c_bf = c_out.astype(BF16)
        cb = lax.dot_general(c_bf, b_bf, (((1,), (1,)), ((), ())), preferred_element_type=F32)
        xdt = xs * expand(dt, g, lane_g)
        y = xs * dskip_ref[:, g * GW:(g + 1) * GW]
        for r in range(R):
            h = R * g + r
            seg = a_cum[:, h:h + 1] - a_cum_t[h:h + 1, :]
            lmat = jnp.where(causal, jnp.exp(jnp.minimum(seg, 0.0)), 0.0)
            head = (lane_g >= r * P) & (lane_g < (r + 1) * P)
            xdt_r = jnp.where(head, xdt, 0.0).astype(BF16)
            y = y + jnp.dot((cb * lmat).astype(BF16), xdt_r, preferred_element_type=F32)
        h_prev = state_scr[g]
        y = y + jnp.dot(c_bf, h_prev.astype(BF16), preferred_element_type=F32) * expand(dec_in, g, lane_g)
        xdt_end = (xdt * expand(dec_end, g, lane_g)).astype(BF16)
        new_state = jnp.dot(b_in.T.astype(BF16), xdt_end, preferred_element_type=F32)
        state_scr[g] = h_prev * expand(chunk_dec, g, lane_g1) + new_state
        y_scr[:, g * GW:(g + 1) * GW] = y

    yz = y_scr[...] * _silu(z_ref[...].astype(F32))
    out_ref[...] = _rms_rows(yz, ng_ref[...]).astype(BF16)


def _ssd(proj, small, conv_w, conv_b, dt_bias, a_log, d_skip, norm_g, bsz, seq):
    L = SSM_CHUNK
    nc = seq // L
    tail = 16
    per = L // tail
    return pl.pallas_call(
        _ssd_kernel,
        grid=(bsz, nc),
        in_specs=[
            pl.BlockSpec((L, SSM_XBC), lambda b, c: (b * nc + c, XBC_OFF // SSM_XBC)),
            pl.BlockSpec((tail, SSM_XBC), lambda b, c: (jnp.maximum((b * nc + c) * per - 1, 0), XBC_OFF // SSM_XBC)),
            pl.BlockSpec((L, D_MODEL), lambda b, c: (b * nc + c, Z_OFF // D_MODEL)),
            pl.BlockSpec((L, SMALL_W), lambda b, c: (b * nc + c, 0)),
            pl.BlockSpec((SSM_CONV, SSM_XBC), lambda b, c: (0, 0)),
            pl.BlockSpec((1, SSM_XBC), lambda b, c: (0, 0)),
            pl.BlockSpec((1, SMALL_W), lambda b, c: (0, 0)),
            pl.BlockSpec((1, SMALL_W), lambda b, c: (0, 0)),
            pl.BlockSpec((1, D_MODEL), lambda b, c: (0, 0)),
            pl.BlockSpec((1, D_MODEL), lambda b, c: (0, 0)),
        ],
        out_specs=pl.BlockSpec((L, D_MODEL), lambda b, c: (b * nc + c, 0)),
        out_shape=jax.ShapeDtypeStruct((bsz * seq, D_MODEL), BF16),
        scratch_shapes=[
            pltpu.VMEM((SSM_GROUPS, SSM_STATE, SSM_HEADS_PER_GROUP * SSM_HEAD_DIM), F32),
            pltpu.VMEM((L, D_MODEL), F32),
        ],
        compiler_params=_params(("arbitrary", "arbitrary")),
        name="ssd",
    )(proj, proj, proj, small, conv_w, conv_b, dt_bias, a_log, d_skip, norm_g)


def _nsa_cmp_kernel(r_ref, q_ref, wk_ref, wv_ref, pos_ref, kng_ref, qng_ref, slope_ref,
                    ocmp_ref, qn_ref, idx_ref, kc_scr, vc_scr, *, seq, tq):
    Dh = NSA_HEAD_DIM
    nb = seq // CMP_STRIDE
    nqb = seq // SEL_BLOCK
    half = CMP_STRIDE
    scale = Dh ** -0.5

    def compress(j, w_ref):
        top = jnp.zeros((nb, Dh), F32)
        bot = jnp.zeros((nb, Dh), F32)
        for l in range(half):
            t = r_ref[0, 0, j, :, l * Dh:(l + 1) * Dh].astype(F32)
            top = top + jnp.dot((t + pos_ref[j, l:l + 1, :]).astype(BF16), w_ref[l], preferred_element_type=F32)
            bot = bot + jnp.dot((t + pos_ref[j, half + l:half + l + 1, :]).astype(BF16), w_ref[half + l],
                                preferred_element_type=F32)
        return top + pltpu.roll(bot, nb - 1, 0)

    kc_scr[...] = _rms_rows(compress(0, wk_ref), kng_ref[...]).astype(BF16)
    vc_scr[...] = compress(1, wv_ref).astype(BF16)
    k_cmp = kc_scr[...]
    v_cmp = vc_scr[...]

    n_start = lax.broadcasted_iota(I32, (1, nb), 1) * CMP_STRIDE
    qg = qng_ref[...]

    def scores(qn, qpos, slope):
        s = lax.dot_general(qn, k_cmp, (((1,), (1,)), ((), ())), preferred_element_type=F32) * scale
        dist = (qpos - n_start).astype(F32) - (CMP_LEN - 1) / 2
        valid = (n_start + (CMP_LEN - 1)) <= qpos
        s = jnp.where(valid, s - slope * dist, NEG_INF)
        p = jnp.exp(s - jnp.max(s, axis=-1, keepdims=True))
        p = p / jnp.sum(p, axis=-1, keepdims=True)
        return jnp.where(valid, p, 0.0)

    sel_row = lax.broadcasted_iota(I32, (nqb, seq), 0) * SEL_BLOCK
    sel_col = lax.broadcasted_iota(I32, (nqb, seq), 1)
    pick_first = jnp.where(sel_row == sel_col, 1.0, 0.0).astype(BF16)
    first_pos = lax.broadcasted_iota(I32, (nqb, 1), 0) * SEL_BLOCK
    p_first = jnp.zeros((nqb, nb), F32)

    for r in range(NSA_GROUP):
        slope = slope_ref[0, r:r + 1, 0:1]

        def body(i, carry, r=r, slope=slope):
            r0 = pl.multiple_of(i * tq, tq)
            qn = _rms_rows(q_ref[pl.ds(r0, tq), r * Dh:(r + 1) * Dh].astype(F32), qg).astype(BF16)
            qn_ref[pl.ds(r0, tq), r * Dh:(r + 1) * Dh] = qn
            qpos = r0 + lax.broadcasted_iota(I32, (tq, 1), 0)
            p = scores(qn, qpos, slope)
            ocmp_ref[pl.ds(r0, tq), r * Dh:(r + 1) * Dh] = jnp.dot(
                p.astype(BF16), v_cmp, preferred_element_type=F32).astype(BF16)
            return carry

        lax.fori_loop(0, seq // tq, body, 0)
        q_first = jnp.dot(pick_first, qn_ref[:, r * Dh:(r + 1) * Dh], preferred_element_type=F32).astype(BF16)
        p_first = p_first + scores(q_first, first_pos, slope)

    n_lo = lax.broadcasted_iota(I32, (nb, nqb), 0) * CMP_STRIDE
    j_lo = lax.broadcasted_iota(I32, (nb, nqb), 1) * SEL_BLOCK
    overlap = jnp.maximum(jnp.minimum(n_lo + CMP_LEN, j_lo + SEL_BLOCK) - jnp.maximum(n_lo, j_lo), 0)
    overlap = overlap.astype(F32) / CMP_LEN
    imp = jnp.dot(p_first, overlap, precision=lax.Precision.HIGHEST, preferred_element_type=F32)
    qb_i = lax.broadcasted_iota(I32, (nqb, nqb), 0)
    blk_j = lax.broadcasted_iota(I32, (nqb, nqb), 1)
    forced = (blk_j == 0) | (blk_j == qb_i) | (blk_j == qb_i - 1)
    imp = jnp.where(forced, FORCED_SCORE, jnp.where(blk_j > qb_i, -FORCED_SCORE, imp))
    out_lane = lax.broadcasted_iota(I32, (nqb, LANES), 1)
    picked = jnp.zeros((nqb, LANES), I32)
    for k in range(SEL_TOPK):
        best = jnp.max(imp, axis=-1, keepdims=True)
        arg = jnp.min(jnp.where(imp == best, blk_j, nqb), axis=-1, keepdims=True)
        picked = jnp.where(out_lane == k, arg, picked)
        imp = jnp.where(blk_j == arg, -jnp.inf, imp)
    idx_ref[0, 0] = picked


def _nsa_cmp(r5, proj, wk, wv, pos, kng, qng, slopes, bsz, seq):
    G = NSA_KV_HEADS
    nb = seq // CMP_STRIDE
    nqb = seq // SEL_BLOCK
    qw = NSA_GROUP * NSA_HEAD_DIM
    return pl.pallas_call(
        functools.partial(_nsa_cmp_kernel, seq=seq, tq=256),
        grid=(bsz, G),
        in_specs=[
            pl.BlockSpec((1, 1, 2, nb, CMP_STRIDE * NSA_HEAD_DIM), lambda b, g: (b, g, 0, 0, 0)),
            pl.BlockSpec((seq, qw), lambda b, g: (b, Q_OFF // qw + g)),
            pl.BlockSpec((CMP_LEN, NSA_HEAD_DIM, NSA_HEAD_DIM), lambda b, g: (0, 0, 0)),
            pl.BlockSpec((CMP_LEN, NSA_HEAD_DIM, NSA_HEAD_DIM), lambda b, g: (0, 0, 0)),
            pl.BlockSpec((2, CMP_LEN, NSA_HEAD_DIM), lambda b, g: (0, 0, 0)),
            pl.BlockSpec((1, NSA_HEAD_DIM), lambda b, g: (0, 0)),
            pl.BlockSpec((1, NSA_HEAD_DIM), lambda b, g: (0, 0)),
            pl.BlockSpec((1, 8, LANES), lambda b, g: (g, 0, 0)),
        ],
        out_specs=[
            pl.BlockSpec((seq, qw), lambda b, g: (b, g)),
            pl.BlockSpec((seq, qw), lambda b, g: (b, g)),
            pl.BlockSpec((1, 1, nqb, LANES), lambda b, g: (b, g, 0, 0)),
        ],
        out_shape=[
            jax.ShapeDtypeStruct((bsz * seq, NSA_HEADS * NSA_HEAD_DIM), BF16),
            jax.ShapeDtypeStruct((bsz * seq, NSA_HEADS * NSA_HEAD_DIM), BF16),
            jax.ShapeDtypeStruct((bsz, G, nqb, LANES), I32),
        ],
        scratch_shapes=[pltpu.VMEM((nb, NSA_HEAD_DIM), BF16), pltpu.VMEM((nb, NSA_HEAD_DIM), BF16)],
        compiler_params=_params(("arbitrary", "arbitrary")),
        name="nsa_cmp",
    )(r5, proj, wk, wv, pos, kng, qng, slopes)


def _nsa_attn_kernel(idx_ref, qn_ref, ks_ref, vs_ref, kw_ref, vw_ref, ocmp_ref, gate_ref, kng_ref, slope_ref,
                     out_ref, ksn_scr, kwn_scr, *, seq):
    Dh = NSA_HEAD_DIM
    R = NSA_GROUP
    QB = SEL_BLOCK
    nqb = seq // QB
    scale = Dh ** -0.5
    b = pl.program_id(0)
    g = pl.program_id(1)
    qb = pl.program_id(2)

    @pl.when(qb == 0)
    def _():
        def body(i, carry):
            r0 = pl.multiple_of(i * 256, 256)
            ksn_scr[pl.ds(r0, 256), :] = _rms_rows(ks_ref[pl.ds(r0, 256), :].astype(F32), kng_ref[1:2, :]).astype(BF16)
            kwn_scr[pl.ds(r0, 256), :] = _rms_rows(kw_ref[pl.ds(r0, 256), :].astype(F32), kng_ref[2:3, :]).astype(BF16)
            return carry

        lax.fori_loop(0, seq // 256, body, 0)

    q4 = jnp.concatenate([qn_ref[:, r * Dh:(r + 1) * Dh] for r in range(R)], axis=0)
    row = lax.broadcasted_iota(I32, (R * QB, 1), 0)
    qpos = qb * QB + (row & (QB - 1))
    slope = slope_ref[0]

    def attend(k, v, kpos, valid_fn):
        s = lax.dot_general(q4, k, (((1,), (1,)), ((), ())), preferred_element_type=F32) * scale
        dist = qpos - kpos
        s = jnp.where(valid_fn(dist), s - slope * dist.astype(F32), NEG_INF)
        p = jnp.exp(s - jnp.max(s, axis=-1, keepdims=True))
        o = jnp.dot(p.astype(BF16), v, preferred_element_type=F32)
        return o / jnp.sum(p, axis=-1, keepdims=True)

    base = ((b * NSA_KV_HEADS + g) * nqb + qb) * SEL_TOPK
    nk = SEL_TOPK * QB
    key_lane = lax.broadcasted_iota(I32, (1, nk), 1)
    kpos_s = key_lane & (QB - 1)
    ks, vs = [], []
    for k in range(SEL_TOPK):
        start = pl.multiple_of(idx_ref[base + k] * QB, QB)
        ks.append(ksn_scr[pl.ds(start, QB), :])
        vs.append(vs_ref[pl.ds(start, QB), :])
        kpos_s = kpos_s + jnp.where((key_lane >= k * QB) & (key_lane < (k + 1) * QB), start, 0)
    o_sel = attend(jnp.concatenate(ks, axis=0), jnp.concatenate(vs, axis=0), kpos_s, lambda d: d >= 0)

    span = WINDOW + QB
    wstart = pl.multiple_of(jnp.maximum(qb * QB - WINDOW, 0), QB)
    kpos_w = wstart + lax.broadcasted_iota(I32, (1, span), 1)
    o_win = attend(kwn_scr[pl.ds(wstart, span), :], vw_ref[pl.ds(wstart, span), :], kpos_w,
                   lambda d: (d >= 0) & (d < WINDOW))

    o_cmp = jnp.concatenate([ocmp_ref[:, r * Dh:(r + 1) * Dh] for r in range(R)], axis=0).astype(F32)
    sig = _sigmoid(gate_ref[...])
    lane = lax.broadcasted_iota(I32, (QB, LANES), 1)

    def gate(br):
        cols = [jnp.sum(jnp.where(lane == GATE_LANE0 + br * NSA_HEADS + g * R + r, sig, 0.0), axis=-1, keepdims=True)
                for r in range(R)]
        return jnp.concatenate(cols, axis=0)

    o = gate(0) * o_cmp + gate(1) * o_sel + gate(2) * o_win
    for r in range(R):
        out_ref[:, r * Dh:(r + 1) * Dh] = o[r * QB:(r + 1) * QB].astype(BF16)


def _nsa_attn(sel_flat, qn, proj, ocmp, small, kng, slope_rows, bsz, seq):
    G = NSA_KV_HEADS
    Dh = NSA_HEAD_DIM
    QB = SEL_BLOCK
    nqb = seq // QB
    qw = NSA_GROUP * Dh
    kv0 = KV_OFF // Dh

    def kv_spec(j):
        return pl.BlockSpec((seq, Dh), lambda b, g, q, idx: (b, kv0 + j * G + g))

    grid_spec = pltpu.PrefetchScalarGridSpec(
        num_scalar_prefetch=1,
        grid=(bsz, G, nqb),
        in_specs=[
            pl.BlockSpec((QB, qw), lambda b, g, q, idx: (b * nqb + q, g)),
            kv_spec(2), kv_spec(3), kv_spec(4), kv_spec(5),
            pl.BlockSpec((QB, qw), lambda b, g, q, idx: (b * nqb + q, g)),
            pl.BlockSpec((QB, SMALL_W), lambda b, g, q, idx: (b * nqb + q, 0)),
            pl.BlockSpec((N_BRANCH, Dh), lambda b, g, q, idx: (0, 0)),
            pl.BlockSpec((1, NSA_GROUP * QB, 1), lambda b, g, q, idx: (g, 0, 0)),
        ],
        out_specs=pl.BlockSpec((QB, qw), lambda b, g, q, idx: (b * nqb + q, g)),
        scratch_shapes=[pltpu.VMEM((seq, Dh), BF16), pltpu.VMEM((seq, Dh), BF16)],
    )
    return pl.pallas_call(
        functools.partial(_nsa_attn_kernel, seq=seq),
        grid_spec=grid_spec,
        out_shape=jax.ShapeDtypeStruct((bsz * seq, NSA_HEADS * Dh), BF16),
        compiler_params=_params(("arbitrary", "arbitrary", "arbitrary")),
        name="nsa_attn",
    )(sel_flat, qn, proj, proj, proj, proj, ocmp, small, kng, slope_rows)


def _merge_kernel(yssm_ref, bg_ref, cg_ref, xi_ref, cgp_ref, xip_ref, ynsa_ref, scw_ref, w0_ref, w1_ref, w2_ref,
                  g0_ref, g1_ref, g2_ref, out_ref, ysc_scr, *, tm, tiles_per_seq):
    @pl.when(pl.program_id(1) == 0)
    def _():
        keep_prev = (pl.program_id(0) % tiles_per_seq > 0).astype(F32)
        chunk = 128
        tail = cgp_ref.shape[0]

        def body(i, carry):
            r0 = pl.multiple_of(i * chunk, chunk)
            u = cg_ref[pl.ds(r0, chunk), :].astype(F32) * xi_ref[pl.ds(r0, chunk), :].astype(F32)
            p0 = pl.multiple_of(jnp.maximum(r0 - tail, 0), tail)
            prev_in = cg_ref[pl.ds(p0, tail), :].astype(F32) * xi_ref[pl.ds(p0, tail), :].astype(F32)
            prev_out = cgp_ref[...].astype(F32) * xip_ref[...].astype(F32) * keep_prev
            prev = jnp.where(i > 0, prev_in, prev_out)
            acc = u * scw_ref[SC_CONV - 1:SC_CONV, :]
            for s in range(1, SC_CONV):
                k = SC_CONV - 1 - s
                acc = acc + _shift_rows(u, prev, s) * scw_ref[k:k + 1, :]
            ysc_scr[pl.ds(r0, chunk), :] = (bg_ref[pl.ds(r0, chunk), :].astype(F32) * acc).astype(BF16)
            return carry

        lax.fori_loop(0, tm // chunk, body, 0)

    a0 = jnp.dot(yssm_ref[...], w0_ref[...], preferred_element_type=F32)
    a1 = jnp.dot(ysc_scr[...], w1_ref[...], preferred_element_type=F32)
    a2 = jnp.dot(ynsa_ref[...], w2_ref[...], preferred_element_type=F32)
    mixed = (_sigmoid(g0_ref[...].astype(F32)) * a0 + _sigmoid(g1_ref[...].astype(F32)) * a1
             + _sigmoid(g2_ref[...].astype(F32)) * a2)
    out_ref[...] = mixed.astype(BF16)


def _merge(yssm, proj, ynsa, sc_w, w_ssm, w_sc, w_nsa, seq, tm=512, tn=512):
    m = yssm.shape[0]
    tm = min(tm, seq)
    tail = 16
    per = tm // tail
    sc0 = SC_OFF // D_MODEL
    mg0 = MG_OFF // tn
    ntn = D_MODEL // tn
    row_full = lambda i, j: (i, 0)
    wspec = pl.BlockSpec((D_MODEL, tn), lambda i, j: (0, j))
    return pl.pallas_call(
        functools.partial(_merge_kernel, tm=tm, tiles_per_seq=seq // tm),
        grid=(m // tm, ntn),
        in_specs=[
            pl.BlockSpec((tm, D_MODEL), row_full),
            pl.BlockSpec((tm, D_MODEL), lambda i, j: (i, sc0)),
            pl.BlockSpec((tm, D_MODEL), lambda i, j: (i, sc0 + 1)),
            pl.BlockSpec((tm, D_MODEL), lambda i, j: (i, sc0 + 2)),
            pl.BlockSpec((tail, D_MODEL), lambda i, j: (jnp.maximum(i * per - 1, 0), sc0 + 1)),
            pl.BlockSpec((tail, D_MODEL), lambda i, j: (jnp.maximum(i * per - 1, 0), sc0 + 2)),
            pl.BlockSpec((tm, D_MODEL), row_full),
            pl.BlockSpec((SC_CONV, D_MODEL), lambda i, j: (0, 0)),
            wspec, wspec, wspec,
            pl.BlockSpec((tm, tn), lambda i, j: (i, mg0 + j)),
            pl.BlockSpec((tm, tn), lambda i, j: (i, mg0 + ntn + j)),
            pl.BlockSpec((tm, tn), lambda i, j: (i, mg0 + 2 * ntn + j)),
        ],
        out_specs=pl.BlockSpec((tm, tn), lambda i, j: (i, j)),
        out_shape=jax.ShapeDtypeStruct((m, D_MODEL), BF16),
        scratch_shapes=[pltpu.VMEM((tm, D_MODEL), BF16)],
        compiler_params=_params(("arbitrary", "arbitrary")),
        name="merge",
    )(yssm, proj, proj, proj, proj, proj, ynsa, sc_w, w_ssm, w_sc, w_nsa, proj, proj, proj)


def _outproj_kernel(a_ref, w_ref, x_ref, out_ref):
    out_ref[...] = x_ref[...] + jnp.dot(a_ref[...], w_ref[...], preferred_element_type=F32)


def _outproj(a, w, x2, tm=1024, tn=512):
    m = a.shape[0]
    tm = min(tm, m)
    return pl.pallas_call(
        _outproj_kernel,
        grid=(m // tm, D_MODEL // tn),
        in_specs=[
            pl.BlockSpec((tm, D_MODEL), lambda i, j: (i, 0)),
            pl.BlockSpec((D_MODEL, tn), lambda i, j: (0, j)),
            pl.BlockSpec((tm, tn), lambda i, j: (i, j)),
        ],
        out_specs=pl.BlockSpec((tm, tn), lambda i, j: (i, j)),
        out_shape=jax.ShapeDtypeStruct((m, D_MODEL), F32),
        compiler_params=_params(("arbitrary", "arbitrary")),
        name="outproj",
    )(a, w, x2)


def _ffn_kernel(x_ref, g_ref, wg_ref, wu_ref, wd_ref, out_ref, h_scr, *, tm):
    @pl.when(pl.program_id(1) == 0)
    def _():
        _norm_to_scratch(x_ref, g_ref, h_scr, tm)
        out_ref[...] = x_ref[...]

    h = h_scr[...]
    a = jnp.dot(h, wg_ref[...], preferred_element_type=F32)
    u = jnp.dot(h, wu_ref[...], preferred_element_type=F32)
    out_ref[...] += jnp.dot((_silu(a) * u).astype(BF16), wd_ref[...], preferred_element_type=F32)


def _ffn(x2, g, wg, wu, wd, tm=512, tf=512):
    m = x2.shape[0]
    tm = min(tm, m)
    return pl.pallas_call(
        functools.partial(_ffn_kernel, tm=tm),
        grid=(m // tm, D_FF // tf),
        in_specs=[
            pl.BlockSpec((tm, D_MODEL), lambda i, f: (i, 0)),
            pl.BlockSpec((1, D_MODEL), lambda i, f: (0, 0)),
            pl.BlockSpec((D_MODEL, tf), lambda i, f: (0, f)),
            pl.BlockSpec((D_MODEL, tf), lambda i, f: (0, f)),
            pl.BlockSpec((tf, D_MODEL), lambda i, f: (f, 0)),
        ],
        out_specs=pl.BlockSpec((tm, D_MODEL), lambda i, f: (i, 0)),
        out_shape=jax.ShapeDtypeStruct((m, D_MODEL), F32),
        scratch_shapes=[pltpu.VMEM((tm, D_MODEL), BF16)],
        compiler_params=_params(("arbitrary", "arbitrary")),
        name="ffn",
    )(x2, g, wg, wu, wd)


def _route_kernel(x_ref, g_ref, wr_ref, h_ref, idx_ref, wgt_ref):
    h = _rms_rows(x_ref[...], g_ref[...])
    h_ref[...] = h.astype(BF16)
    logits = jnp.dot(h, wr_ref[...], precision=lax.Precision.HIGHEST, preferred_element_type=F32)
    lane = lax.broadcasted_iota(I32, logits.shape, 1)
    logits = jnp.where(lane < N_EXPERTS, logits, -jnp.inf)
    v0 = jnp.max(logits, axis=-1, keepdims=True)
    i0 = jnp.min(jnp.where(logits == v0, lane, LANES), axis=-1, keepdims=True)
    rest = jnp.where(lane == i0, -jnp.inf, logits)
    v1 = jnp.max(rest, axis=-1, keepdims=True)
    i1 = jnp.min(jnp.where(rest == v1, lane, LANES), axis=-1, keepdims=True)
    e1 = jnp.exp(v1 - v0)
    w0 = 1.0 / (1.0 + e1)
    idx_ref[...] = jnp.where(lane == 0, i0, jnp.where(lane == 1, i1, 0))
    wgt_ref[...] = jnp.where(lane == 0, w0, jnp.where(lane == 1, e1 * w0, 0.0))


def _route(x2, g, w_router, tm=256):
    m = x2.shape[0]
    tm = min(tm, m)
    return pl.pallas_call(
        _route_kernel,
        grid=(m // tm,),
        in_specs=[
            pl.BlockSpec((tm, D_MODEL), lambda i: (i, 0)),
            pl.BlockSpec((1, D_MODEL), lambda i: (0, 0)),
            pl.BlockSpec((D_MODEL, LANES), lambda i: (0, 0)),
        ],
        out_specs=[
            pl.BlockSpec((tm, D_MODEL), lambda i: (i, 0)),
            pl.BlockSpec((tm, LANES), lambda i: (i, 0)),
            pl.BlockSpec((tm, LANES), lambda i: (i, 0)),
        ],
        out_shape=[
            jax.ShapeDtypeStruct((m, D_MODEL), BF16),
            jax.ShapeDtypeStruct((m, LANES), I32),
            jax.ShapeDtypeStruct((m, LANES), F32),
        ],
        compiler_params=_params(("arbitrary",)),
        name="route",
    )(x2, g, w_router)


def _gather_kernel(src_ref, in_ref, out_ref, sem, *, rows):
    base = pl.program_id(0) * rows

    def start(r, carry):
        pltpu.make_async_copy(in_ref.at[src_ref[base + r]], out_ref.at[base + r], sem).start()
        return carry

    lax.fori_loop(0, rows, start, 0)
    done = out_ref.at[pl.ds(base, rows)]
    pltpu.make_async_copy(done, done, sem).wait()


def _gather_rows(src, table, n_out, rows=GATHER_ROWS):
    width = table.shape[1]
    grid_spec = pltpu.PrefetchScalarGridSpec(
        num_scalar_prefetch=1,
        grid=(n_out // rows,),
        in_specs=[pl.BlockSpec(memory_space=pl.ANY)],
        out_specs=pl.BlockSpec(memory_space=pl.ANY),
        scratch_shapes=[pltpu.SemaphoreType.DMA(())],
    )
    out = pl.pallas_call(
        functools.partial(_gather_kernel, rows=rows),
        grid_spec=grid_spec,
        out_shape=jax.ShapeDtypeStruct((n_out, width // LANES, LANES), table.dtype),
        compiler_params=pltpu.CompilerParams(dimension_semantics=("arbitrary",), has_side_effects=True),
        name="gather_rows",
    )(src, table.reshape(table.shape[0], width // LANES, LANES))
    return out.reshape(n_out, width)


def _gffn_kernel(te_ref, tv_ref, hs_ref, rw_ref, wg_ref, wu_ref, wd_ref, out_ref, acc_scr):
    i = pl.program_id(0)
    f = pl.program_id(1)
    nf = pl.num_programs(1)

    @pl.when(f == 0)
    def _():
        acc_scr[...] = jnp.zeros_like(acc_scr)

    @pl.when(tv_ref[i] > 0)
    def _():
        h = hs_ref[...]
        a = jnp.dot(h, wg_ref[0], preferred_element_type=F32)
        u = jnp.dot(h, wu_ref[0], preferred_element_type=F32)
        acc_scr[...] += jnp.dot((_silu(a) * u).astype(BF16), wd_ref[0], preferred_element_type=F32)

    @pl.when(f == nf - 1)
    def _():
        out_ref[...] = (acc_scr[...] * rw_ref[...]).astype(BF16)


def _gffn(tile_expert, tile_valid, hs, row_w, wg, wu, wd, tm=MOE_TILE, tf=512):
    p = hs.shape[0]
    grid_spec = pltpu.PrefetchScalarGridSpec(
        num_scalar_prefetch=2,
        grid=(p // tm, D_FF // tf),
        in_specs=[
            pl.BlockSpec((tm, D_MODEL), lambda i, f, te, tv: (i, 0)),
            pl.BlockSpec((tm, 1), lambda i, f, te, tv: (i, 0)),
            pl.BlockSpec((1, D_MODEL, tf), lambda i, f, te, tv: (te[i], 0, f)),
            pl.BlockSpec((1, D_MODEL, tf), lambda i, f, te, tv: (te[i], 0, f)),
            pl.BlockSpec((1, tf, D_MODEL), lambda i, f, te, tv: (te[i], f, 0)),
        ],
        out_specs=pl.BlockSpec((tm, D_MODEL), lambda i, f, te, tv: (i, 0)),
        scratch_shapes=[pltpu.VMEM((tm, D_MODEL), F32)],
    )
    return pl.pallas_call(
        _gffn_kernel,
        grid_spec=grid_spec,
        out_shape=jax.ShapeDtypeStruct((p, D_MODEL), BF16),
        compiler_params=_params(("arbitrary", "arbitrary")),
        name="grouped_ffn",
    )(tile_expert, tile_valid, hs, row_w, wg, wu, wd)


def _combine_kernel(x_ref, y_ref, out_ref):
    out_ref[...] = x_ref[...] + y_ref[:, :D_MODEL].astype(F32) + y_ref[:, D_MODEL:].astype(F32)


def _combine(x2, y_pairs, tm=512):
    m = x2.shape[0]
    tm = min(tm, m)
    return pl.pallas_call(
        _combine_kernel,
        grid=(m // tm,),
        in_specs=[
            pl.BlockSpec((tm, D_MODEL), lambda i: (i, 0)),
            pl.BlockSpec((tm, TOP_K * D_MODEL), lambda i: (i, 0)),
        ],
        out_specs=pl.BlockSpec((tm, D_MODEL), lambda i: (i, 0)),
        out_shape=jax.ShapeDtypeStruct((m, D_MODEL), F32),
        compiler_params=_params(("arbitrary",)),
        name="combine",
    )(x2, y_pairs)


def _moe(x2, g, w_router, wg, wu, wd):
    m = x2.shape[0]
    tm = MOE_TILE
    h, top_idx, top_w = _route(x2, g, w_router)
    e_flat = top_idx[:, :TOP_K].reshape(-1)
    w_flat = top_w[:, :TOP_K].reshape(-1)
    onehot = (e_flat[:, None] == jnp.arange(N_EXPERTS, dtype=I32)[None, :]).astype(I32)
    ranks = jnp.cumsum(onehot, axis=0) - onehot
    rank = jnp.sum(ranks * onehot, axis=1)
    counts = jnp.sum(onehot, axis=0)
    padded = ((counts + tm - 1) // tm) * tm
    seg_end = jnp.cumsum(padded)
    seg_start = seg_end - padded
    pos = seg_start[e_flat] + rank
    p_max = TOP_K * m + N_EXPERTS * tm
    n_tiles = p_max // tm
    src = jnp.zeros((p_max,), I32).at[pos].set(jnp.arange(TOP_K * m, dtype=I32) // TOP_K)
    row_w = jnp.zeros((p_max,), F32).at[pos].set(w_flat).reshape(p_max, 1)
    tile_start = jnp.arange(n_tiles, dtype=I32) * tm
    tile_expert = jnp.minimum(jnp.sum((tile_start[:, None] >= seg_end[None, :]).astype(I32), axis=1), N_EXPERTS - 1)
    tile_valid = (tile_start < seg_end[-1]).astype(I32)

    hs = _gather_rows(src, h, p_max)
    ys = _gffn(tile_expert, tile_valid, hs, row_w, wg, wu, wd)
    y_pairs = _gather_rows(pos, ys, TOP_K * m).reshape(m, TOP_K * D_MODEL)
    return _combine(x2, y_pairs)


def _pad_row(v, width):
    return jnp.pad(v.astype(F32), (0, width - v.shape[0])).reshape(1, width)


def _alibi_tables():
    h = jnp.arange(1, NSA_HEADS + 1, dtype=F32)
    slopes = jnp.exp2(-8.0 * h / NSA_HEADS).reshape(NSA_KV_HEADS, NSA_GROUP)
    lanes_tbl = jnp.broadcast_to(jnp.pad(slopes, ((0, 0), (0, 8 - NSA_GROUP)))[:, :, None], (NSA_KV_HEADS, 8, LANES))
    rows_tbl = jnp.repeat(slopes, SEL_BLOCK, axis=1)[:, :, None]
    return lanes_tbl, rows_tbl


def _mixer_layer(x2, bsz, seq, norm_mix, w_in, ssm_conv_w, ssm_conv_b, ssm_dt_bias, ssm_a_log, ssm_d, ssm_norm,
                 w_ssm_out, sc_conv_w, w_sc_out, q_norm, k_norm, cmp_pos, w_cmp_k, w_cmp_v, w_nsa_out, w_out):
    w_big = jnp.concatenate(
        [w_in[:, _SRC_XBC:_SRC_DT], w_in[:, _SRC_Z:_SRC_XBC], w_in[:, _SRC_SC:_SRC_NG], w_in[:, _SRC_MG:]],
        axis=1).astype(BF16)
    w_small = jnp.concatenate(
        [w_in[:, _SRC_DT:_SRC_SC], w_in[:, _SRC_NG:_SRC_MG],
         jnp.zeros((D_MODEL, SMALL_W - SSM_HEADS - N_BRANCH * NSA_HEADS), F32)], axis=1).astype(BF16)
    proj, small = _inproj(x2, norm_mix.reshape(1, D_MODEL), w_big, w_small)

    y_ssm = _ssd(proj, small, ssm_conv_w, ssm_conv_b.reshape(1, SSM_XBC), _pad_row(ssm_dt_bias, SMALL_W),
                 _pad_row(ssm_a_log, SMALL_W), jnp.repeat(ssm_d, SSM_HEAD_DIM).reshape(1, D_MODEL),
                 ssm_norm.reshape(1, D_MODEL), bsz, seq)

    G, Dh = NSA_KV_HEADS, NSA_HEAD_DIM
    nb = seq // CMP_STRIDE
    r5 = proj[:, KV_OFF:KV_OFF + 2 * G * Dh].reshape(bsz, nb, CMP_STRIDE, 2, G, Dh)
    r5 = r5.transpose(0, 4, 3, 1, 2, 5).reshape(bsz, G, 2, nb, CMP_STRIDE * Dh)
    lanes_tbl, rows_tbl = _alibi_tables()
    o_cmp, qn, sel = _nsa_cmp(r5, proj, w_cmp_k.reshape(CMP_LEN, Dh, Dh).astype(BF16),
                              w_cmp_v.reshape(CMP_LEN, Dh, Dh).astype(BF16), cmp_pos, k_norm[0:1], q_norm.reshape(1, Dh),
                              lanes_tbl, bsz, seq)
    sel_flat = sel[..., :SEL_TOPK].reshape(-1)
    y_nsa = _nsa_attn(sel_flat, qn, proj, o_cmp, small, k_norm, rows_tbl, bsz, seq)

    mixed = _merge(y_ssm, proj, y_nsa, sc_conv_w, w_ssm_out.astype(BF16), w_sc_out.astype(BF16),
                   w_nsa_out.astype(BF16), seq)
    return _outproj(mixed, w_out.astype(BF16), x2)


def kernel(x, norm_mix, w_in, ssm_conv_w, ssm_conv_b, ssm_dt_bias, ssm_a_log, ssm_d, ssm_norm, w_ssm_out,
           sc_conv_w, w_sc_out, q_norm, k_norm, cmp_pos, w_cmp_k, w_cmp_v, w_nsa_out, w_out, norm_ffn,
           ffn_w_gate, ffn_w_up, ffn_w_down, moe_router, moe_w_gate, moe_w_up, moe_w_down):
    bsz, seq, _ = x.shape
    depth = norm_mix.shape[0]
    x2 = x.reshape(bsz * seq, D_MODEL)
    for layer in range(depth):
        x2 = _mixer_layer(x2, bsz, seq, norm_mix[layer], w_in[layer], ssm_conv_w[layer], ssm_conv_b[layer],
                          ssm_dt_bias[layer], ssm_a_log[layer], ssm_d[layer], ssm_norm[layer], w_ssm_out[layer],
                          sc_conv_w[layer], w_sc_out[layer], q_norm[layer], k_norm[layer], cmp_pos[layer],
                          w_cmp_k[layer], w_cmp_v[layer], w_nsa_out[layer], w_out[layer])
        g = norm_ffn[layer].reshape(1, D_MODEL)
        i = layer // 2
        if layer % 2 == 0:
            x2 = _ffn(x2, g, ffn_w_gate[i].astype(BF16), ffn_w_up[i].astype(BF16), ffn_w_down[i].astype(BF16))
        else:
            w_router = jnp.pad(moe_router[i], ((0, 0), (0, LANES - N_EXPERTS)))
            x2 = _moe(x2, g, w_router, moe_w_gate[i].astype(BF16), moe_w_up[i].astype(BF16),
                      moe_w_down[i].astype(BF16))
    return x2.reshape(bsz, seq, D_MODEL)
```

```python
import functools

import jax
import jax.numpy as jnp
from jax import lax
from jax.experimental import pallas as pl
from jax.experimental.pallas import tpu as pltpu

F32 = jnp.float32
BF16 = jnp.bfloat16
I32 = jnp.int32

D_MODEL = 2048
EPS = 1e-6
NEG_INF = -1e30
FORCED_SCORE = 1e4
SSM_HEAD_DIM = 64
SSM_HEADS = D_MODEL // SSM_HEAD_DIM
SSM_GROUPS = 8
SSM_HEADS_PER_GROUP = SSM_HEADS // SSM_GROUPS
SSM_STATE = 128
SSM_CONV = 4
SSM_CHUNK = 128
SSM_XBC = D_MODEL + 2 * SSM_GROUPS * SSM_STATE
SC_CONV = 3
NSA_HEADS = 16
NSA_HEAD_DIM = 128
NSA_KV_HEADS = 4
NSA_GROUP = NSA_HEADS // NSA_KV_HEADS
CMP_STRIDE = 16
CMP_LEN = 2 * CMP_STRIDE
SEL_BLOCK = 64
SEL_TOPK = 8
WINDOW = 512
N_BRANCH = 3
D_FF = 5632
N_EXPERTS = 8
TOP_K = 2

XBC_OFF = 0
Z_OFF = XBC_OFF + SSM_XBC
SC_OFF = Z_OFF + D_MODEL
Q_OFF = SC_OFF + 3 * D_MODEL
KV_OFF = Q_OFF + NSA_HEADS * NSA_HEAD_DIM
MG_OFF = KV_OFF + 6 * NSA_KV_HEADS * NSA_HEAD_DIM
PROJ_W = MG_OFF + N_BRANCH * D_MODEL
LANES = 128
SMALL_W = LANES
GATE_LANE0 = SSM_HEADS
_SRC_Z = 0
_SRC_XBC = D_MODEL
_SRC_DT = _SRC_XBC + SSM_XBC
_SRC_SC = _SRC_DT + SSM_HEADS
_SRC_Q = _SRC_SC + 3 * D_MODEL
_SRC_KV = _SRC_Q + NSA_HEADS * NSA_HEAD_DIM
_SRC_NG = _SRC_KV + 6 * NSA_KV_HEADS * NSA_HEAD_DIM
_SRC_MG = _SRC_NG + N_BRANCH * NSA_HEADS

VMEM_LIMIT = 56 * 1024 * 1024
MOE_TILE = 512
GATHER_ROWS = 512
NSA_Q_BLOCKS_PER_STEP = 4


def _params(sem, vmem=VMEM_LIMIT):
    return pltpu.CompilerParams(dimension_semantics=sem, vmem_limit_bytes=vmem)


def _sigmoid(x):
    return 1.0 / (1.0 + jnp.exp(-x))


def _silu(x):
    return x * _sigmoid(x)


def _rms_rows(x, g):
    return x * lax.rsqrt(jnp.mean(x * x, axis=-1, keepdims=True) + EPS) * g


def _norm_to_scratch(x_ref, g_ref, h_scr, rows, chunk=256):
    g = g_ref[...]

    def body(i, carry):
        r0 = pl.multiple_of(i * chunk, chunk)
        h_scr[pl.ds(r0, chunk), :] = _rms_rows(x_ref[pl.ds(r0, chunk), :], g).astype(BF16)
        return carry

    lax.fori_loop(0, rows // chunk, body, 0)


def _inproj_kernel(x_ref, g_ref, w_ref, ws_ref, proj_ref, small_ref, h_scr, *, tm):
    @pl.when(pl.program_id(1) == 0)
    def _():
        _norm_to_scratch(x_ref, g_ref, h_scr, tm)
        small_ref[...] = jnp.dot(h_scr[...], ws_ref[...], preferred_element_type=F32)

    proj_ref[...] = jnp.dot(h_scr[...], w_ref[...], preferred_element_type=F32).astype(BF16)


def _inproj(x2, g, w_big, w_small, tm=1024, tn=512):
    m = x2.shape[0]
    tm = min(tm, m)
    return pl.pallas_call(
        functools.partial(_inproj_kernel, tm=tm),
        grid=(m // tm, PROJ_W // tn),
        in_specs=[
            pl.BlockSpec((tm, D_MODEL), lambda i, j: (i, 0)),
            pl.BlockSpec((1, D_MODEL), lambda i, j: (0, 0)),
            pl.BlockSpec((D_MODEL, tn), lambda i, j: (0, j)),
            pl.BlockSpec((D_MODEL, SMALL_W), lambda i, j: (0, 0)),
        ],
        out_specs=[
            pl.BlockSpec((tm, tn), lambda i, j: (i, j)),
            pl.BlockSpec((tm, SMALL_W), lambda i, j: (i, 0)),
        ],
        out_shape=[
            jax.ShapeDtypeStruct((m, PROJ_W), BF16),
            jax.ShapeDtypeStruct((m, SMALL_W), F32),
        ],
        scratch_shapes=[pltpu.VMEM((tm, D_MODEL), BF16)],
        compiler_params=_params(("arbitrary", "arbitrary")),
        name="inproj",
    )(x2, g, w_big, w_small)


def _shift_rows(cur, prev_tail, s):
    rc = pltpu.roll(cur, s, 0)
    rp = pltpu.roll(prev_tail, s, 0)
    row = lax.broadcasted_iota(I32, (8, cur.shape[1]), 0)
    top = jnp.where(row < s, rp[0:8], rc[0:8])
    return jnp.concatenate([top, rc[8:]], axis=0)


def _ssd_kernel(xbc_ref, prev_ref, z_ref, small_ref, cw_ref, cb_ref, dtb_ref, alog_ref, dskip_ref,
                ng_ref, out_ref, state_scr, y_scr):
    L = SSM_CHUNK
    P = SSM_HEAD_DIM
    R = SSM_HEADS_PER_GROUP
    GW = R * P
    c = pl.program_id(1)

    @pl.when(c == 0)
    def _():
        state_scr[...] = jnp.zeros_like(state_scr)

    keep_prev = (c > 0).astype(F32)

    def conv_silu(lo, w):
        cur = xbc_ref[:, lo:lo + w].astype(F32)
        prev = prev_ref[:, lo:lo + w].astype(F32) * keep_prev
        acc = cur * cw_ref[SSM_CONV - 1:SSM_CONV, lo:lo + w] + cb_ref[:, lo:lo + w]
        for s in range(1, SSM_CONV):
            k = SSM_CONV - 1 - s
            acc = acc + _shift_rows(cur, prev, s) * cw_ref[k:k + 1, lo:lo + w]
        return _silu(acc)

    row = lax.broadcasted_iota(I32, (L, L), 0)
    col = lax.broadcasted_iota(I32, (L, L), 1)
    causal = row >= col
    tril = causal.astype(F32)

    pre = small_ref[...] + dtb_ref[...]
    dt = jnp.maximum(pre, 0.0) + jnp.log(1.0 + jnp.exp(-jnp.abs(pre)))
    dt = jnp.where(col < SSM_HEADS, dt, 0.0)
    a_neg = -jnp.exp(alog_ref[...])
    a_cum = jnp.dot(tril, dt * a_neg, precision=lax.Precision.HIGHEST, preferred_element_type=F32)
    a_cum_t = a_cum.T
    a_last = a_cum[L - 1:L, :]
    dec_end = jnp.exp(a_last - a_cum)
    dec_in = jnp.exp(a_cum)
    chunk_dec = jnp.exp(a_last)

    lane_g = lax.broadcasted_iota(I32, (L, GW), 1)
    lane_g1 = lax.broadcasted_iota(I32, (1, GW), 1)

    def expand(mat, g, lanes):
        out = mat[:, R * g + R - 1:R * g + R]
        for r in range(R - 2, -1, -1):
            out = jnp.where(lanes < (r + 1) * P, mat[:, R * g + r:R * g + r + 1], out)
        return out

    for g in range(SSM_GROUPS):
        xs = conv_silu(g * GW, GW)
        b_in = conv_silu(D_MODEL + g * SSM_STATE, SSM_STATE)
        c_out = conv_silu(D_MODEL + (SSM_GROUPS + g) * SSM_STATE, SSM_STATE)
        b_bf = b_in.astype(BF16)
        c_bf = c_out.astype(BF16)
        cb = lax.dot_general(c_bf, b_bf, (((1,), (1,)), ((), ())), preferred_element_type=F32)
        xdt = xs * expand(dt, g, lane_g)
        y = xs * dskip_ref[:, g * GW:(g + 1) * GW]
        for r in range(R):
            h = R * g + r
            seg = a_cum[:, h:h + 1] - a_cum_t[h:h + 1, :]
            lmat = jnp.where(causal, jnp.exp(jnp.minimum(seg, 0.0)), 0.0)
            head = (lane_g >= r * P) & (lane_g < (r + 1) * P)
            xdt_r = jnp.where(head, xdt, 0.0).astype(BF16)
            y = y + jnp.dot((cb * lmat).astype(BF16), xdt_r, preferred_element_type=F32)
        h_prev = state_scr[g]
        y = y + jnp.dot(c_bf, h_prev.astype(BF16), preferred_element_type=F32) * expand(dec_in, g, lane_g)
        xdt_end = (xdt * expand(dec_end, g, lane_g)).astype(BF16)
        new_state = jnp.dot(b_in.T.astype(BF16), xdt_end, preferred_element_type=F32)
        state_scr[g] = h_prev * expand(chunk_dec, g, lane_g1) + new_state
        y_scr[:, g * GW:(g + 1) * GW] = y

    yz = y_scr[...] * _silu(z_ref[...].astype(F32))
    out_ref[...] = _rms_rows(yz, ng_ref[...]).astype(BF16)


def _ssd(proj, small, conv_w, conv_b, dt_bias, a_log, d_skip, norm_g, bsz, seq):
    L = SSM_CHUNK
    nc = seq // L
    tail = 16
    per = L // tail
    return pl.pallas_call(
        _ssd_kernel,
        grid=(bsz, nc),
        in_specs=[
            pl.BlockSpec((L, SSM_XBC), lambda b, c: (b * nc + c, XBC_OFF // SSM_XBC)),
            pl.BlockSpec((tail, SSM_XBC), lambda b, c: (jnp.maximum((b * nc + c) * per - 1, 0), XBC_OFF // SSM_XBC)),
            pl.BlockSpec((L, D_MODEL), lambda b, c: (b * nc + c, Z_OFF // D_MODEL)),
            pl.BlockSpec((L, SMALL_W), lambda b, c: (b * nc + c, 0)),
            pl.BlockSpec((SSM_CONV, SSM_XBC), lambda b, c: (0, 0)),
            pl.BlockSpec((1, SSM_XBC), lambda b, c: (0, 0)),
            pl.BlockSpec((1, SMALL_W), lambda b, c: (0, 0)),
            pl.BlockSpec((1, SMALL_W), lambda b, c: (0, 0)),
            pl.BlockSpec((1, D_MODEL), lambda b, c: (0, 0)),
            pl.BlockSpec((1, D_MODEL), lambda b, c: (0, 0)),
        ],
        out_specs=pl.BlockSpec((L, D_MODEL), lambda b, c: (b * nc + c, 0)),
        out_shape=jax.ShapeDtypeStruct((bsz * seq, D_MODEL), BF16),
        scratch_shapes=[
            pltpu.VMEM((SSM_GROUPS, SSM_STATE, SSM_HEADS_PER_GROUP * SSM_HEAD_DIM), F32),
            pltpu.VMEM((L, D_MODEL), F32),
        ],
        compiler_params=_params(("arbitrary", "arbitrary")),
        name="ssd",
    )(proj, proj, proj, small, conv_w, conv_b, dt_bias, a_log, d_skip, norm_g)


def _nsa_cmp_kernel(r_ref, q_ref, wk_ref, wv_ref, pos_ref, kng_ref, qng_ref, slope_ref,
                    ocmp_ref, qn_ref, idx_ref, kc_scr, vc_scr, *, seq, tq):
    Dh = NSA_HEAD_DIM
    nb = seq // CMP_STRIDE
    nqb = seq // SEL_BLOCK

    def compress(j, w_ref):
        t = r_ref[0, 0, j].astype(F32)
        top = jnp.dot((t + pos_ref[j, 0:1, :]).astype(BF16), w_ref[0], preferred_element_type=F32)
        bot = jnp.dot((t + pos_ref[j, 1:2, :]).astype(BF16), w_ref[1], preferred_element_type=F32)
        return top + pltpu.roll(bot, nb - 1, 0)

    kc_scr[...] = _rms_rows(compress(0, wk_ref), kng_ref[...]).astype(BF16)
    vc_scr[...] = compress(1, wv_ref).astype(BF16)
    k_cmp = kc_scr[...]
    v_cmp = vc_scr[...]

    n_start = lax.broadcasted_iota(I32, (1, nb), 1) * CMP_STRIDE
    qg = qng_ref[...] * (Dh ** -0.5)

    def scores(qn, qpos, slope):
        s = lax.dot_general(qn, k_cmp, (((1,), (1,)), ((), ())), preferred_element_type=F32)
        dist = (qpos - n_start).astype(F32) - (CMP_LEN - 1) / 2
        valid = (n_start + (CMP_LEN - 1)) <= qpos
        s = jnp.where(valid, s - slope * dist, NEG_INF)
        p = jnp.exp(s - jnp.max(s, axis=-1, keepdims=True))
        p = p / jnp.sum(p, axis=-1, keepdims=True)
        return jnp.where(valid, p, 0.0)

    sel_row = lax.broadcasted_iota(I32, (nqb, seq), 0) * SEL_BLOCK
    sel_col = lax.broadcasted_iota(I32, (nqb, seq), 1)
    pick_first = jnp.where(sel_row == sel_col, 1.0, 0.0).astype(BF16)
    first_pos = lax.broadcasted_iota(I32, (nqb, 1), 0) * SEL_BLOCK
    p_first = jnp.zeros((nqb, nb), F32)

    slopes = [slope_ref[0, r:r + 1, 0:1] for r in range(NSA_GROUP)]

    def body(i, carry):
        r0 = pl.multiple_of(i * tq, tq)
        qpos = r0 + lax.broadcasted_iota(I32, (tq, 1), 0)
        for r in range(NSA_GROUP):
            qn = _rms_rows(q_ref[pl.ds(r0, tq), r * Dh:(r + 1) * Dh].astype(F32), qg).astype(BF16)
            qn_ref[pl.ds(r0, tq), r * Dh:(r + 1) * Dh] = qn
            p = scores(qn, qpos, slopes[r])
            ocmp_ref[pl.ds(r0, tq), r * Dh:(r + 1) * Dh] = jnp.dot(
                p.astype(BF16), v_cmp, preferred_element_type=F32).astype(BF16)
        return carry

    lax.fori_loop(0, seq // tq, body, 0)
    for r in range(NSA_GROUP):
        q_first = jnp.dot(pick_first, qn_ref[:, r * Dh:(r + 1) * Dh], preferred_element_type=F32).astype(BF16)
        p_first = p_first + scores(q_first, first_pos, slopes[r])

    n_lo = lax.broadcasted_iota(I32, (nb, nqb), 0) * CMP_STRIDE
    j_lo = lax.broadcasted_iota(I32, (nb, nqb), 1) * SEL_BLOCK
    overlap = jnp.maximum(jnp.minimum(n_lo + CMP_LEN, j_lo + SEL_BLOCK) - jnp.maximum(n_lo, j_lo), 0)
    overlap = overlap.astype(F32) / CMP_LEN
    imp = jnp.dot(p_first, overlap, precision=lax.Precision.HIGHEST, preferred_element_type=F32)
    qb_i = lax.broadcasted_iota(I32, (nqb, nqb), 0)
    blk_j = lax.broadcasted_iota(I32, (nqb, nqb), 1)
    forced = (blk_j == 0) | (blk_j == qb_i) | (blk_j == qb_i - 1)
    imp = jnp.where(forced, FORCED_SCORE, jnp.where(blk_j > qb_i, -FORCED_SCORE, imp))
    out_lane = lax.broadcasted_iota(I32, (nqb, LANES), 1)
    picked = jnp.zeros((nqb, LANES), I32)
    for k in range(SEL_TOPK):
        best = jnp.max(imp, axis=-1, keepdims=True)
        arg = jnp.min(jnp.where(imp == best, blk_j, nqb), axis=-1, keepdims=True)
        picked = jnp.where(out_lane == k, arg, picked)
        imp = jnp.where(blk_j == arg, -jnp.inf, imp)
    idx_ref[0, 0] = picked


def _nsa_cmp(r5, proj, wk, wv, pos, kng, qng, slopes, bsz, seq):
    G = NSA_KV_HEADS
    nb = seq // CMP_STRIDE
    nqb = seq // SEL_BLOCK
    qw = NSA_GROUP * NSA_HEAD_DIM
    return pl.pallas_call(
        functools.partial(_nsa_cmp_kernel, seq=seq, tq=min(512, seq)),
        grid=(bsz, G),
        in_specs=[
            pl.BlockSpec((1, 1, 2, nb, CMP_STRIDE * NSA_HEAD_DIM), lambda b, g: (b, g, 0, 0, 0)),
            pl.BlockSpec((seq, qw), lambda b, g: (b, Q_OFF // qw + g)),
            pl.BlockSpec((2, CMP_STRIDE * NSA_HEAD_DIM, NSA_HEAD_DIM), lambda b, g: (0, 0, 0)),
            pl.BlockSpec((2, CMP_STRIDE * NSA_HEAD_DIM, NSA_HEAD_DIM), lambda b, g: (0, 0, 0)),
            pl.BlockSpec((2, 2, CMP_STRIDE * NSA_HEAD_DIM), lambda b, g: (0, 0, 0)),
            pl.BlockSpec((1, NSA_HEAD_DIM), lambda b, g: (0, 0)),
            pl.BlockSpec((1, NSA_HEAD_DIM), lambda b, g: (0, 0)),
            pl.BlockSpec((1, 8, LANES), lambda b, g: (g, 0, 0)),
        ],
        out_specs=[
            pl.BlockSpec((seq, qw), lambda b, g: (b, g)),
            pl.BlockSpec((seq, qw), lambda b, g: (b, g)),
            pl.BlockSpec((1, 1, nqb, LANES), lambda b, g: (b, g, 0, 0)),
        ],
        out_shape=[
            jax.ShapeDtypeStruct((bsz * seq, NSA_HEADS * NSA_HEAD_DIM), BF16),
            jax.ShapeDtypeStruct((bsz * seq, NSA_HEADS * NSA_HEAD_DIM), BF16),
            jax.ShapeDtypeStruct((bsz, G, nqb, LANES), I32),
        ],
        scratch_shapes=[pltpu.VMEM((nb, NSA_HEAD_DIM), BF16), pltpu.VMEM((nb, NSA_HEAD_DIM), BF16)],
        compiler_params=_params(("arbitrary", "arbitrary")),
        name="nsa_cmp",
    )(r5, proj, wk, wv, pos, kng, qng, slopes)


def _nsa_attn_kernel(idx_ref, qn_ref, ks_ref, vs_ref, kw_ref, vw_ref, ocmp_ref, gate_ref, kng_ref, slope_ref,
                     out_ref, ksn_scr, kwn_scr, vwp_scr, wbias_scr, *, seq, qps):
    Dh = NSA_HEAD_DIM
    R = NSA_GROUP
    QB = SEL_BLOCK
    nqb = seq // QB
    span = WINDOW + QB
    b = pl.program_id(0)
    g = pl.program_id(1)
    step = pl.program_id(2)
    slope = slope_ref[0]
    row = lax.broadcasted_iota(I32, (R * QB, 1), 0)
    q_in = row & (QB - 1)

    @pl.when(step == 0)
    def _():
        kwn_scr[0:WINDOW, :] = jnp.zeros((WINDOW, Dh), BF16)
        vwp_scr[0:WINDOW, :] = jnp.zeros((WINDOW, Dh), BF16)

        def body(i, carry):
            r0 = pl.multiple_of(i * 256, 256)
            ksn_scr[pl.ds(r0, 256), :] = _rms_rows(ks_ref[pl.ds(r0, 256), :].astype(F32), kng_ref[1:2, :]).astype(BF16)
            kwn_scr[pl.ds(WINDOW + r0, 256), :] = _rms_rows(kw_ref[pl.ds(r0, 256), :].astype(F32),
                                                           kng_ref[2:3, :]).astype(BF16)
            vwp_scr[pl.ds(WINDOW + r0, 256), :] = vw_ref[pl.ds(r0, 256), :]
            return carry

        lax.fori_loop(0, seq // 256, body, 0)
        dist_w = q_in + WINDOW - lax.broadcasted_iota(I32, (R * QB, span), 1)
        wbias_scr[...] = jnp.where((dist_w >= 0) & (dist_w < WINDOW), -slope * dist_w.astype(F32), NEG_INF)

    def softmax_av(s, v):
        p = jnp.exp(s - jnp.max(s, axis=-1, keepdims=True))
        o = jnp.dot(p.astype(BF16), v, preferred_element_type=F32)
        return o / jnp.sum(p, axis=-1, keepdims=True)

    def qk(q4, k):
        return lax.dot_general(q4, k, (((1,), (1,)), ((), ())), preferred_element_type=F32)

    nk = SEL_TOPK * QB
    key_lane = lax.broadcasted_iota(I32, (1, nk), 1)
    win_lane = lax.broadcasted_iota(I32, (1, span), 1)

    for i in range(qps):
        qb = step * qps + i
        rows = slice(i * QB, (i + 1) * QB)
        q4 = jnp.concatenate([qn_ref[rows, r * Dh:(r + 1) * Dh] for r in range(R)], axis=0)

        base = ((b * NSA_KV_HEADS + g) * nqb + qb) * SEL_TOPK
        kpos_s = key_lane & (QB - 1)
        ks, vs = [], []
        for k in range(SEL_TOPK):
            start = pl.multiple_of(idx_ref[base + k] * QB, QB)
            ks.append(ksn_scr[pl.ds(start, QB), :])
            vs.append(vs_ref[pl.ds(start, QB), :])
            kpos_s = kpos_s + jnp.where((key_lane >= k * QB) & (key_lane < (k + 1) * QB), start, 0)
        dist_s = (q_in + qb * QB).astype(F32) - kpos_s.astype(F32)
        s_sel = jnp.where(dist_s >= 0, qk(q4, jnp.concatenate(ks, axis=0)) - slope * dist_s, NEG_INF)
        o_sel = softmax_av(s_sel, jnp.concatenate(vs, axis=0))

        wstart = pl.multiple_of(qb * QB, QB)
        before_start = jnp.where(win_lane >= WINDOW - qb * QB, 0.0, NEG_INF)
        s_win = qk(q4, kwn_scr[pl.ds(wstart, span), :]) + wbias_scr[...] + before_start
        o_win = softmax_av(s_win, vwp_scr[pl.ds(wstart, span), :])

        o_cmp = jnp.concatenate([ocmp_ref[rows, r * Dh:(r + 1) * Dh] for r in range(R)], axis=0).astype(F32)
        sig = _sigmoid(gate_ref[rows, :])
        lane = lax.broadcasted_iota(I32, (QB, LANES), 1)

        def gate(br):
            cols = [jnp.sum(jnp.where(lane == GATE_LANE0 + br * NSA_HEADS + g * R + r, sig, 0.0), axis=-1,
                            keepdims=True) for r in range(R)]
            return jnp.concatenate(cols, axis=0)

        o = gate(0) * o_cmp + gate(1) * o_sel + gate(2) * o_win
        for r in range(R):
            out_ref[rows, r * Dh:(r + 1) * Dh] = o[r * QB:(r + 1) * QB].astype(BF16)


def _nsa_attn(sel_flat, qn, proj, ocmp, small, kng, slope_rows, bsz, seq):
    G = NSA_KV_HEADS
    Dh = NSA_HEAD_DIM
    QB = SEL_BLOCK
    nqb = seq // QB
    qw = NSA_GROUP * Dh
    kv0 = KV_OFF // Dh

    def kv_spec(j):
        return pl.BlockSpec((seq, Dh), lambda b, g, q, idx: (b, kv0 + j * G + g))

    qps = NSA_Q_BLOCKS_PER_STEP
    steps = nqb // qps
    tq = qps * QB
    grid_spec = pltpu.PrefetchScalarGridSpec(
        num_scalar_prefetch=1,
        grid=(bsz, G, steps),
        in_specs=[
            pl.BlockSpec((tq, qw), lambda b, g, q, idx: (b * steps + q, g)),
            kv_spec(2), kv_spec(3), kv_spec(4), kv_spec(5),
            pl.BlockSpec((tq, qw), lambda b, g, q, idx: (b * steps + q, g)),
            pl.BlockSpec((tq, SMALL_W), lambda b, g, q, idx: (b * steps + q, 0)),
            pl.BlockSpec((N_BRANCH, Dh), lambda b, g, q, idx: (0, 0)),
            pl.BlockSpec((1, NSA_GROUP * QB, 1), lambda b, g, q, idx: (g, 0, 0)),
        ],
        out_specs=pl.BlockSpec((tq, qw), lambda b, g, q, idx: (b * steps + q, g)),
        scratch_shapes=[
            pltpu.VMEM((seq, Dh), BF16),
            pltpu.VMEM((WINDOW + seq, Dh), BF16),
            pltpu.VMEM((WINDOW + seq, Dh), BF16),
            pltpu.VMEM((NSA_GROUP * QB, WINDOW + QB), F32),
        ],
    )
    return pl.pallas_call(
        functools.partial(_nsa_attn_kernel, seq=seq, qps=qps),
        grid_spec=grid_spec,
        out_shape=jax.ShapeDtypeStruct((bsz * seq, NSA_HEADS * Dh), BF16),
        compiler_params=_params(("arbitrary", "arbitrary", "arbitrary")),
        name="nsa_attn",
    )(sel_flat, qn, proj, proj, proj, proj, ocmp, small, kng, slope_rows)


def _merge_kernel(yssm_ref, bg_ref, cg_ref, xi_ref, cgp_ref, xip_ref, ynsa_ref, scw_ref, w0_ref, w1_ref, w2_ref,
                  g0_ref, g1_ref, g2_ref, out_ref, ysc_scr, *, tm, tiles_per_seq):
    @pl.when(pl.program_id(1) == 0)
    def _():
        keep_prev = (pl.program_id(0) % tiles_per_seq > 0).astype(F32)
        chunk = 128
        tail = cgp_ref.shape[0]

        def body(i, carry):
            r0 = pl.multiple_of(i * chunk, chunk)
            u = cg_ref[pl.ds(r0, chunk), :].astype(F32) * xi_ref[pl.ds(r0, chunk), :].astype(F32)
            p0 = pl.multiple_of(jnp.maximum(r0 - tail, 0), tail)
            prev_in = cg_ref[pl.ds(p0, tail), :].astype(F32) * xi_ref[pl.ds(p0, tail), :].astype(F32)
            prev_out = cgp_ref[...].astype(F32) * xip_ref[...].astype(F32) * keep_prev
            prev = jnp.where(i > 0, prev_in, prev_out)
            acc = u * scw_ref[SC_CONV - 1:SC_CONV, :]
            for s in range(1, SC_CONV):
                k = SC_CONV - 1 - s
                acc = acc + _shift_rows(u, prev, s) * scw_ref[k:k + 1, :]
            ysc_scr[pl.ds(r0, chunk), :] = (bg_ref[pl.ds(r0, chunk), :].astype(F32) * acc).astype(BF16)
            return carry

        lax.fori_loop(0, tm // chunk, body, 0)

    a0 = jnp.dot(yssm_ref[...], w0_ref[...], preferred_element_type=F32)
    a1 = jnp.dot(ysc_scr[...], w1_ref[...], preferred_element_type=F32)
    a2 = jnp.dot(ynsa_ref[...], w2_ref[...], preferred_element_type=F32)
    mixed = (_sigmoid(g0_ref[...].astype(F32)) * a0 + _sigmoid(g1_ref[...].astype(F32)) * a1
             + _sigmoid(g2_ref[...].astype(F32)) * a2)
    out_ref[...] = mixed.astype(BF16)


def _merge(yssm, proj, ynsa, sc_w, w_ssm, w_sc, w_nsa, seq, tm=512, tn=512):
    m = yssm.shape[0]
    tm = min(tm, seq)
    tail = 16
    per = tm // tail
    sc0 = SC_OFF // D_MODEL
    mg0 = MG_OFF // tn
    ntn = D_MODEL // tn
    row_full = lambda i, j: (i, 0)
    wspec = pl.BlockSpec((D_MODEL, tn), lambda i, j: (0, j))
    return pl.pallas_call(
        functools.partial(_merge_kernel, tm=tm, tiles_per_seq=seq // tm),
        grid=(m // tm, ntn),
        in_specs=[
            pl.BlockSpec((tm, D_MODEL), row_full),
            pl.BlockSpec((tm, D_MODEL), lambda i, j: (i, sc0)),
            pl.BlockSpec((tm, D_MODEL), lambda i, j: (i, sc0 + 1)),
            pl.BlockSpec((tm, D_MODEL), lambda i, j: (i, sc0 + 2)),
            pl.BlockSpec((tail, D_MODEL), lambda i, j: (jnp.maximum(i * per - 1, 0), sc0 + 1)),
            pl.BlockSpec((tail, D_MODEL), lambda i, j: (jnp.maximum(i * per - 1, 0), sc0 + 2)),
            pl.BlockSpec((tm, D_MODEL), row_full),
            pl.BlockSpec((SC_CONV, D_MODEL), lambda i, j: (0, 0)),
            wspec, wspec, wspec,
            pl.BlockSpec((tm, tn), lambda i, j: (i, mg0 + j)),
            pl.BlockSpec((tm, tn), lambda i, j: (i, mg0 + ntn + j)),
            pl.BlockSpec((tm, tn), lambda i, j: (i, mg0 + 2 * ntn + j)),
        ],
        out_specs=pl.BlockSpec((tm, tn), lambda i, j: (i, j)),
        out_shape=jax.ShapeDtypeStruct((m, D_MODEL), BF16),
        scratch_shapes=[pltpu.VMEM((tm, D_MODEL), BF16)],
        compiler_params=_params(("arbitrary", "arbitrary")),
        name="merge",
    )(yssm, proj, proj, proj, proj, proj, ynsa, sc_w, w_ssm, w_sc, w_nsa, proj, proj, proj)


def _outproj_kernel(a_ref, w_ref, x_ref, out_ref):
    out_ref[...] = x_ref[...] + jnp.dot(a_ref[...], w_ref[...], preferred_element_type=F32)


def _outproj(a, w, x2, tm=1024, tn=512):
    m = a.shape[0]
    tm = min(tm, m)
    return pl.pallas_call(
        _outproj_kernel,
        grid=(m // tm, D_MODEL // tn),
        in_specs=[
            pl.BlockSpec((tm, D_MODEL), lambda i, j: (i, 0)),
            pl.BlockSpec((D_MODEL, tn), lambda i, j: (0, j)),
            pl.BlockSpec((tm, tn), lambda i, j: (i, j)),
        ],
        out_specs=pl.BlockSpec((tm, tn), lambda i, j: (i, j)),
        out_shape=jax.ShapeDtypeStruct((m, D_MODEL), F32),
        compiler_params=_params(("arbitrary", "arbitrary")),
        name="outproj",
    )(a, w, x2)


def _ffn_kernel(x_ref, g_ref, wg_ref, wu_ref, wd_ref, out_ref, h_scr, *, tm):
    @pl.when(pl.program_id(1) == 0)
    def _():
        _norm_to_scratch(x_ref, g_ref, h_scr, tm)
        out_ref[...] = x_ref[...]

    h = h_scr[...]
    a = jnp.dot(h, wg_ref[...], preferred_element_type=F32)
    u = jnp.dot(h, wu_ref[...], preferred_element_type=F32)
    out_ref[...] += jnp.dot((_silu(a) * u).astype(BF16), wd_ref[...], preferred_element_type=F32)


def _ffn(x2, g, wg, wu, wd, tm=512, tf=512):
    m = x2.shape[0]
    tm = min(tm, m)
    return pl.pallas_call(
        functools.partial(_ffn_kernel, tm=tm),
        grid=(m // tm, D_FF // tf),
        in_specs=[
            pl.BlockSpec((tm, D_MODEL), lambda i, f: (i, 0)),
            pl.BlockSpec((1, D_MODEL), lambda i, f: (0, 0)),
            pl.BlockSpec((D_MODEL, tf), lambda i, f: (0, f)),
            pl.BlockSpec((D_MODEL, tf), lambda i, f: (0, f)),
            pl.BlockSpec((tf, D_MODEL), lambda i, f: (f, 0)),
        ],
        out_specs=pl.BlockSpec((tm, D_MODEL), lambda i, f: (i, 0)),
        out_shape=jax.ShapeDtypeStruct((m, D_MODEL), F32),
        scratch_shapes=[pltpu.VMEM((tm, D_MODEL), BF16)],
        compiler_params=_params(("arbitrary", "arbitrary")),
        name="ffn",
    )(x2, g, wg, wu, wd)


def _route_kernel(x_ref, g_ref, wr_ref, h_ref, idx_ref, wgt_ref):
    h = _rms_rows(x_ref[...], g_ref[...])
    h_ref[...] = h.astype(BF16)
    logits = jnp.dot(h, wr_ref[...], precision=lax.Precision.HIGHEST, preferred_element_type=F32)
    lane = lax.broadcasted_iota(I32, logits.shape, 1)
    logits = jnp.where(lane < N_EXPERTS, logits, -jnp.inf)
    v0 = jnp.max(logits, axis=-1, keepdims=True)
    i0 = jnp.min(jnp.where(logits == v0, lane, LANES), axis=-1, keepdims=True)
    rest = jnp.where(lane == i0, -jnp.inf, logits)
    v1 = jnp.max(rest, axis=-1, keepdims=True)
    i1 = jnp.min(jnp.where(rest == v1, lane, LANES), axis=-1, keepdims=True)
    e1 = jnp.exp(v1 - v0)
    w0 = 1.0 / (1.0 + e1)
    idx_ref[...] = jnp.where(lane == 0, i0, jnp.where(lane == 1, i1, 0))
    wgt_ref[...] = jnp.where(lane == 0, w0, jnp.where(lane == 1, e1 * w0, 0.0))


def _route(x2, g, w_router, tm=256):
    m = x2.shape[0]
    tm = min(tm, m)
    return pl.pallas_call(
        _route_kernel,
        grid=(m // tm,),
        in_specs=[
            pl.BlockSpec((tm, D_MODEL), lambda i: (i, 0)),
            pl.BlockSpec((1, D_MODEL), lambda i: (0, 0)),
            pl.BlockSpec((D_MODEL, LANES), lambda i: (0, 0)),
        ],
        out_specs=[
            pl.BlockSpec((tm, D_MODEL), lambda i: (i, 0)),
            pl.BlockSpec((tm, LANES), lambda i: (i, 0)),
            pl.BlockSpec((tm, LANES), lambda i: (i, 0)),
        ],
        out_shape=[
            jax.ShapeDtypeStruct((m, D_MODEL), BF16),
            jax.ShapeDtypeStruct((m, LANES), I32),
            jax.ShapeDtypeStruct((m, LANES), F32),
        ],
        compiler_params=_params(("arbitrary",)),
        name="route",
    )(x2, g, w_router)


def _gather_kernel(src_ref, in_ref, out_ref, sem, *, rows):
    base = pl.program_id(0) * rows

    def start(r, carry):
        pltpu.make_async_copy(in_ref.at[src_ref[base + r]], out_ref.at[r], sem).start()
        return carry

    lax.fori_loop(0, rows, start, 0, unroll=8)
    pltpu.make_async_copy(in_ref.at[pl.ds(0, rows)], out_ref, sem).wait()


def _gather_rows(src, table, n_out, rows=GATHER_ROWS):
    width = table.shape[1]
    grid_spec = pltpu.PrefetchScalarGridSpec(
        num_scalar_prefetch=1,
        grid=(n_out // rows,),
        in_specs=[pl.BlockSpec(memory_space=pl.ANY)],
        out_specs=pl.BlockSpec((rows, width // LANES, LANES), lambda i, src: (i, 0, 0)),
        scratch_shapes=[pltpu.SemaphoreType.DMA(())],
    )
    out = pl.pallas_call(
        functools.partial(_gather_kernel, rows=rows),
        grid_spec=grid_spec,
        out_shape=jax.ShapeDtypeStruct((n_out, width // LANES, LANES), table.dtype),
        compiler_params=pltpu.CompilerParams(dimension_semantics=("arbitrary",)),
        name="gather_rows",
    )(src, table.reshape(table.shape[0], width // LANES, LANES))
    return out.reshape(n_out, width)


def _gffn_kernel(te_ref, tv_ref, hs_ref, wg_ref, wu_ref, wd_ref, out_ref, acc_scr):
    i = pl.program_id(0)
    f = pl.program_id(1)
    nf = pl.num_programs(1)

    @pl.when(f == 0)
    def _():
        acc_scr[...] = jnp.zeros_like(acc_scr)

    @pl.when(tv_ref[i] > 0)
    def _():
        h = hs_ref[...]
        a = jnp.dot(h, wg_ref[0], preferred_element_type=F32)
        u = jnp.dot(h, wu_ref[0], preferred_element_type=F32)
        acc_scr[...] += jnp.dot((_silu(a) * u).astype(BF16), wd_ref[0], preferred_element_type=F32)

    @pl.when(f == nf - 1)
    def _():
        out_ref[...] = acc_scr[...].astype(BF16)


def _gffn(tile_expert, tile_valid, hs, wg, wu, wd, tm=MOE_TILE, tf=512):
    p = hs.shape[0]
    nf = D_FF // tf

    def fblk(f, tv, i):
        return jnp.where(tv[i] > 0, f, nf - 1)

    grid_spec = pltpu.PrefetchScalarGridSpec(
        num_scalar_prefetch=2,
        grid=(p // tm, nf),
        in_specs=[
            pl.BlockSpec((tm, D_MODEL), lambda i, f, te, tv: (i, 0)),
            pl.BlockSpec((1, D_MODEL, tf), lambda i, f, te, tv: (te[i], 0, fblk(f, tv, i))),
            pl.BlockSpec((1, D_MODEL, tf), lambda i, f, te, tv: (te[i], 0, fblk(f, tv, i))),
            pl.BlockSpec((1, tf, D_MODEL), lambda i, f, te, tv: (te[i], fblk(f, tv, i), 0)),
        ],
        out_specs=pl.BlockSpec((tm, D_MODEL), lambda i, f, te, tv: (i, 0)),
        scratch_shapes=[pltpu.VMEM((tm, D_MODEL), F32)],
    )
    return pl.pallas_call(
        _gffn_kernel,
        grid_spec=grid_spec,
        out_shape=jax.ShapeDtypeStruct((p, D_MODEL), BF16),
        compiler_params=_params(("arbitrary", "arbitrary")),
        name="grouped_ffn",
    )(tile_expert, tile_valid, hs, wg, wu, wd)


def _combine_kernel(x_ref, y_ref, w_ref, out_ref):
    w = w_ref[...]
    out_ref[...] = (x_ref[...] + w[:, 0:1] * y_ref[:, :D_MODEL].astype(F32)
                    + w[:, 1:2] * y_ref[:, D_MODEL:].astype(F32))


def _combine(x2, y_pairs, top_w, tm=512):
    m = x2.shape[0]
    tm = min(tm, m)
    return pl.pallas_call(
        _combine_kernel,
        grid=(m // tm,),
        in_specs=[
            pl.BlockSpec((tm, D_MODEL), lambda i: (i, 0)),
            pl.BlockSpec((tm, TOP_K * D_MODEL), lambda i: (i, 0)),
            pl.BlockSpec((tm, LANES), lambda i: (i, 0)),
        ],
        out_specs=pl.BlockSpec((tm, D_MODEL), lambda i: (i, 0)),
        out_shape=jax.ShapeDtypeStruct((m, D_MODEL), F32),
        compiler_params=_params(("arbitrary",)),
        name="combine",
    )(x2, y_pairs, top_w)


def _moe(x2, g, w_router, wg, wu, wd):
    m = x2.shape[0]
    tm = MOE_TILE
    h, top_idx, top_w = _route(x2, g, w_router)
    e_flat = top_idx[:, :TOP_K].reshape(-1)
    onehot = (e_flat[:, None] == jnp.arange(N_EXPERTS, dtype=I32)[None, :]).astype(I32)
    ranks = jnp.cumsum(onehot, axis=0) - onehot
    rank = jnp.sum(ranks * onehot, axis=1)
    counts = jnp.sum(onehot, axis=0)
    padded = ((counts + tm - 1) // tm) * tm
    seg_end = jnp.cumsum(padded)
    seg_start = seg_end - padded
    pos = seg_start[e_flat] + rank
    p_max = TOP_K * m + N_EXPERTS * tm
    n_tiles = p_max // tm
    src = jnp.zeros((p_max,), I32).at[pos].set(jnp.arange(TOP_K * m, dtype=I32) // TOP_K)
    tile_start = jnp.arange(n_tiles, dtype=I32) * tm
    tile_valid = (tile_start < seg_end[-1]).astype(I32)
    last_expert = jnp.max(jnp.where(counts > 0, jnp.arange(N_EXPERTS, dtype=I32), 0))
    tile_expert = jnp.sum((tile_start[:, None] >= seg_end[None, :]).astype(I32), axis=1)
    tile_expert = jnp.where(tile_valid > 0, tile_expert, last_expert)

    hs = _gather_rows(src, h, p_max)
    ys = _gffn(tile_expert, tile_valid, hs, wg, wu, wd)
    y_pairs = _gather_rows(pos, ys, TOP_K * m).reshape(m, TOP_K * D_MODEL)
    return _combine(x2, y_pairs, top_w)


def _pad_row(v, width):
    return jnp.pad(v.astype(F32), (0, width - v.shape[0])).reshape(1, width)


def _alibi_tables():
    h = jnp.arange(1, NSA_HEADS + 1, dtype=F32)
    slopes = jnp.exp2(-8.0 * h / NSA_HEADS).reshape(NSA_KV_HEADS, NSA_GROUP)
    lanes_tbl = jnp.broadcast_to(jnp.pad(slopes, ((0, 0), (0, 8 - NSA_GROUP)))[:, :, None], (NSA_KV_HEADS, 8, LANES))
    rows_tbl = jnp.repeat(slopes, SEL_BLOCK, axis=1)[:, :, None]
    return lanes_tbl, rows_tbl


def _mixer_layer(x2, bsz, seq, norm_mix, w_in, ssm_conv_w, ssm_conv_b, ssm_dt_bias, ssm_a_log, ssm_d, ssm_norm,
                 w_ssm_out, sc_conv_w, w_sc_out, q_norm, k_norm, cmp_pos, w_cmp_k, w_cmp_v, w_nsa_out, w_out):
    w_big = jnp.concatenate(
        [w_in[:, _SRC_XBC:_SRC_DT], w_in[:, _SRC_Z:_SRC_XBC], w_in[:, _SRC_SC:_SRC_NG], w_in[:, _SRC_MG:]],
        axis=1).astype(BF16)
    w_small = jnp.concatenate(
        [w_in[:, _SRC_DT:_SRC_SC], w_in[:, _SRC_NG:_SRC_MG],
         jnp.zeros((D_MODEL, SMALL_W - SSM_HEADS - N_BRANCH * NSA_HEADS), F32)], axis=1).astype(BF16)
    proj, small = _inproj(x2, norm_mix.reshape(1, D_MODEL), w_big, w_small)

    y_ssm = _ssd(proj, small, ssm_conv_w, ssm_conv_b.reshape(1, SSM_XBC), _pad_row(ssm_dt_bias, SMALL_W),
                 _pad_row(ssm_a_log, SMALL_W), jnp.repeat(ssm_d, SSM_HEAD_DIM).reshape(1, D_MODEL),
                 ssm_norm.reshape(1, D_MODEL), bsz, seq)

    G, Dh = NSA_KV_HEADS, NSA_HEAD_DIM
    nb = seq // CMP_STRIDE
    r5 = proj[:, KV_OFF:KV_OFF + 2 * G * Dh].reshape(bsz, nb, CMP_STRIDE, 2, G, Dh)
    r5 = r5.transpose(0, 4, 3, 1, 2, 5).reshape(bsz, G, 2, nb, CMP_STRIDE * Dh)
    lanes_tbl, rows_tbl = _alibi_tables()
    half_k = CMP_STRIDE * Dh
    o_cmp, qn, sel = _nsa_cmp(r5, proj, w_cmp_k.reshape(2, half_k, Dh).astype(BF16),
                              w_cmp_v.reshape(2, half_k, Dh).astype(BF16), cmp_pos.reshape(2, 2, half_k),
                              k_norm[0:1], q_norm.reshape(1, Dh), lanes_tbl, bsz, seq)
    sel_flat = sel[..., :SEL_TOPK].reshape(-1)
    y_nsa = _nsa_attn(sel_flat, qn, proj, o_cmp, small, k_norm, rows_tbl, bsz, seq)

    mixed = _merge(y_ssm, proj, y_nsa, sc_conv_w, w_ssm_out.astype(BF16), w_sc_out.astype(BF16),
                   w_nsa_out.astype(BF16), seq)
    return _outproj(mixed, w_out.astype(BF16), x2)


def kernel(x, norm_mix, w_in, ssm_conv_w, ssm_conv_b, ssm_dt_bias, ssm_a_log, ssm_d, ssm_norm, w_ssm_out,
           sc_conv_w, w_sc_out, q_norm, k_norm, cmp_pos, w_cmp_k, w_cmp_v, w_nsa_out, w_out, norm_ffn,
           ffn_w_gate, ffn_w_up, ffn_w_down, moe_router, moe_w_gate, moe_w_up, moe_w_down):
    bsz, seq, _ = x.shape
    depth = norm_mix.shape[0]
    x2 = x.reshape(bsz * seq, D_MODEL)
    for layer in range(depth):
        x2 = _mixer_layer(x2, bsz, seq, norm_mix[layer], w_in[layer], ssm_conv_w[layer], ssm_conv_b[layer],
                          ssm_dt_bias[layer], ssm_a_log[layer], ssm_d[layer], ssm_norm[layer], w_ssm_out[layer],
                          sc_conv_w[layer], w_sc_out[layer], q_norm[layer], k_norm[layer], cmp_pos[layer],
                          w_cmp_k[layer], w_cmp_v[layer], w_nsa_out[layer], w_out[layer])
        g = norm_ffn[layer].reshape(1, D_MODEL)
        i = layer // 2
        if layer % 2 == 0:
            x2 = _ffn(x2, g, ffn_w_gate[i].astype(BF16), ffn_w_up[i].astype(BF16), ffn_w_down[i].astype(BF16))
        else:
            w_router = jnp.pad(moe_router[i], ((0, 0), (0, LANES - N_EXPERTS)))
            x2 = _moe(x2, g, w_router, moe_w_gate[i].astype(BF16), moe_w_up[i].astype(BF16),
                      moe_w_down[i].astype(BF16))
    return x2.reshape(bsz, seq, D_MODEL)
```

```python
import functools

import jax
import jax.numpy as jnp
from jax import lax
from jax.experimental import pallas as pl
from jax.experimental.pallas import tpu as pltpu

F32 = jnp.float32
BF16 = jnp.bfloat16
I32 = jnp.int32

D_MODEL = 2048
EPS = 1e-6
NEG_INF = -1e30
MASKED_DIST = 1e30
LOG2E = 1.4426950408889634
FORCED_SCORE = 1e4
SSM_HEAD_DIM = 64
SSM_HEADS = D_MODEL // SSM_HEAD_DIM
SSM_GROUPS = 8
SSM_HEADS_PER_GROUP = SSM_HEADS // SSM_GROUPS
SSM_STATE = 128
SSM_CONV = 4
SSM_CHUNK = 128
SSM_XBC = D_MODEL + 2 * SSM_GROUPS * SSM_STATE
SC_CONV = 3
NSA_HEADS = 16
NSA_HEAD_DIM = 128
NSA_KV_HEADS = 4
NSA_GROUP = NSA_HEADS // NSA_KV_HEADS
CMP_STRIDE = 16
CMP_LEN = 2 * CMP_STRIDE
SEL_BLOCK = 64
SEL_TOPK = 8
WINDOW = 512
N_BRANCH = 3
D_FF = 5632
N_EXPERTS = 8
TOP_K = 2

XBC_OFF = 0
Z_OFF = XBC_OFF + SSM_XBC
SC_OFF = Z_OFF + D_MODEL
Q_OFF = SC_OFF + 3 * D_MODEL
MG_OFF = Q_OFF + NSA_HEADS * NSA_HEAD_DIM
KV_OFF = MG_OFF + N_BRANCH * D_MODEL
PROJ_W = KV_OFF + 6 * NSA_KV_HEADS * NSA_HEAD_DIM
LANES = 128
SMALL_W = LANES
GATE_LANE0 = SSM_HEADS
_SRC_Z = 0
_SRC_XBC = D_MODEL
_SRC_DT = _SRC_XBC + SSM_XBC
_SRC_SC = _SRC_DT + SSM_HEADS
_SRC_Q = _SRC_SC + 3 * D_MODEL
_SRC_KV = _SRC_Q + NSA_HEADS * NSA_HEAD_DIM
_SRC_NG = _SRC_KV + 6 * NSA_KV_HEADS * NSA_HEAD_DIM
_SRC_MG = _SRC_NG + N_BRANCH * NSA_HEADS

VMEM_LIMIT = 56 * 1024 * 1024
MOE_TILE = 512
GATHER_ROWS = 512
NSA_Q_BLOCKS_PER_STEP = 4
SWIGLU_SUB_ROWS = 512


def _params(sem, vmem=VMEM_LIMIT):
    return pltpu.CompilerParams(dimension_semantics=sem, vmem_limit_bytes=vmem)


def _sigmoid(x):
    return 1.0 / (1.0 + jnp.exp(-x))


def _silu(x):
    return x * _sigmoid(x)


def _rms_rows(x, g):
    return x * lax.rsqrt(jnp.mean(x * x, axis=-1, keepdims=True) + EPS) * g


def _norm_to_scratch(x_ref, g_ref, h_scr, rows, chunk=256):
    g = g_ref[...]

    def body(i, carry):
        r0 = pl.multiple_of(i * chunk, chunk)
        h_scr[pl.ds(r0, chunk), :] = _rms_rows(x_ref[pl.ds(r0, chunk), :], g).astype(BF16)
        return carry

    lax.fori_loop(0, rows // chunk, body, 0)


def _inproj_kernel(x_ref, g_ref, w_ref, ws_ref, proj_ref, small_ref, h_scr, *, tm):
    @pl.when(pl.program_id(1) == 0)
    def _():
        _norm_to_scratch(x_ref, g_ref, h_scr, tm)
        small_ref[...] = jnp.dot(h_scr[...], ws_ref[...], preferred_element_type=F32)

    proj_ref[...] = jnp.dot(h_scr[...], w_ref[...], preferred_element_type=F32).astype(BF16)


def _inproj(x2, g, w_big, w_small, tm=1024, tn=1024):
    m = x2.shape[0]
    tm = min(tm, m)
    return pl.pallas_call(
        functools.partial(_inproj_kernel, tm=tm),
        grid=(m // tm, PROJ_W // tn),
        in_specs=[
            pl.BlockSpec((tm, D_MODEL), lambda i, j: (i, 0)),
            pl.BlockSpec((1, D_MODEL), lambda i, j: (0, 0)),
            pl.BlockSpec((D_MODEL, tn), lambda i, j: (0, j)),
            pl.BlockSpec((D_MODEL, SMALL_W), lambda i, j: (0, 0)),
        ],
        out_specs=[
            pl.BlockSpec((tm, tn), lambda i, j: (i, j)),
            pl.BlockSpec((tm, SMALL_W), lambda i, j: (i, 0)),
        ],
        out_shape=[
            jax.ShapeDtypeStruct((m, PROJ_W), BF16),
            jax.ShapeDtypeStruct((m, SMALL_W), F32),
        ],
        scratch_shapes=[pltpu.VMEM((tm, D_MODEL), BF16)],
        compiler_params=_params(("arbitrary", "arbitrary")),
        name="inproj",
    )(x2, g, w_big, w_small)


def _shift_rows(cur, prev_tail, s):
    rc = pltpu.roll(cur, s, 0)
    rp = pltpu.roll(prev_tail, s, 0)
    row = lax.broadcasted_iota(I32, (8, cur.shape[1]), 0)
    top = jnp.where(row < s, rp[0:8], rc[0:8])
    return jnp.concatenate([top, rc[8:]], axis=0)


def _ssd_kernel(xbc_ref, prev_ref, z_ref, small_ref, cw_ref, cb_ref, dtb_ref, alog_ref, dskip_ref,
                ng_ref, out_ref, state_scr, y_scr):
    L = SSM_CHUNK
    P = SSM_HEAD_DIM
    R = SSM_HEADS_PER_GROUP
    GW = R * P
    c = pl.program_id(1)

    @pl.when(c == 0)
    def _():
        state_scr[...] = jnp.zeros_like(state_scr)

    keep_prev = (c > 0).astype(F32)

    def conv_silu(lo, w):
        cur = xbc_ref[:, lo:lo + w].astype(F32)
        prev = prev_ref[:, lo:lo + w].astype(F32) * keep_prev
        acc = cur * cw_ref[SSM_CONV - 1:SSM_CONV, lo:lo + w] + cb_ref[:, lo:lo + w]
        for s in range(1, SSM_CONV):
            k = SSM_CONV - 1 - s
            acc = acc + _shift_rows(cur, prev, s) * cw_ref[k:k + 1, lo:lo + w]
        return _silu(acc)

    row = lax.broadcasted_iota(I32, (L, L), 0)
    col = lax.broadcasted_iota(I32, (L, L), 1)
    causal = row >= col
    tril = causal.astype(F32)

    pre = small_ref[...] + dtb_ref[...]
    dt = jnp.maximum(pre, 0.0) + jnp.log(1.0 + jnp.exp(-jnp.abs(pre)))
    dt = jnp.where(col < SSM_HEADS, dt, 0.0)
    a_neg = -jnp.exp(alog_ref[...])
    a_cum = jnp.dot(tril, dt * a_neg, precision=lax.Precision.HIGHEST, preferred_element_type=F32)
    a_cum_t = a_cum.T
    a_last = a_cum[L - 1:L, :]
    dec_end = jnp.exp(a_last - a_cum)
    dec_in = jnp.exp(a_cum)
    chunk_dec = jnp.exp(a_last)

    lane_g = lax.broadcasted_iota(I32, (L, GW), 1)
    lane_g1 = lax.broadcasted_iota(I32, (1, GW), 1)

    def expand(mat, g, lanes):
        out = mat[:, R * g + R - 1:R * g + R]
        for r in range(R - 2, -1, -1):
            out = jnp.where(lanes < (r + 1) * P, mat[:, R * g + r:R * g + r + 1], out)
        return out

    for g in range(SSM_GROUPS):
        xs = conv_silu(g * GW, GW)
        b_in = conv_silu(D_MODEL + g * SSM_STATE, SSM_STATE)
        c_out = conv_silu(D_MODEL + (SSM_GROUPS + g) * SSM_STATE, SSM_STATE)
        b_bf = b_in.astype(BF16)
        c_bf = c_out.astype(BF16)
        cb = lax.dot_general(c_bf, b_bf, (((1,), (1,)), ((), ())), preferred_element_type=F32)
        xdt = xs * expand(dt, g, lane_g)
        y = xs * dskip_ref[:, g * GW:(g + 1) * GW]
        for r in range(R):
            h = R * g + r
            seg = a_cum[:, h:h + 1] - a_cum_t[h:h + 1, :]
            lmat = jnp.where(causal, jnp.exp(jnp.minimum(seg, 0.0)), 0.0)
            head = (lane_g >= r * P) & (lane_g < (r + 1) * P)
            xdt_r = jnp.where(head, xdt, 0.0).astype(BF16)
            y = y + jnp.dot((cb * lmat).astype(BF16), xdt_r, preferred_element_type=F32)
        h_prev = state_scr[g]
        y = y + jnp.dot(c_bf, h_prev.astype(BF16), preferred_element_type=F32) * expand(dec_in, g, lane_g)
        xdt_end = (xdt * expand(dec_end, g, lane_g)).astype(BF16)
        new_state = jnp.dot(b_in.T.astype(BF16), xdt_end, preferred_element_type=F32)
        state_scr[g] = h_prev * expand(chunk_dec, g, lane_g1) + new_state
        y_scr[:, g * GW:(g + 1) * GW] = y

    yz = y_scr[...] * _silu(z_ref[...].astype(F32))
    out_ref[...] = _rms_rows(yz, ng_ref[...]).astype(BF16)


def _ssd(proj, small, conv_w, conv_b, dt_bias, a_log, d_skip, norm_g, bsz, seq):
    L = SSM_CHUNK
    nc = seq // L
    tail = 16
    per = L // tail
    return pl.pallas_call(
        _ssd_kernel,
        grid=(bsz, nc),
        in_specs=[
            pl.BlockSpec((L, SSM_XBC), lambda b, c: (b * nc + c, XBC_OFF // SSM_XBC)),
            pl.BlockSpec((tail, SSM_XBC), lambda b, c: (jnp.maximum((b * nc + c) * per - 1, 0), XBC_OFF // SSM_XBC)),
            pl.BlockSpec((L, D_MODEL), lambda b, c: (b * nc + c, Z_OFF // D_MODEL)),
            pl.BlockSpec((L, SMALL_W), lambda b, c: (b * nc + c, 0)),
            pl.BlockSpec((SSM_CONV, SSM_XBC), lambda b, c: (0, 0)),
            pl.BlockSpec((1, SSM_XBC), lambda b, c: (0, 0)),
            pl.BlockSpec((1, SMALL_W), lambda b, c: (0, 0)),
            pl.BlockSpec((1, SMALL_W), lambda b, c: (0, 0)),
            pl.BlockSpec((1, D_MODEL), lambda b, c: (0, 0)),
            pl.BlockSpec((1, D_MODEL), lambda b, c: (0, 0)),
        ],
        out_specs=pl.BlockSpec((L, D_MODEL), lambda b, c: (b * nc + c, 0)),
        out_shape=jax.ShapeDtypeStruct((bsz * seq, D_MODEL), BF16),
        scratch_shapes=[
            pltpu.VMEM((SSM_GROUPS, SSM_STATE, SSM_HEADS_PER_GROUP * SSM_HEAD_DIM), F32),
            pltpu.VMEM((L, D_MODEL), F32),
        ],
        compiler_params=_params(("arbitrary", "arbitrary")),
        name="ssd",
    )(proj, proj, proj, small, conv_w, conv_b, dt_bias, a_log, d_skip, norm_g)


def _nsa_cmp_kernel(r_ref, q_ref, wk_ref, wv_ref, pos_ref, kng_ref, qng_ref, slope_ref,
                    ocmp_ref, qn_ref, idx_ref, kc_scr, vc_scr, *, seq, tq):
    Dh = NSA_HEAD_DIM
    nb = seq // CMP_STRIDE
    nqb = seq // SEL_BLOCK

    def compress(j, w_ref):
        t = r_ref[0, 0, j].astype(F32)
        top = jnp.dot((t + pos_ref[j, 0:1, :]).astype(BF16), w_ref[0], preferred_element_type=F32)
        bot = jnp.dot((t + pos_ref[j, 1:2, :]).astype(BF16), w_ref[1], preferred_element_type=F32)
        return top + pltpu.roll(bot, nb - 1, 0)

    kc_scr[...] = _rms_rows(compress(0, wk_ref), kng_ref[...]).astype(BF16)
    vc_scr[...] = compress(1, wv_ref).astype(BF16)
    k_cmp = kc_scr[...]
    v_cmp = vc_scr[...]

    n_start = lax.broadcasted_iota(I32, (1, nb), 1) * CMP_STRIDE
    qg = qng_ref[...] * (Dh ** -0.5 * LOG2E)

    def scores(qn, qpos, slope):
        s = lax.dot_general(qn, k_cmp, (((1,), (1,)), ((), ())), preferred_element_type=F32)
        dist = (qpos - n_start).astype(F32) - (CMP_LEN - 1) / 2
        valid = (n_start + (CMP_LEN - 1)) <= qpos
        s = jnp.where(valid, s - slope * dist, NEG_INF)
        p = jnp.exp2(s - jnp.max(s, axis=-1, keepdims=True))
        p = p / jnp.sum(p, axis=-1, keepdims=True)
        return jnp.where(valid, p, 0.0)

    sel_row = lax.broadcasted_iota(I32, (nqb, seq), 0) * SEL_BLOCK
    sel_col = lax.broadcasted_iota(I32, (nqb, seq), 1)
    pick_first = jnp.where(sel_row == sel_col, 1.0, 0.0).astype(BF16)
    first_pos = lax.broadcasted_iota(I32, (nqb, 1), 0) * SEL_BLOCK
    p_first = jnp.zeros((nqb, nb), F32)

    slopes = [slope_ref[0, r:r + 1, 0:1] for r in range(NSA_GROUP)]

    def body(i, carry):
        r0 = pl.multiple_of(i * tq, tq)
        qpos = r0 + lax.broadcasted_iota(I32, (tq, 1), 0)
        for r in range(NSA_GROUP):
            qn = _rms_rows(q_ref[pl.ds(r0, tq), r * Dh:(r + 1) * Dh].astype(F32), qg).astype(BF16)
            qn_ref[pl.ds(r0, tq), r * Dh:(r + 1) * Dh] = qn
            p = scores(qn, qpos, slopes[r])
            ocmp_ref[pl.ds(r0, tq), r * Dh:(r + 1) * Dh] = jnp.dot(
                p.astype(BF16), v_cmp, preferred_element_type=F32).astype(BF16)
        return carry

    lax.fori_loop(0, seq // tq, body, 0)
    for r in range(NSA_GROUP):
        q_first = jnp.dot(pick_first, qn_ref[:, r * Dh:(r + 1) * Dh], preferred_element_type=F32).astype(BF16)
        p_first = p_first + scores(q_first, first_pos, slopes[r])

    n_lo = lax.broadcasted_iota(I32, (nb, nqb), 0) * CMP_STRIDE
    j_lo = lax.broadcasted_iota(I32, (nb, nqb), 1) * SEL_BLOCK
    overlap = jnp.maximum(jnp.minimum(n_lo + CMP_LEN, j_lo + SEL_BLOCK) - jnp.maximum(n_lo, j_lo), 0)
    overlap = overlap.astype(F32) / CMP_LEN
    imp = jnp.dot(p_first, overlap, precision=lax.Precision.HIGHEST, preferred_element_type=F32)
    qb_i = lax.broadcasted_iota(I32, (nqb, nqb), 0)
    blk_j = lax.broadcasted_iota(I32, (nqb, nqb), 1)
    forced = (blk_j == 0) | (blk_j == qb_i) | (blk_j == qb_i - 1)
    imp = jnp.where(forced, FORCED_SCORE, jnp.where(blk_j > qb_i, -FORCED_SCORE, imp))
    out_lane = lax.broadcasted_iota(I32, (nqb, LANES), 1)
    picked = jnp.zeros((nqb, LANES), I32)
    for k in range(SEL_TOPK):
        best = jnp.max(imp, axis=-1, keepdims=True)
        arg = jnp.min(jnp.where(imp == best, blk_j, nqb), axis=-1, keepdims=True)
        picked = jnp.where(out_lane == k, arg, picked)
        imp = jnp.where(blk_j == arg, -jnp.inf, imp)
    idx_ref[0, 0] = picked


def _nsa_cmp(r5, proj, wk, wv, pos, kng, qng, slopes, bsz, seq):
    G = NSA_KV_HEADS
    nb = seq // CMP_STRIDE
    nqb = seq // SEL_BLOCK
    qw = NSA_GROUP * NSA_HEAD_DIM
    return pl.pallas_call(
        functools.partial(_nsa_cmp_kernel, seq=seq, tq=min(512, seq)),
        grid=(bsz, G),
        in_specs=[
            pl.BlockSpec((1, 1, 2, nb, CMP_STRIDE * NSA_HEAD_DIM), lambda b, g: (b, g, 0, 0, 0)),
            pl.BlockSpec((seq, qw), lambda b, g: (b, Q_OFF // qw + g)),
            pl.BlockSpec((2, CMP_STRIDE * NSA_HEAD_DIM, NSA_HEAD_DIM), lambda b, g: (0, 0, 0)),
            pl.BlockSpec((2, CMP_STRIDE * NSA_HEAD_DIM, NSA_HEAD_DIM), lambda b, g: (0, 0, 0)),
            pl.BlockSpec((2, 2, CMP_STRIDE * NSA_HEAD_DIM), lambda b, g: (0, 0, 0)),
            pl.BlockSpec((1, NSA_HEAD_DIM), lambda b, g: (0, 0)),
            pl.BlockSpec((1, NSA_HEAD_DIM), lambda b, g: (0, 0)),
            pl.BlockSpec((1, 8, LANES), lambda b, g: (g, 0, 0)),
        ],
        out_specs=[
            pl.BlockSpec((seq, qw), lambda b, g: (b, g)),
            pl.BlockSpec((seq, qw), lambda b, g: (b, g)),
            pl.BlockSpec((1, 1, nqb, LANES), lambda b, g: (b, g, 0, 0)),
        ],
        out_shape=[
            jax.ShapeDtypeStruct((bsz * seq, NSA_HEADS * NSA_HEAD_DIM), BF16),
            jax.ShapeDtypeStruct((bsz * seq, NSA_HEADS * NSA_HEAD_DIM), BF16),
            jax.ShapeDtypeStruct((bsz, G, nqb, LANES), I32),
        ],
        scratch_shapes=[pltpu.VMEM((nb, NSA_HEAD_DIM), BF16), pltpu.VMEM((nb, NSA_HEAD_DIM), BF16)],
        compiler_params=_params(("arbitrary", "arbitrary")),
        name="nsa_cmp",
    )(r5, proj, wk, wv, pos, kng, qng, slopes)


def _nsa_attn_kernel(idx_ref, slope_ref, qn_ref, ks_ref, vs_ref, kw_ref, vw_ref, ocmp_ref, gate_ref, kng_ref,
                     out_ref, ksn_scr, kwn_scr, vwp_scr, wdist_scr, s_scr, p_scr, *, seq, qps):
    Dh = NSA_HEAD_DIM
    R = NSA_GROUP
    QB = SEL_BLOCK
    nqb = seq // QB
    span = WINDOW + QB
    b = pl.program_id(0)
    g = pl.program_id(1)
    step = pl.program_id(2)
    q_in = lax.broadcasted_iota(I32, (QB, 1), 0)

    @pl.when(step == 0)
    def _():
        kwn_scr[0:WINDOW, :] = jnp.zeros((WINDOW, Dh), BF16)
        vwp_scr[0:WINDOW, :] = jnp.zeros((WINDOW, Dh), BF16)

        def body(i, carry):
            r0 = pl.multiple_of(i * 256, 256)
            ksn_scr[pl.ds(r0, 256), :] = _rms_rows(ks_ref[pl.ds(r0, 256), :].astype(F32), kng_ref[1:2, :]).astype(BF16)
            kwn_scr[pl.ds(WINDOW + r0, 256), :] = _rms_rows(kw_ref[pl.ds(r0, 256), :].astype(F32),
                                                           kng_ref[2:3, :]).astype(BF16)
            vwp_scr[pl.ds(WINDOW + r0, 256), :] = vw_ref[pl.ds(r0, 256), :]
            return carry

        lax.fori_loop(0, seq // 256, body, 0)
        dist_w = q_in + WINDOW - lax.broadcasted_iota(I32, (QB, span), 1)
        wdist_scr[...] = jnp.where((dist_w >= 0) & (dist_w < WINDOW), dist_w.astype(F32), MASKED_DIST)

    nk = SEL_TOPK * QB
    key_lane = lax.broadcasted_iota(I32, (1, nk), 1)
    win_lane = lax.broadcasted_iota(I32, (1, span), 1)
    lane = lax.broadcasted_iota(I32, (QB, LANES), 1)
    slopes = [slope_ref[g * R + r] for r in range(R)]

    def attend(slot, q4, k, v, dist, width):
        s_scr[slot, :, 0:width] = lax.dot_general(q4, k, (((1,), (1,)), ((), ())), preferred_element_type=F32)
        inv = []
        for r in range(R):
            s = s_scr[slot, r * QB:(r + 1) * QB, 0:width] - slopes[r] * dist
            p = jnp.exp2(s - jnp.max(s, axis=-1, keepdims=True))
            inv.append(1.0 / jnp.sum(p, axis=-1, keepdims=True))
            p_scr[slot, r * QB:(r + 1) * QB, 0:width] = p.astype(BF16)
        o = jnp.dot(p_scr[slot, :, 0:width], v, preferred_element_type=F32)
        return [o[r * QB:(r + 1) * QB] * inv[r] for r in range(R)]

    for i in range(qps):
        qb = step * qps + i
        rows = slice(i * QB, (i + 1) * QB)
        q4 = jnp.concatenate([qn_ref[rows, r * Dh:(r + 1) * Dh] for r in range(R)], axis=0)

        base = ((b * NSA_KV_HEADS + g) * nqb + qb) * SEL_TOPK
        kpos_s = key_lane & (QB - 1)
        ks, vs = [], []
        for k in range(SEL_TOPK):
            start = pl.multiple_of(idx_ref[base + k] * QB, QB)
            ks.append(ksn_scr[pl.ds(start, QB), :])
            vs.append(vs_ref[pl.ds(start, QB), :])
            kpos_s = kpos_s + jnp.where((key_lane >= k * QB) & (key_lane < (k + 1) * QB), start, 0)
        dist_s = (q_in + qb * QB - kpos_s).astype(F32)
        dist_s = jnp.where(dist_s >= 0, dist_s, MASKED_DIST)
        o_sel = attend(2 * (i % 2), q4, jnp.concatenate(ks, axis=0), jnp.concatenate(vs, axis=0), dist_s, nk)

        wstart = pl.multiple_of(qb * QB, QB)
        dist_w = jnp.where(win_lane >= WINDOW - qb * QB, wdist_scr[...], MASKED_DIST)
        o_win = attend(2 * (i % 2) + 1, q4, kwn_scr[pl.ds(wstart, span), :], vwp_scr[pl.ds(wstart, span), :],
                       dist_w, span)

        sig = _sigmoid(gate_ref[rows, :])

        def gate(br, r):
            return jnp.sum(jnp.where(lane == GATE_LANE0 + br * NSA_HEADS + g * R + r, sig, 0.0), axis=-1, keepdims=True)

        for r in range(R):
            o_cmp = ocmp_ref[rows, r * Dh:(r + 1) * Dh].astype(F32)
            o = gate(0, r) * o_cmp + gate(1, r) * o_sel[r] + gate(2, r) * o_win[r]
            out_ref[rows, r * Dh:(r + 1) * Dh] = o.astype(BF16)


def _nsa_attn(sel_flat, slopes, qn, proj, ocmp, small, kng, bsz, seq):
    G = NSA_KV_HEADS
    Dh = NSA_HEAD_DIM
    QB = SEL_BLOCK
    nqb = seq // QB
    qw = NSA_GROUP * Dh
    kv0 = KV_OFF // Dh

    def kv_spec(j):
        return pl.BlockSpec((seq, Dh), lambda b, g, q, idx, sl: (b, kv0 + j * G + g))

    qps = NSA_Q_BLOCKS_PER_STEP
    steps = nqb // qps
    tq = qps * QB
    span = WINDOW + QB
    grid_spec = pltpu.PrefetchScalarGridSpec(
        num_scalar_prefetch=2,
        grid=(bsz, G, steps),
        in_specs=[
            pl.BlockSpec((tq, qw), lambda b, g, q, idx, sl: (b * steps + q, g)),
            kv_spec(2), kv_spec(3), kv_spec(4), kv_spec(5),
            pl.BlockSpec((tq, qw), lambda b, g, q, idx, sl: (b * steps + q, g)),
            pl.BlockSpec((tq, SMALL_W), lambda b, g, q, idx, sl: (b * steps + q, 0)),
            pl.BlockSpec((N_BRANCH, Dh), lambda b, g, q, idx, sl: (0, 0)),
        ],
        out_specs=pl.BlockSpec((tq, qw), lambda b, g, q, idx, sl: (b * steps + q, g)),
        scratch_shapes=[
            pltpu.VMEM((seq, Dh), BF16),
            pltpu.VMEM((WINDOW + seq, Dh), BF16),
            pltpu.VMEM((WINDOW + seq, Dh), BF16),
            pltpu.VMEM((QB, span), F32),
            pltpu.VMEM((4, NSA_GROUP * QB, span), F32),
            pltpu.VMEM((4, NSA_GROUP * QB, span), BF16),
        ],
    )
    return pl.pallas_call(
        functools.partial(_nsa_attn_kernel, seq=seq, qps=qps),
        grid_spec=grid_spec,
        out_shape=jax.ShapeDtypeStruct((bsz * seq, NSA_HEADS * Dh), BF16),
        compiler_params=_params(("arbitrary", "arbitrary", "arbitrary")),
        name="nsa_attn",
    )(sel_flat, slopes, qn, proj, proj, proj, proj, ocmp, small, kng)


def _merge_kernel(yssm_ref, bg_ref, cg_ref, xi_ref, cgp_ref, xip_ref, ynsa_ref, scw_ref, w0_ref, w1_ref, w2_ref,
                  g0_ref, g1_ref, g2_ref, out_ref, ysc_scr, *, tm, tiles_per_seq, tn):
    keep_prev = (pl.program_id(0) % tiles_per_seq > 0).astype(F32)
    chunk = 128
    tail = cgp_ref.shape[0]

    def body(i, carry):
        r0 = pl.multiple_of(i * chunk, chunk)
        u = cg_ref[pl.ds(r0, chunk), :].astype(F32) * xi_ref[pl.ds(r0, chunk), :].astype(F32)
        p0 = pl.multiple_of(jnp.maximum(r0 - tail, 0), tail)
        prev_in = cg_ref[pl.ds(p0, tail), :].astype(F32) * xi_ref[pl.ds(p0, tail), :].astype(F32)
        prev_out = cgp_ref[...].astype(F32) * xip_ref[...].astype(F32) * keep_prev
        prev = jnp.where(i > 0, prev_in, prev_out)
        acc = u * scw_ref[SC_CONV - 1:SC_CONV, :]
        for s in range(1, SC_CONV):
            k = SC_CONV - 1 - s
            acc = acc + _shift_rows(u, prev, s) * scw_ref[k:k + 1, :]
        ysc_scr[pl.ds(r0, chunk), :] = (bg_ref[pl.ds(r0, chunk), :].astype(F32) * acc).astype(BF16)
        return carry

    lax.fori_loop(0, tm // chunk, body, 0)

    for n0 in range(0, D_MODEL, tn):
        cols = slice(n0, n0 + tn)
        a0 = jnp.dot(yssm_ref[...], w0_ref[:, cols], preferred_element_type=F32)
        a1 = jnp.dot(ysc_scr[...], w1_ref[:, cols], preferred_element_type=F32)
        a2 = jnp.dot(ynsa_ref[...], w2_ref[:, cols], preferred_element_type=F32)
        mixed = (_sigmoid(g0_ref[:, cols].astype(F32)) * a0 + _sigmoid(g1_ref[:, cols].astype(F32)) * a1
                 + _sigmoid(g2_ref[:, cols].astype(F32)) * a2)
        out_ref[:, cols] = mixed.astype(BF16)


def _merge(yssm, proj, ynsa, sc_w, w_ssm, w_sc, w_nsa, seq, tm=256, tn=512):
    m = yssm.shape[0]
    tm = min(tm, seq)
    tail = 16
    per = tm // tail
    sc0 = SC_OFF // D_MODEL
    mg0 = MG_OFF // D_MODEL
    rows = lambda c: pl.BlockSpec((tm, D_MODEL), lambda i: (i, c))
    tails = lambda c: pl.BlockSpec((tail, D_MODEL), lambda i: (jnp.maximum(i * per - 1, 0), c))
    wspec = pl.BlockSpec((D_MODEL, D_MODEL), lambda i: (0, 0), pipeline_mode=pl.Buffered(1))
    return pl.pallas_call(
        functools.partial(_merge_kernel, tm=tm, tiles_per_seq=seq // tm, tn=tn),
        grid=(m // tm,),
        in_specs=[
            rows(0),
            rows(sc0), rows(sc0 + 1), rows(sc0 + 2),
            tails(sc0 + 1), tails(sc0 + 2),
            rows(0),
            pl.BlockSpec((SC_CONV, D_MODEL), lambda i: (0, 0)),
            wspec, wspec, wspec,
            rows(mg0), rows(mg0 + 1), rows(mg0 + 2),
        ],
        out_specs=rows(0),
        out_shape=jax.ShapeDtypeStruct((m, D_MODEL), BF16),
        scratch_shapes=[pltpu.VMEM((tm, D_MODEL), BF16)],
        compiler_params=_params(("arbitrary",)),
        name="merge",
    )(yssm, proj, proj, proj, proj, proj, ynsa, sc_w, w_ssm, w_sc, w_nsa, proj, proj, proj)


def _outproj_kernel(a_ref, w_ref, x_ref, out_ref):
    out_ref[...] = x_ref[...] + jnp.dot(a_ref[...], w_ref[...], preferred_element_type=F32)


def _outproj(a, w, x2, tm=1024, tn=512):
    m = a.shape[0]
    tm = min(tm, m)
    return pl.pallas_call(
        _outproj_kernel,
        grid=(m // tm, D_MODEL // tn),
        in_specs=[
            pl.BlockSpec((tm, D_MODEL), lambda i, j: (i, 0)),
            pl.BlockSpec((D_MODEL, tn), lambda i, j: (0, j)),
            pl.BlockSpec((tm, tn), lambda i, j: (i, j)),
        ],
        out_specs=pl.BlockSpec((tm, tn), lambda i, j: (i, j)),
        out_shape=jax.ShapeDtypeStruct((m, D_MODEL), F32),
        compiler_params=_params(("arbitrary", "arbitrary")),
        name="outproj",
    )(a, w, x2)


def _ffn_kernel(x_ref, g_ref, wg_ref, wu_ref, wd_ref, out_ref, h_scr, *, tm):
    @pl.when(pl.program_id(1) == 0)
    def _():
        _norm_to_scratch(x_ref, g_ref, h_scr, tm)
        out_ref[...] = x_ref[...]

    _swiglu_accumulate(h_scr, wg_ref[...], wu_ref[...], wd_ref[...], out_ref, tm)


def _swiglu_accumulate(h_ref, wg, wu, wd, acc_ref, rows, sub=SWIGLU_SUB_ROWS):
    for r0 in range(0, rows, sub):
        h = h_ref[r0:r0 + sub, :]
        a = jnp.dot(h, wg, preferred_element_type=F32)
        u = jnp.dot(h, wu, preferred_element_type=F32)
        acc_ref[r0:r0 + sub, :] += jnp.dot((_silu(a) * u).astype(BF16), wd, preferred_element_type=F32)


def _ffn(x2, g, wg, wu, wd, tm=1024, tf=512):
    m = x2.shape[0]
    tm = min(tm, m)
    return pl.pallas_call(
        functools.partial(_ffn_kernel, tm=tm),
        grid=(m // tm, D_FF // tf),
        in_specs=[
            pl.BlockSpec((tm, D_MODEL), lambda i, f: (i, 0)),
            pl.BlockSpec((1, D_MODEL), lambda i, f: (0, 0)),
            pl.BlockSpec((D_MODEL, tf), lambda i, f: (0, f)),
            pl.BlockSpec((D_MODEL, tf), lambda i, f: (0, f)),
            pl.BlockSpec((tf, D_MODEL), lambda i, f: (f, 0)),
        ],
        out_specs=pl.BlockSpec((tm, D_MODEL), lambda i, f: (i, 0)),
        out_shape=jax.ShapeDtypeStruct((m, D_MODEL), F32),
        scratch_shapes=[pltpu.VMEM((tm, D_MODEL), BF16)],
        compiler_params=_params(("arbitrary", "arbitrary")),
        name="ffn",
    )(x2, g, wg, wu, wd)


SLABS = D_MODEL // LANES


def _route_kernel(x_ref, g_ref, wr_ref, h_ref, idx_ref, wgt_ref):
    h = _rms_rows(x_ref[...], g_ref[...])
    h_ref[...] = h.astype(BF16)
    logits = jnp.dot(h, wr_ref[...], precision=lax.Precision.HIGHEST, preferred_element_type=F32)
    lane = lax.broadcasted_iota(I32, logits.shape, 1)
    logits = jnp.where(lane < N_EXPERTS, logits, -jnp.inf)
    v0 = jnp.max(logits, axis=-1, keepdims=True)
    i0 = jnp.min(jnp.where(logits == v0, lane, LANES), axis=-1, keepdims=True)
    rest = jnp.where(lane == i0, -jnp.inf, logits)
    v1 = jnp.max(rest, axis=-1, keepdims=True)
    i1 = jnp.min(jnp.where(rest == v1, lane, LANES), axis=-1, keepdims=True)
    e1 = jnp.exp(v1 - v0)
    w0 = 1.0 / (1.0 + e1)
    idx_ref[...] = jnp.where(lane == 0, i0, jnp.where(lane == 1, i1, 0))
    wgt_ref[...] = jnp.where(lane == 0, w0, jnp.where(lane == 1, e1 * w0, 0.0))


def _route(x2, g, w_router, tm=256):
    m = x2.shape[0]
    tm = min(tm, m)
    return pl.pallas_call(
        _route_kernel,
        grid=(m // tm,),
        in_specs=[
            pl.BlockSpec((tm, D_MODEL), lambda i: (i, 0)),
            pl.BlockSpec((1, D_MODEL), lambda i: (0, 0)),
            pl.BlockSpec((D_MODEL, LANES), lambda i: (0, 0)),
        ],
        out_specs=[
            pl.BlockSpec((tm, D_MODEL), lambda i: (i, 0)),
            pl.BlockSpec((tm, LANES), lambda i: (i, 0)),
            pl.BlockSpec((tm, LANES), lambda i: (i, 0)),
        ],
        out_shape=[
            jax.ShapeDtypeStruct((m, D_MODEL), BF16),
            jax.ShapeDtypeStruct((m, LANES), I32),
            jax.ShapeDtypeStruct((m, LANES), F32),
        ],
        compiler_params=_params(("arbitrary",)),
        name="route",
    )(x2, g, w_router)


def _gather_kernel(src_ref, in_ref, out_ref, sem, *, rows):
    base = pl.program_id(0) * rows

    def start(r, carry):
        pltpu.make_async_copy(in_ref.at[src_ref[base + r]], out_ref.at[r], sem).start()
        return carry

    lax.fori_loop(0, rows, start, 0, unroll=8)
    pltpu.make_async_copy(in_ref.at[pl.ds(0, rows)], out_ref, sem).wait()


def _gather_rows(src, table, n_out, rows=GATHER_ROWS):
    grid_spec = pltpu.PrefetchScalarGridSpec(
        num_scalar_prefetch=1,
        grid=(n_out // rows,),
        in_specs=[pl.BlockSpec(memory_space=pl.ANY)],
        out_specs=pl.BlockSpec((rows, SLABS, LANES), lambda i, src: (i, 0, 0)),
        scratch_shapes=[pltpu.SemaphoreType.DMA(())],
    )
    out = pl.pallas_call(
        functools.partial(_gather_kernel, rows=rows),
        grid_spec=grid_spec,
        out_shape=jax.ShapeDtypeStruct((n_out, SLABS, LANES), table.dtype),
        compiler_params=pltpu.CompilerParams(dimension_semantics=("arbitrary",)),
        name="gather_rows",
    )(src, table.reshape(table.shape[0], SLABS, LANES))
    return out.reshape(n_out, D_MODEL)


def _gffn_kernel(te_ref, tv_ref, hs_ref, wg_ref, wu_ref, wd_ref, out_ref, acc_scr):
    i = pl.program_id(0)
    f = pl.program_id(1)
    nf = pl.num_programs(1)

    @pl.when(f == 0)
    def _():
        acc_scr[...] = jnp.zeros_like(acc_scr)

    @pl.when(tv_ref[i] > 0)
    def _():
        _swiglu_accumulate(hs_ref, wg_ref[0], wu_ref[0], wd_ref[0], acc_scr, hs_ref.shape[0])

    @pl.when(f == nf - 1)
    def _():
        out_ref[...] = acc_scr[...].astype(BF16)


def _gffn(tile_expert, tile_valid, hs, wg, wu, wd, tm=MOE_TILE, tf=512):
    p = hs.shape[0]
    nf = D_FF // tf

    def fblk(f, tv, i):
        return jnp.where(tv[i] > 0, f, nf - 1)

    grid_spec = pltpu.PrefetchScalarGridSpec(
        num_scalar_prefetch=2,
        grid=(p // tm, nf),
        in_specs=[
            pl.BlockSpec((tm, D_MODEL), lambda i, f, te, tv: (i, 0)),
            pl.BlockSpec((1, D_MODEL, tf), lambda i, f, te, tv: (te[i], 0, fblk(f, tv, i))),
            pl.BlockSpec((1, D_MODEL, tf), lambda i, f, te, tv: (te[i], 0, fblk(f, tv, i))),
            pl.BlockSpec((1, tf, D_MODEL), lambda i, f, te, tv: (te[i], fblk(f, tv, i), 0)),
        ],
        out_specs=pl.BlockSpec((tm, D_MODEL), lambda i, f, te, tv: (i, 0)),
        scratch_shapes=[pltpu.VMEM((tm, D_MODEL), F32)],
    )
    return pl.pallas_call(
        _gffn_kernel,
        grid_spec=grid_spec,
        out_shape=jax.ShapeDtypeStruct((p, D_MODEL), BF16),
        compiler_params=_params(("arbitrary", "arbitrary")),
        name="grouped_ffn",
    )(tile_expert, tile_valid, hs, wg, wu, wd)


def _combine_kernel(x_ref, y_ref, w_ref, out_ref):
    w = w_ref[...]
    w0 = w[:, 0:1]
    w1 = w[:, 1:2]
    out_ref[...] = (x_ref[...] + w0 * y_ref[:, :D_MODEL].astype(F32) + w1 * y_ref[:, D_MODEL:].astype(F32))


def _combine(x2, y_pairs, top_w, tm=512):
    m = x2.shape[0]
    tm = min(tm, m)
    return pl.pallas_call(
        _combine_kernel,
        grid=(m // tm,),
        in_specs=[
            pl.BlockSpec((tm, D_MODEL), lambda i: (i, 0)),
            pl.BlockSpec((tm, TOP_K * D_MODEL), lambda i: (i, 0)),
            pl.BlockSpec((tm, LANES), lambda i: (i, 0)),
        ],
        out_specs=pl.BlockSpec((tm, D_MODEL), lambda i: (i, 0)),
        out_shape=jax.ShapeDtypeStruct((m, D_MODEL), F32),
        compiler_params=_params(("arbitrary",)),
        name="combine",
    )(x2, y_pairs, top_w)


def _moe(x2, g, w_router, wg, wu, wd):
    m = x2.shape[0]
    tm = MOE_TILE
    h, top_idx, top_w = _route(x2, g, w_router)
    e_flat = top_idx[:, :TOP_K].reshape(-1)
    onehot = (e_flat[:, None] == jnp.arange(N_EXPERTS, dtype=I32)[None, :]).astype(I32)
    ranks = jnp.cumsum(onehot, axis=0) - onehot
    rank = jnp.sum(ranks * onehot, axis=1)
    counts = jnp.sum(onehot, axis=0)
    padded = ((counts + tm - 1) // tm) * tm
    seg_end = jnp.cumsum(padded)
    seg_start = seg_end - padded
    pos = seg_start[e_flat] + rank
    p_max = TOP_K * m + N_EXPERTS * tm
    n_tiles = p_max // tm
    src = jnp.zeros((p_max,), I32).at[pos].set(jnp.arange(TOP_K * m, dtype=I32) // TOP_K)
    tile_start = jnp.arange(n_tiles, dtype=I32) * tm
    tile_valid = (tile_start < seg_end[-1]).astype(I32)
    last_expert = jnp.max(jnp.where(counts > 0, jnp.arange(N_EXPERTS, dtype=I32), 0))
    tile_expert = jnp.sum((tile_start[:, None] >= seg_end[None, :]).astype(I32), axis=1)
    tile_expert = jnp.where(tile_valid > 0, tile_expert, last_expert)

    hs = _gather_rows(src, h, p_max)
    ys = _gffn(tile_expert, tile_valid, hs, wg, wu, wd)
    y_pairs = _gather_rows(pos, ys, TOP_K * m).reshape(m, TOP_K * D_MODEL)
    return _combine(x2, y_pairs, top_w)


def _pad_row(v, width):
    return jnp.pad(v.astype(F32), (0, width - v.shape[0])).reshape(1, width)


def _alibi_tables():
    h = jnp.arange(1, NSA_HEADS + 1, dtype=F32)
    slopes = (jnp.exp2(-8.0 * h / NSA_HEADS) * LOG2E).reshape(NSA_KV_HEADS, NSA_GROUP)
    lanes_tbl = jnp.broadcast_to(jnp.pad(slopes, ((0, 0), (0, 8 - NSA_GROUP)))[:, :, None], (NSA_KV_HEADS, 8, LANES))
    return lanes_tbl, slopes.reshape(-1)


def _mixer_layer(x2, bsz, seq, norm_mix, w_in, ssm_conv_w, ssm_conv_b, ssm_dt_bias, ssm_a_log, ssm_d, ssm_norm,
                 w_ssm_out, sc_conv_w, w_sc_out, q_norm, k_norm, cmp_pos, w_cmp_k, w_cmp_v, w_nsa_out, w_out):
    w_big = jnp.concatenate(
        [w_in[:, _SRC_XBC:_SRC_DT], w_in[:, _SRC_Z:_SRC_XBC], w_in[:, _SRC_SC:_SRC_KV], w_in[:, _SRC_MG:],
         w_in[:, _SRC_KV:_SRC_NG]], axis=1).astype(BF16)
    w_small = jnp.concatenate(
        [w_in[:, _SRC_DT:_SRC_SC], w_in[:, _SRC_NG:_SRC_MG],
         jnp.zeros((D_MODEL, SMALL_W - SSM_HEADS - N_BRANCH * NSA_HEADS), F32)], axis=1).astype(BF16)
    proj, small = _inproj(x2, norm_mix.reshape(1, D_MODEL), w_big, w_small)

    y_ssm = _ssd(proj, small, ssm_conv_w, ssm_conv_b.reshape(1, SSM_XBC), _pad_row(ssm_dt_bias, SMALL_W),
                 _pad_row(ssm_a_log, SMALL_W), jnp.repeat(ssm_d, SSM_HEAD_DIM).reshape(1, D_MODEL),
                 ssm_norm.reshape(1, D_MODEL), bsz, seq)

    G, Dh = NSA_KV_HEADS, NSA_HEAD_DIM
    nb = seq // CMP_STRIDE
    r5 = proj[:, KV_OFF:KV_OFF + 2 * G * Dh].reshape(bsz, nb, CMP_STRIDE, 2, G, Dh)
    r5 = r5.transpose(0, 4, 3, 1, 2, 5).reshape(bsz, G, 2, nb, CMP_STRIDE * Dh)
    lanes_tbl, slopes_flat = _alibi_tables()
    half_k = CMP_STRIDE * Dh
    o_cmp, qn, sel = _nsa_cmp(r5, proj, w_cmp_k.reshape(2, half_k, Dh).astype(BF16),
                              w_cmp_v.reshape(2, half_k, Dh).astype(BF16), cmp_pos.reshape(2, 2, half_k),
                              k_norm[0:1], q_norm.reshape(1, Dh), lanes_tbl, bsz, seq)
    sel_flat = sel[..., :SEL_TOPK].reshape(-1)
    y_nsa = _nsa_attn(sel_flat, slopes_flat, qn, proj, o_cmp, small, k_norm, bsz, seq)

    mixed = _merge(y_ssm, proj, y_nsa, sc_conv_w, w_ssm_out.astype(BF16), w_sc_out.astype(BF16),
                   w_nsa_out.astype(BF16), seq)
    return _outproj(mixed, w_out.astype(BF16), x2)


def kernel(x, norm_mix, w_in, ssm_conv_w, ssm_conv_b, ssm_dt_bias, ssm_a_log, ssm_d, ssm_norm, w_ssm_out,
           sc_conv_w, w_sc_out, q_norm, k_norm, cmp_pos, w_cmp_k, w_cmp_v, w_nsa_out, w_out, norm_ffn,
           ffn_w_gate, ffn_w_up, ffn_w_down, moe_router, moe_w_gate, moe_w_up, moe_w_down):
    bsz, seq, _ = x.shape
    depth = norm_mix.shape[0]
    x2 = x.reshape(bsz * seq, D_MODEL)
    for layer in range(depth):
        x2 = _mixer_layer(x2, bsz, seq, norm_mix[layer], w_in[layer], ssm_conv_w[layer], ssm_conv_b[layer],
                          ssm_dt_bias[layer], ssm_a_log[layer], ssm_d[layer], ssm_norm[layer], w_ssm_out[layer],
                          sc_conv_w[layer], w_sc_out[layer], q_norm[layer], k_norm[layer], cmp_pos[layer],
                          w_cmp_k[layer], w_cmp_v[layer], w_nsa_out[layer], w_out[layer])
        g = norm_ffn[layer].reshape(1, D_MODEL)
        i = layer // 2
        if layer % 2 == 0:
            x2 = _ffn(x2, g, ffn_w_gate[i].astype(BF16), ffn_w_up[i].astype(BF16), ffn_w_down[i].astype(BF16))
        else:
            w_router = jnp.pad(moe_router[i], ((0, 0), (0, LANES - N_EXPERTS)))
            x2 = _moe(x2, g, w_router, moe_w_gate[i].astype(BF16), moe_w_up[i].astype(BF16),
                      moe_w_down[i].astype(BF16))
    return x2.reshape(bsz, seq, D_MODEL)
```

```python
import functools
import math

import jax
import jax.numpy as jnp
from jax import lax
from jax.experimental import pallas as pl
from jax.experimental.pallas import tpu as pltpu

F32 = jnp.float32
BF16 = jnp.bfloat16
I32 = jnp.int32

D_MODEL = 2048
EPS = 1e-6
NEG_INF = -1e30
MASKED_DIST = 1e30
LOG2E = 1.4426950408889634
FORCED_SCORE = 1e4
SSM_HEAD_DIM = 64
SSM_HEADS = D_MODEL // SSM_HEAD_DIM
SSM_GROUPS = 8
SSM_HEADS_PER_GROUP = SSM_HEADS // SSM_GROUPS
SSM_STATE = 128
SSM_CONV = 4
SSM_CHUNK = 128
SSM_XBC = D_MODEL + 2 * SSM_GROUPS * SSM_STATE
SC_CONV = 3
NSA_HEADS = 16
NSA_HEAD_DIM = 128
NSA_KV_HEADS = 4
NSA_GROUP = NSA_HEADS // NSA_KV_HEADS
CMP_STRIDE = 16
CMP_LEN = 2 * CMP_STRIDE
SEL_BLOCK = 64
SEL_TOPK = 8
WINDOW = 512
N_BRANCH = 3
D_FF = 5632
N_EXPERTS = 8
TOP_K = 2

XBC_OFF = 0
Z_OFF = XBC_OFF + SSM_XBC
SC_OFF = Z_OFF + D_MODEL
Q_OFF = SC_OFF + 3 * D_MODEL
MG_OFF = Q_OFF + NSA_HEADS * NSA_HEAD_DIM
KV_OFF = MG_OFF + N_BRANCH * D_MODEL
PROJ_W = KV_OFF + 6 * NSA_KV_HEADS * NSA_HEAD_DIM
LANES = 128
SMALL_W = LANES
GATE_LANE0 = SSM_HEADS
_SRC_Z = 0
_SRC_XBC = D_MODEL
_SRC_DT = _SRC_XBC + SSM_XBC
_SRC_SC = _SRC_DT + SSM_HEADS
_SRC_Q = _SRC_SC + 3 * D_MODEL
_SRC_KV = _SRC_Q + NSA_HEADS * NSA_HEAD_DIM
_SRC_NG = _SRC_KV + 6 * NSA_KV_HEADS * NSA_HEAD_DIM
_SRC_MG = _SRC_NG + N_BRANCH * NSA_HEADS

VMEM_LIMIT = 56 * 1024 * 1024
MOE_TILE = 768
GATHER_ROWS = 512
NSA_Q_BLOCKS_PER_STEP = 4
SWIGLU_SUB_ROWS = 512


def _params(sem, vmem=VMEM_LIMIT):
    return pltpu.CompilerParams(dimension_semantics=sem, vmem_limit_bytes=vmem)


def _sigmoid(x):
    return 1.0 / (1.0 + jnp.exp2(x * (-LOG2E)))


def _silu(x):
    return x * _sigmoid(x)


def _rms_rows(x, g):
    return x * lax.rsqrt(jnp.mean(x * x, axis=-1, keepdims=True) + EPS) * g


def _norm_to_scratch(x_ref, g_ref, h_scr, rows, chunk=256):
    g = g_ref[...]

    def body(i, carry):
        r0 = pl.multiple_of(i * chunk, chunk)
        h_scr[pl.ds(r0, chunk), :] = _rms_rows(x_ref[pl.ds(r0, chunk), :], g).astype(BF16)
        return carry

    lax.fori_loop(0, rows // chunk, body, 0)


def _inproj_kernel(x_ref, g_ref, w_ref, ws_ref, proj_ref, small_ref, h_scr, *, tm):
    @pl.when(pl.program_id(1) == 0)
    def _():
        _norm_to_scratch(x_ref, g_ref, h_scr, tm)
        small_ref[...] = jnp.dot(h_scr[...], ws_ref[...], preferred_element_type=F32)

    proj_ref[...] = jnp.dot(h_scr[...], w_ref[...], preferred_element_type=F32).astype(BF16)


def _inproj(x2, g, w_big, w_small, tm=1024, tn=1024):
    m = x2.shape[0]
    tm = min(tm, m)
    return pl.pallas_call(
        functools.partial(_inproj_kernel, tm=tm),
        grid=(m // tm, PROJ_W // tn),
        in_specs=[
            pl.BlockSpec((tm, D_MODEL), lambda i, j: (i, 0)),
            pl.BlockSpec((1, D_MODEL), lambda i, j: (0, 0)),
            pl.BlockSpec((D_MODEL, tn), lambda i, j: (0, j)),
            pl.BlockSpec((D_MODEL, SMALL_W), lambda i, j: (0, 0)),
        ],
        out_specs=[
            pl.BlockSpec((tm, tn), lambda i, j: (i, j)),
            pl.BlockSpec((tm, SMALL_W), lambda i, j: (i, 0)),
        ],
        out_shape=[
            jax.ShapeDtypeStruct((m, PROJ_W), BF16),
            jax.ShapeDtypeStruct((m, SMALL_W), F32),
        ],
        scratch_shapes=[pltpu.VMEM((tm, D_MODEL), BF16)],
        compiler_params=_params(("arbitrary", "arbitrary")),
        name="inproj",
    )(x2, g, w_big, w_small)


def _shift_rows(cur, prev_tail, s):
    rc = pltpu.roll(cur, s, 0)
    rp = pltpu.roll(prev_tail, s, 0)
    row = lax.broadcasted_iota(I32, (8, cur.shape[1]), 0)
    top = jnp.where(row < s, rp[0:8], rc[0:8])
    return jnp.concatenate([top, rc[8:]], axis=0)


def _ssd_kernel(xbc_ref, prev_ref, z_ref, small_ref, cw_ref, cb_ref, dtb_ref, alog_ref, dskip_ref,
                ng_ref, out_ref, state_scr, y_scr):
    L = SSM_CHUNK
    P = SSM_HEAD_DIM
    R = SSM_HEADS_PER_GROUP
    GW = R * P
    c = pl.program_id(1)

    @pl.when(c == 0)
    def _():
        state_scr[...] = jnp.zeros_like(state_scr)

    keep_prev = (c > 0).astype(F32)

    def conv_silu(lo, w):
        cur = xbc_ref[:, lo:lo + w].astype(F32)
        prev = prev_ref[:, lo:lo + w].astype(F32) * keep_prev
        acc = cur * cw_ref[SSM_CONV - 1:SSM_CONV, lo:lo + w] + cb_ref[:, lo:lo + w]
        for s in range(1, SSM_CONV):
            k = SSM_CONV - 1 - s
            acc = acc + _shift_rows(cur, prev, s) * cw_ref[k:k + 1, lo:lo + w]
        return _silu(acc)

    row = lax.broadcasted_iota(I32, (L, L), 0)
    col = lax.broadcasted_iota(I32, (L, L), 1)
    causal = row >= col
    tril = causal.astype(F32)

    pre = small_ref[...] + dtb_ref[...]
    dt = jnp.maximum(pre, 0.0) + jnp.log(1.0 + jnp.exp(-jnp.abs(pre)))
    dt = jnp.where(col < SSM_HEADS, dt, 0.0)
    a_neg = jnp.exp(alog_ref[...]) * (-LOG2E)
    a_cum = jnp.dot(tril, dt * a_neg, precision=lax.Precision.HIGHEST, preferred_element_type=F32)
    a_cum_t = a_cum.T
    a_last = a_cum[L - 1:L, :]
    dec_end = jnp.exp2(a_last - a_cum)
    dec_in = jnp.exp2(a_cum)
    chunk_dec = jnp.exp2(a_last)

    lane_g = lax.broadcasted_iota(I32, (L, GW), 1)
    lane_g1 = lax.broadcasted_iota(I32, (1, GW), 1)
    head_mask = [jnp.where((lane_g >= r * P) & (lane_g < (r + 1) * P), 1.0, 0.0).astype(BF16) for r in range(R)]

    def expand(mat, g, lanes):
        out = mat[:, R * g + R - 1:R * g + R]
        for r in range(R - 2, -1, -1):
            out = jnp.where(lanes < (r + 1) * P, mat[:, R * g + r:R * g + r + 1], out)
        return out

    for g in range(SSM_GROUPS):
        xs = conv_silu(g * GW, GW)
        b_in = conv_silu(D_MODEL + g * SSM_STATE, SSM_STATE)
        c_out = conv_silu(D_MODEL + (SSM_GROUPS + g) * SSM_STATE, SSM_STATE)
        b_bf = b_in.astype(BF16)
        c_bf = c_out.astype(BF16)
        cb = lax.dot_general(c_bf, b_bf, (((1,), (1,)), ((), ())), preferred_element_type=F32)
        xdt = xs * expand(dt, g, lane_g)
        xdt_bf = xdt.astype(BF16)
        y = xs * dskip_ref[:, g * GW:(g + 1) * GW]
        for r in range(R):
            h = R * g + r
            seg = a_cum[:, h:h + 1] - a_cum_t[h:h + 1, :]
            lmat = jnp.where(causal, jnp.exp2(seg), 0.0)
            y = y + jnp.dot((cb * lmat).astype(BF16), xdt_bf * head_mask[r], preferred_element_type=F32)
        h_prev = state_scr[g]
        y = y + jnp.dot(c_bf, h_prev.astype(BF16), preferred_element_type=F32) * expand(dec_in, g, lane_g)
        xdt_end = (xdt * expand(dec_end, g, lane_g)).astype(BF16)
        new_state = jnp.dot(b_in.T.astype(BF16), xdt_end, preferred_element_type=F32)
        state_scr[g] = h_prev * expand(chunk_dec, g, lane_g1) + new_state
        y_scr[:, g * GW:(g + 1) * GW] = y

    yz = y_scr[...] * _silu(z_ref[...].astype(F32))
    out_ref[...] = _rms_rows(yz, ng_ref[...]).astype(BF16)


def _ssd(proj, small, conv_w, conv_b, dt_bias, a_log, d_skip, norm_g, bsz, seq):
    L = SSM_CHUNK
    nc = seq // L
    tail = 16
    per = L // tail
    return pl.pallas_call(
        _ssd_kernel,
        grid=(bsz, nc),
        in_specs=[
            pl.BlockSpec((L, SSM_XBC), lambda b, c: (b * nc + c, XBC_OFF // SSM_XBC)),
            pl.BlockSpec((tail, SSM_XBC), lambda b, c: (jnp.maximum((b * nc + c) * per - 1, 0), XBC_OFF // SSM_XBC)),
            pl.BlockSpec((L, D_MODEL), lambda b, c: (b * nc + c, Z_OFF // D_MODEL)),
            pl.BlockSpec((L, SMALL_W), lambda b, c: (b * nc + c, 0)),
            pl.BlockSpec((SSM_CONV, SSM_XBC), lambda b, c: (0, 0)),
            pl.BlockSpec((1, SSM_XBC), lambda b, c: (0, 0)),
            pl.BlockSpec((1, SMALL_W), lambda b, c: (0, 0)),
            pl.BlockSpec((1, SMALL_W), lambda b, c: (0, 0)),
            pl.BlockSpec((1, D_MODEL), lambda b, c: (0, 0)),
            pl.BlockSpec((1, D_MODEL), lambda b, c: (0, 0)),
        ],
        out_specs=pl.BlockSpec((L, D_MODEL), lambda b, c: (b * nc + c, 0)),
        out_shape=jax.ShapeDtypeStruct((bsz * seq, D_MODEL), BF16),
        scratch_shapes=[
            pltpu.VMEM((SSM_GROUPS, SSM_STATE, SSM_HEADS_PER_GROUP * SSM_HEAD_DIM), F32),
            pltpu.VMEM((L, D_MODEL), F32),
        ],
        compiler_params=_params(("arbitrary", "arbitrary")),
        name="ssd",
    )(proj, proj, proj, small, conv_w, conv_b, dt_bias, a_log, d_skip, norm_g)


def _nsa_cmp_kernel(r_ref, q_ref, wk_ref, wv_ref, pos_ref, kng_ref, qng_ref, slope_ref,
                    ocmp_ref, qn_ref, idx_ref, kc_scr, vc_scr, *, seq, tq):
    Dh = NSA_HEAD_DIM
    nb = seq // CMP_STRIDE
    nqb = seq // SEL_BLOCK

    def compress(j, w_ref):
        t = r_ref[0, 0, j].astype(F32)
        top = jnp.dot((t + pos_ref[j, 0:1, :]).astype(BF16), w_ref[0], preferred_element_type=F32)
        bot = jnp.dot((t + pos_ref[j, 1:2, :]).astype(BF16), w_ref[1], preferred_element_type=F32)
        return top + pltpu.roll(bot, nb - 1, 0)

    kc_scr[...] = _rms_rows(compress(0, wk_ref), kng_ref[...]).T.astype(BF16)
    vc_scr[...] = compress(1, wv_ref).astype(BF16)
    k_cmp_t = kc_scr[...]
    v_cmp = vc_scr[...]

    n_start = lax.broadcasted_iota(I32, (1, nb), 1) * CMP_STRIDE
    qg = qng_ref[...] * (Dh ** -0.5 * LOG2E)

    def scores(qn, qpos, slope):
        s = jnp.dot(qn, k_cmp_t, preferred_element_type=F32)
        dist = (qpos - n_start).astype(F32) - (CMP_LEN - 1) / 2
        valid = (n_start + (CMP_LEN - 1)) <= qpos
        s = jnp.where(valid, s - slope * dist, NEG_INF)
        p = jnp.exp2(s - jnp.max(s, axis=-1, keepdims=True))
        p = p / jnp.sum(p, axis=-1, keepdims=True)
        return jnp.where(valid, p, 0.0)

    sel_row = lax.broadcasted_iota(I32, (nqb, seq), 0) * SEL_BLOCK
    sel_col = lax.broadcasted_iota(I32, (nqb, seq), 1)
    pick_first = jnp.where(sel_row == sel_col, 1.0, 0.0).astype(BF16)
    first_pos = lax.broadcasted_iota(I32, (nqb, 1), 0) * SEL_BLOCK
    p_first = jnp.zeros((nqb, nb), F32)

    slopes = [slope_ref[0, r:r + 1, 0:1] for r in range(NSA_GROUP)]

    def body(i, carry):
        r0 = pl.multiple_of(i * tq, tq)
        qpos = r0 + lax.broadcasted_iota(I32, (tq, 1), 0)
        for r in range(NSA_GROUP):
            qn = _rms_rows(q_ref[pl.ds(r0, tq), r * Dh:(r + 1) * Dh].astype(F32), qg).astype(BF16)
            qn_ref[pl.ds(r0, tq), r * Dh:(r + 1) * Dh] = qn
            p = scores(qn, qpos, slopes[r])
            ocmp_ref[pl.ds(r0, tq), r * Dh:(r + 1) * Dh] = jnp.dot(
                p.astype(BF16), v_cmp, preferred_element_type=F32).astype(BF16)
        return carry

    lax.fori_loop(0, seq // tq, body, 0)
    for r in range(NSA_GROUP):
        q_first = jnp.dot(pick_first, qn_ref[:, r * Dh:(r + 1) * Dh], preferred_element_type=F32).astype(BF16)
        p_first = p_first + scores(q_first, first_pos, slopes[r])

    n_lo = lax.broadcasted_iota(I32, (nb, nqb), 0) * CMP_STRIDE
    j_lo = lax.broadcasted_iota(I32, (nb, nqb), 1) * SEL_BLOCK
    overlap = jnp.maximum(jnp.minimum(n_lo + CMP_LEN, j_lo + SEL_BLOCK) - jnp.maximum(n_lo, j_lo), 0)
    overlap = overlap.astype(F32) / CMP_LEN
    imp = jnp.dot(p_first, overlap, precision=lax.Precision.HIGHEST, preferred_element_type=F32)
    qb_i = lax.broadcasted_iota(I32, (nqb, nqb), 0)
    blk_j = lax.broadcasted_iota(I32, (nqb, nqb), 1)
    forced = (blk_j == 0) | (blk_j == qb_i) | (blk_j == qb_i - 1)
    imp = jnp.where(forced, FORCED_SCORE, jnp.where(blk_j > qb_i, -FORCED_SCORE, imp))
    out_lane = lax.broadcasted_iota(I32, (nqb, LANES), 1)
    picked = jnp.zeros((nqb, LANES), I32)
    for k in range(SEL_TOPK):
        best = jnp.max(imp, axis=-1, keepdims=True)
        arg = jnp.min(jnp.where(imp == best, blk_j, nqb), axis=-1, keepdims=True)
        picked = jnp.where(out_lane == k, arg, picked)
        imp = jnp.where(blk_j == arg, -jnp.inf, imp)
    idx_ref[0, 0] = picked


def _nsa_cmp(r5, proj, wk, wv, pos, kng, qng, slopes, bsz, seq):
    G = NSA_KV_HEADS
    nb = seq // CMP_STRIDE
    nqb = seq // SEL_BLOCK
    qw = NSA_GROUP * NSA_HEAD_DIM
    return pl.pallas_call(
        functools.partial(_nsa_cmp_kernel, seq=seq, tq=min(512, seq)),
        grid=(bsz, G),
        in_specs=[
            pl.BlockSpec((1, 1, 2, nb, CMP_STRIDE * NSA_HEAD_DIM), lambda b, g: (b, g, 0, 0, 0)),
            pl.BlockSpec((seq, qw), lambda b, g: (b, Q_OFF // qw + g)),
            pl.BlockSpec((2, CMP_STRIDE * NSA_HEAD_DIM, NSA_HEAD_DIM), lambda b, g: (0, 0, 0)),
            pl.BlockSpec((2, CMP_STRIDE * NSA_HEAD_DIM, NSA_HEAD_DIM), lambda b, g: (0, 0, 0)),
            pl.BlockSpec((2, 2, CMP_STRIDE * NSA_HEAD_DIM), lambda b, g: (0, 0, 0)),
            pl.BlockSpec((1, NSA_HEAD_DIM), lambda b, g: (0, 0)),
            pl.BlockSpec((1, NSA_HEAD_DIM), lambda b, g: (0, 0)),
            pl.BlockSpec((1, 8, LANES), lambda b, g: (g, 0, 0)),
        ],
        out_specs=[
            pl.BlockSpec((seq, qw), lambda b, g: (b, g)),
            pl.BlockSpec((seq, qw), lambda b, g: (b, g)),
            pl.BlockSpec((1, 1, nqb, LANES), lambda b, g: (b, g, 0, 0)),
        ],
        out_shape=[
            jax.ShapeDtypeStruct((bsz * seq, NSA_HEADS * NSA_HEAD_DIM), BF16),
            jax.ShapeDtypeStruct((bsz * seq, NSA_HEADS * NSA_HEAD_DIM), BF16),
            jax.ShapeDtypeStruct((bsz, G, nqb, LANES), I32),
        ],
        scratch_shapes=[pltpu.VMEM((NSA_HEAD_DIM, nb), BF16), pltpu.VMEM((nb, NSA_HEAD_DIM), BF16)],
        compiler_params=_params(("arbitrary", "arbitrary")),
        name="nsa_cmp",
    )(r5, proj, wk, wv, pos, kng, qng, slopes)


def _nsa_attn_kernel(idx_ref, slope_ref, qn_ref, ks_ref, vs_ref, kw_ref, vw_ref, ocmp_ref, gate_ref, kng_ref,
                     out_ref, ksn_scr, kwn_scr, vwp_scr, wdist_scr, s_scr, p_scr, *, seq, qps):
    Dh = NSA_HEAD_DIM
    R = NSA_GROUP
    QB = SEL_BLOCK
    nqb = seq // QB
    span = WINDOW + QB
    b = pl.program_id(0)
    g = pl.program_id(1)
    step = pl.program_id(2)
    q_in = lax.broadcasted_iota(I32, (QB, 1), 0)

    @pl.when(step == 0)
    def _():
        kwn_scr[0:WINDOW, :] = jnp.zeros((WINDOW, Dh), BF16)
        vwp_scr[0:WINDOW, :] = jnp.zeros((WINDOW, Dh), BF16)

        def body(i, carry):
            r0 = pl.multiple_of(i * 256, 256)
            ksn_scr[pl.ds(r0, 256), :] = _rms_rows(ks_ref[pl.ds(r0, 256), :].astype(F32), kng_ref[1:2, :]).astype(BF16)
            kwn_scr[pl.ds(WINDOW + r0, 256), :] = _rms_rows(kw_ref[pl.ds(r0, 256), :].astype(F32),
                                                           kng_ref[2:3, :]).astype(BF16)
            vwp_scr[pl.ds(WINDOW + r0, 256), :] = vw_ref[pl.ds(r0, 256), :]
            return carry

        lax.fori_loop(0, seq // 256, body, 0)
        dist_w = q_in + WINDOW - lax.broadcasted_iota(I32, (QB, span), 1)
        wdist_scr[...] = jnp.where((dist_w >= 0) & (dist_w < WINDOW), dist_w.astype(F32), MASKED_DIST)

    nk = SEL_TOPK * QB
    key_lane = lax.broadcasted_iota(I32, (1, nk), 1)
    win_lane = lax.broadcasted_iota(I32, (1, span), 1)
    lane = lax.broadcasted_iota(I32, (QB, LANES), 1)
    slopes = [slope_ref[g * R + r] for r in range(R)]

    def attend(slot, q4, k, v, dist, width):
        s_scr[slot, :, 0:width] = lax.dot_general(q4, k, (((1,), (1,)), ((), ())), preferred_element_type=F32)
        inv = []
        for r in range(R):
            s = s_scr[slot, r * QB:(r + 1) * QB, 0:width] - slopes[r] * dist
            p = jnp.exp2(s - jnp.max(s, axis=-1, keepdims=True))
            inv.append(1.0 / jnp.sum(p, axis=-1, keepdims=True))
            p_scr[slot, r * QB:(r + 1) * QB, 0:width] = p.astype(BF16)
        o = jnp.dot(p_scr[slot, :, 0:width], v, preferred_element_type=F32)
        return [o[r * QB:(r + 1) * QB] * inv[r] for r in range(R)]

    for i in range(qps):
        qb = step * qps + i
        rows = slice(i * QB, (i + 1) * QB)
        q4 = jnp.concatenate([qn_ref[rows, r * Dh:(r + 1) * Dh] for r in range(R)], axis=0)

        base = ((b * NSA_KV_HEADS + g) * nqb + qb) * SEL_TOPK
        kpos_s = key_lane & (QB - 1)
        ks, vs = [], []
        for k in range(SEL_TOPK):
            start = pl.multiple_of(idx_ref[base + k] * QB, QB)
            ks.append(ksn_scr[pl.ds(start, QB), :])
            vs.append(vs_ref[pl.ds(start, QB), :])
            kpos_s = kpos_s + jnp.where((key_lane >= k * QB) & (key_lane < (k + 1) * QB), start, 0)
        dist_s = (q_in + qb * QB - kpos_s).astype(F32)
        dist_s = jnp.where(dist_s >= 0, dist_s, MASKED_DIST)
        o_sel = attend(2 * (i % 2), q4, jnp.concatenate(ks, axis=0), jnp.concatenate(vs, axis=0), dist_s, nk)

        wstart = pl.multiple_of(qb * QB, QB)
        dist_w = jnp.where(win_lane >= WINDOW - qb * QB, wdist_scr[...], MASKED_DIST)
        o_win = attend(2 * (i % 2) + 1, q4, kwn_scr[pl.ds(wstart, span), :], vwp_scr[pl.ds(wstart, span), :],
                       dist_w, span)

        sig = _sigmoid(gate_ref[rows, :])

        def gate(br, r):
            return jnp.sum(jnp.where(lane == GATE_LANE0 + br * NSA_HEADS + g * R + r, sig, 0.0), axis=-1, keepdims=True)

        for r in range(R):
            o_cmp = ocmp_ref[rows, r * Dh:(r + 1) * Dh].astype(F32)
            o = gate(0, r) * o_cmp + gate(1, r) * o_sel[r] + gate(2, r) * o_win[r]
            out_ref[rows, r * Dh:(r + 1) * Dh] = o.astype(BF16)


def _nsa_attn(sel_flat, slopes, qn, proj, ocmp, small, kng, bsz, seq):
    G = NSA_KV_HEADS
    Dh = NSA_HEAD_DIM
    QB = SEL_BLOCK
    nqb = seq // QB
    qw = NSA_GROUP * Dh
    kv0 = KV_OFF // Dh

    def kv_spec(j):
        return pl.BlockSpec((seq, Dh), lambda b, g, q, idx, sl: (b, kv0 + j * G + g))

    qps = NSA_Q_BLOCKS_PER_STEP
    steps = nqb // qps
    tq = qps * QB
    span = WINDOW + QB
    grid_spec = pltpu.PrefetchScalarGridSpec(
        num_scalar_prefetch=2,
        grid=(bsz, G, steps),
        in_specs=[
            pl.BlockSpec((tq, qw), lambda b, g, q, idx, sl: (b * steps + q, g)),
            kv_spec(2), kv_spec(3), kv_spec(4), kv_spec(5),
            pl.BlockSpec((tq, qw), lambda b, g, q, idx, sl: (b * steps + q, g)),
            pl.BlockSpec((tq, SMALL_W), lambda b, g, q, idx, sl: (b * steps + q, 0)),
            pl.BlockSpec((N_BRANCH, Dh), lambda b, g, q, idx, sl: (0, 0)),
        ],
        out_specs=pl.BlockSpec((tq, qw), lambda b, g, q, idx, sl: (b * steps + q, g)),
        scratch_shapes=[
            pltpu.VMEM((seq, Dh), BF16),
            pltpu.VMEM((WINDOW + seq, Dh), BF16),
            pltpu.VMEM((WINDOW + seq, Dh), BF16),
            pltpu.VMEM((QB, span), F32),
            pltpu.VMEM((4, NSA_GROUP * QB, span), F32),
            pltpu.VMEM((4, NSA_GROUP * QB, span), BF16),
        ],
    )
    return pl.pallas_call(
        functools.partial(_nsa_attn_kernel, seq=seq, qps=qps),
        grid_spec=grid_spec,
        out_shape=jax.ShapeDtypeStruct((bsz * seq, NSA_HEADS * Dh), BF16),
        compiler_params=_params(("arbitrary", "arbitrary", "arbitrary")),
        name="nsa_attn",
    )(sel_flat, slopes, qn, proj, proj, proj, proj, ocmp, small, kng)


def _merge_kernel(yssm_ref, bg_ref, cg_ref, xi_ref, cgp_ref, xip_ref, ynsa_ref, scw_ref, w0_ref, w1_ref, w2_ref,
                  g0_ref, g1_ref, g2_ref, out_ref, ysc_scr, *, tm, tiles_per_seq, tn):
    keep_prev = (pl.program_id(0) % tiles_per_seq > 0).astype(F32)
    chunk = 128
    tail = cgp_ref.shape[0]

    def body(i, carry):
        r0 = pl.multiple_of(i * chunk, chunk)
        u = cg_ref[pl.ds(r0, chunk), :].astype(F32) * xi_ref[pl.ds(r0, chunk), :].astype(F32)
        p0 = pl.multiple_of(jnp.maximum(r0 - tail, 0), tail)
        prev_in = cg_ref[pl.ds(p0, tail), :].astype(F32) * xi_ref[pl.ds(p0, tail), :].astype(F32)
        prev_out = cgp_ref[...].astype(F32) * xip_ref[...].astype(F32) * keep_prev
        prev = jnp.where(i > 0, prev_in, prev_out)
        acc = u * scw_ref[SC_CONV - 1:SC_CONV, :]
        for s in range(1, SC_CONV):
            k = SC_CONV - 1 - s
            acc = acc + _shift_rows(u, prev, s) * scw_ref[k:k + 1, :]
        ysc_scr[pl.ds(r0, chunk), :] = (bg_ref[pl.ds(r0, chunk), :].astype(F32) * acc).astype(BF16)
        return carry

    lax.fori_loop(0, tm // chunk, body, 0)

    for n0 in range(0, D_MODEL, tn):
        cols = slice(n0, n0 + tn)
        a0 = jnp.dot(yssm_ref[...], w0_ref[:, cols], preferred_element_type=F32)
        a1 = jnp.dot(ysc_scr[...], w1_ref[:, cols], preferred_element_type=F32)
        a2 = jnp.dot(ynsa_ref[...], w2_ref[:, cols], preferred_element_type=F32)
        mixed = (_sigmoid(g0_ref[:, cols].astype(F32)) * a0 + _sigmoid(g1_ref[:, cols].astype(F32)) * a1
                 + _sigmoid(g2_ref[:, cols].astype(F32)) * a2)
        out_ref[:, cols] = mixed.astype(BF16)


def _merge(yssm, proj, ynsa, sc_w, w_ssm, w_sc, w_nsa, seq, tm=256, tn=512):
    m = yssm.shape[0]
    tm = min(tm, seq)
    tail = 16
    per = tm // tail
    sc0 = SC_OFF // D_MODEL
    mg0 = MG_OFF // D_MODEL
    rows = lambda c: pl.BlockSpec((tm, D_MODEL), lambda i: (i, c))
    tails = lambda c: pl.BlockSpec((tail, D_MODEL), lambda i: (jnp.maximum(i * per - 1, 0), c))
    wspec = pl.BlockSpec((D_MODEL, D_MODEL), lambda i: (0, 0), pipeline_mode=pl.Buffered(1))
    return pl.pallas_call(
        functools.partial(_merge_kernel, tm=tm, tiles_per_seq=seq // tm, tn=tn),
        grid=(m // tm,),
        in_specs=[
            rows(0),
            rows(sc0), rows(sc0 + 1), rows(sc0 + 2),
            tails(sc0 + 1), tails(sc0 + 2),
            rows(0),
            pl.BlockSpec((SC_CONV, D_MODEL), lambda i: (0, 0)),
            wspec, wspec, wspec,
            rows(mg0), rows(mg0 + 1), rows(mg0 + 2),
        ],
        out_specs=rows(0),
        out_shape=jax.ShapeDtypeStruct((m, D_MODEL), BF16),
        scratch_shapes=[pltpu.VMEM((tm, D_MODEL), BF16)],
        compiler_params=_params(("arbitrary",)),
        name="merge",
    )(yssm, proj, proj, proj, proj, proj, ynsa, sc_w, w_ssm, w_sc, w_nsa, proj, proj, proj)


def _outproj_kernel(a_ref, w_ref, x_ref, out_ref, *, tn):
    for n0 in range(0, D_MODEL, tn):
        cols = slice(n0, n0 + tn)
        out_ref[:, cols] = x_ref[:, cols] + jnp.dot(a_ref[...], w_ref[:, cols], preferred_element_type=F32)


def _outproj(a, w, x2, tm=512, tn=512):
    m = a.shape[0]
    tm = min(tm, m)
    rows = pl.BlockSpec((tm, D_MODEL), lambda i: (i, 0))
    return pl.pallas_call(
        functools.partial(_outproj_kernel, tn=tn),
        grid=(m // tm,),
        in_specs=[
            rows,
            pl.BlockSpec((D_MODEL, D_MODEL), lambda i: (0, 0), pipeline_mode=pl.Buffered(1)),
            rows,
        ],
        out_specs=rows,
        out_shape=jax.ShapeDtypeStruct((m, D_MODEL), F32),
        compiler_params=_params(("arbitrary",)),
        name="outproj",
    )(a, w, x2)


def _ffn_kernel(x_ref, g_ref, wg_ref, wu_ref, wd_ref, out_ref, h_scr, *, tm):
    @pl.when(pl.program_id(1) == 0)
    def _():
        _norm_to_scratch(x_ref, g_ref, h_scr, tm)
        out_ref[...] = x_ref[...]

    _swiglu_accumulate(h_scr, wg_ref[...], wu_ref[...], wd_ref[...], out_ref, tm)


def _swiglu_accumulate(h_ref, wg, wu, wd, acc_ref, rows, sub=SWIGLU_SUB_ROWS):
    for r0 in range(0, rows, sub):
        h = h_ref[r0:r0 + sub, :]
        a = jnp.dot(h, wg, preferred_element_type=F32)
        u = jnp.dot(h, wu, preferred_element_type=F32)
        acc_ref[r0:r0 + sub, :] += jnp.dot((_silu(a) * u).astype(BF16), wd, preferred_element_type=F32)


def _ffn(x2, g, wg, wu, wd, tm=1024, tf=512):
    m = x2.shape[0]
    tm = min(tm, m)
    return pl.pallas_call(
        functools.partial(_ffn_kernel, tm=tm),
        grid=(m // tm, D_FF // tf),
        in_specs=[
            pl.BlockSpec((tm, D_MODEL), lambda i, f: (i, 0)),
            pl.BlockSpec((1, D_MODEL), lambda i, f: (0, 0)),
            pl.BlockSpec((D_MODEL, tf), lambda i, f: (0, f)),
            pl.BlockSpec((D_MODEL, tf), lambda i, f: (0, f)),
            pl.BlockSpec((tf, D_MODEL), lambda i, f: (f, 0)),
        ],
        out_specs=pl.BlockSpec((tm, D_MODEL), lambda i, f: (i, 0)),
        out_shape=jax.ShapeDtypeStruct((m, D_MODEL), F32),
        scratch_shapes=[pltpu.VMEM((tm, D_MODEL), BF16)],
        compiler_params=_params(("arbitrary", "arbitrary")),
        name="ffn",
    )(x2, g, wg, wu, wd)


SLABS = D_MODEL // LANES


def _route_kernel(x_ref, g_ref, wr_ref, h_ref, idx_ref, wgt_ref):
    h = _rms_rows(x_ref[...], g_ref[...])
    h_ref[...] = h.astype(BF16)
    logits = jnp.dot(h, wr_ref[...], precision=lax.Precision.HIGHEST, preferred_element_type=F32)
    lane = lax.broadcasted_iota(I32, logits.shape, 1)
    logits = jnp.where(lane < N_EXPERTS, logits, -jnp.inf)
    v0 = jnp.max(logits, axis=-1, keepdims=True)
    i0 = jnp.min(jnp.where(logits == v0, lane, LANES), axis=-1, keepdims=True)
    rest = jnp.where(lane == i0, -jnp.inf, logits)
    v1 = jnp.max(rest, axis=-1, keepdims=True)
    i1 = jnp.min(jnp.where(rest == v1, lane, LANES), axis=-1, keepdims=True)
    e1 = jnp.exp(v1 - v0)
    w0 = 1.0 / (1.0 + e1)
    idx_ref[...] = jnp.where(lane == 0, i0, jnp.where(lane == 1, i1, 0))
    wgt_ref[...] = jnp.where(lane == 0, w0, jnp.where(lane == 1, e1 * w0, 0.0))


def _route(x2, g, w_router, tm=256):
    m = x2.shape[0]
    tm = min(tm, m)
    return pl.pallas_call(
        _route_kernel,
        grid=(m // tm,),
        in_specs=[
            pl.BlockSpec((tm, D_MODEL), lambda i: (i, 0)),
            pl.BlockSpec((1, D_MODEL), lambda i: (0, 0)),
            pl.BlockSpec((D_MODEL, LANES), lambda i: (0, 0)),
        ],
        out_specs=[
            pl.BlockSpec((tm, D_MODEL), lambda i: (i, 0)),
            pl.BlockSpec((tm, LANES), lambda i: (i, 0)),
            pl.BlockSpec((tm, LANES), lambda i: (i, 0)),
        ],
        out_shape=[
            jax.ShapeDtypeStruct((m, D_MODEL), BF16),
            jax.ShapeDtypeStruct((m, LANES), I32),
            jax.ShapeDtypeStruct((m, LANES), F32),
        ],
        compiler_params=_params(("arbitrary",)),
        name="route",
    )(x2, g, w_router)


def _gather_kernel(src_ref, in_ref, out_ref, sem, *, rows):
    base = pl.program_id(0) * rows

    def start(r, carry):
        pltpu.make_async_copy(in_ref.at[src_ref[base + r]], out_ref.at[r], sem).start()
        return carry

    lax.fori_loop(0, rows, start, 0, unroll=8)
    pltpu.make_async_copy(in_ref.at[pl.ds(0, rows)], out_ref, sem).wait()


def _gather_rows(src, table, n_out, rows=GATHER_ROWS):
    grid_spec = pltpu.PrefetchScalarGridSpec(
        num_scalar_prefetch=1,
        grid=(n_out // rows,),
        in_specs=[pl.BlockSpec(memory_space=pl.ANY)],
        out_specs=pl.BlockSpec((rows, SLABS, LANES), lambda i, src: (i, 0, 0)),
        scratch_shapes=[pltpu.SemaphoreType.DMA(())],
    )
    out = pl.pallas_call(
        functools.partial(_gather_kernel, rows=rows),
        grid_spec=grid_spec,
        out_shape=jax.ShapeDtypeStruct((n_out, SLABS, LANES), table.dtype),
        compiler_params=pltpu.CompilerParams(dimension_semantics=("arbitrary",)),
        name="gather_rows",
    )(src, table.reshape(table.shape[0], SLABS, LANES))
    return out.reshape(n_out, D_MODEL)


def _gffn_kernel(te_ref, tv_ref, hs_ref, wg_ref, wu_ref, wd_ref, out_ref, acc_scr, wg_scr, wu_scr, wd_scr):
    i = pl.program_id(0)
    f = pl.program_id(1)
    nf = pl.num_programs(1)

    @pl.when(f == 0)
    def _():
        acc_scr[...] = jnp.zeros_like(acc_scr)

    @pl.when(tv_ref[i] > 0)
    def _():
        wg_scr[...] = wg_ref[0].astype(BF16)
        wu_scr[...] = wu_ref[0].astype(BF16)
        wd_scr[...] = wd_ref[0].astype(BF16)
        rows = hs_ref.shape[0]
        _swiglu_accumulate(hs_ref, wg_scr[...], wu_scr[...], wd_scr[...], acc_scr, rows, sub=rows)

    @pl.when(f == nf - 1)
    def _():
        out_ref[...] = acc_scr[...].astype(BF16)


def _gffn(tile_expert, tile_valid, hs, wg, wu, wd, tm=MOE_TILE, tf=512):
    p = hs.shape[0]
    nf = D_FF // tf

    def fblk(f, tv, i):
        return jnp.where(tv[i] > 0, f, nf - 1)

    grid_spec = pltpu.PrefetchScalarGridSpec(
        num_scalar_prefetch=2,
        grid=(p // tm, nf),
        in_specs=[
            pl.BlockSpec((tm, D_MODEL), lambda i, f, te, tv: (i, 0)),
            pl.BlockSpec((1, D_MODEL, tf), lambda i, f, te, tv: (te[i], 0, fblk(f, tv, i))),
            pl.BlockSpec((1, D_MODEL, tf), lambda i, f, te, tv: (te[i], 0, fblk(f, tv, i))),
            pl.BlockSpec((1, tf, D_MODEL), lambda i, f, te, tv: (te[i], fblk(f, tv, i), 0)),
        ],
        out_specs=pl.BlockSpec((tm, D_MODEL), lambda i, f, te, tv: (i, 0)),
        scratch_shapes=[
            pltpu.VMEM((tm, D_MODEL), F32),
            pltpu.VMEM((D_MODEL, tf), BF16),
            pltpu.VMEM((D_MODEL, tf), BF16),
            pltpu.VMEM((tf, D_MODEL), BF16),
        ],
    )
    return pl.pallas_call(
        _gffn_kernel,
        grid_spec=grid_spec,
        out_shape=jax.ShapeDtypeStruct((p, D_MODEL), BF16),
        compiler_params=_params(("arbitrary", "arbitrary")),
        name="grouped_ffn",
    )(tile_expert, tile_valid, hs, wg, wu, wd)


def _combine_kernel(x_ref, y_ref, w_ref, out_ref):
    w = w_ref[...]
    w0 = w[:, 0:1]
    w1 = w[:, 1:2]
    out_ref[...] = (x_ref[...] + w0 * y_ref[:, :D_MODEL].astype(F32) + w1 * y_ref[:, D_MODEL:].astype(F32))


def _combine(x2, y_pairs, top_w, tm=512):
    m = x2.shape[0]
    tm = min(tm, m)
    return pl.pallas_call(
        _combine_kernel,
        grid=(m // tm,),
        in_specs=[
            pl.BlockSpec((tm, D_MODEL), lambda i: (i, 0)),
            pl.BlockSpec((tm, TOP_K * D_MODEL), lambda i: (i, 0)),
            pl.BlockSpec((tm, LANES), lambda i: (i, 0)),
        ],
        out_specs=pl.BlockSpec((tm, D_MODEL), lambda i: (i, 0)),
        out_shape=jax.ShapeDtypeStruct((m, D_MODEL), F32),
        compiler_params=_params(("arbitrary",)),
        name="combine",
    )(x2, y_pairs, top_w)


def _moe(x2, g, w_router, wg, wu, wd):
    m = x2.shape[0]
    tm = MOE_TILE
    h, top_idx, top_w = _route(x2, g, w_router)
    e_flat = top_idx[:, :TOP_K].reshape(-1)
    onehot = (e_flat[:, None] == jnp.arange(N_EXPERTS, dtype=I32)[None, :]).astype(I32)
    ranks = jnp.cumsum(onehot, axis=0) - onehot
    rank = jnp.sum(ranks * onehot, axis=1)
    counts = jnp.sum(onehot, axis=0)
    padded = ((counts + tm - 1) // tm) * tm
    seg_end = jnp.cumsum(padded)
    seg_start = seg_end - padded
    pos = seg_start[e_flat] + rank
    step = math.lcm(tm, GATHER_ROWS)
    p_max = -(-(TOP_K * m + N_EXPERTS * tm) // step) * step
    n_tiles = p_max // tm
    src = jnp.zeros((p_max,), I32).at[pos].set(jnp.arange(TOP_K * m, dtype=I32) // TOP_K)
    tile_start = jnp.arange(n_tiles, dtype=I32) * tm
    tile_valid = (tile_start < seg_end[-1]).astype(I32)
    last_expert = jnp.max(jnp.where(counts > 0, jnp.arange(N_EXPERTS, dtype=I32), 0))
    tile_expert = jnp.sum((tile_start[:, None] >= seg_end[None, :]).astype(I32), axis=1)
    tile_expert = jnp.where(tile_valid > 0, tile_expert, last_expert)

    hs = _gather_rows(src, h, p_max)
    ys = _gffn(tile_expert, tile_valid, hs, wg, wu, wd)
    y_pairs = _gather_rows(pos, ys, TOP_K * m).reshape(m, TOP_K * D_MODEL)
    return _combine(x2, y_pairs, top_w)


def _pad_row(v, width):
    return jnp.pad(v.astype(F32), (0, width - v.shape[0])).reshape(1, width)


def _alibi_tables():
    h = jnp.arange(1, NSA_HEADS + 1, dtype=F32)
    slopes = (jnp.exp2(-8.0 * h / NSA_HEADS) * LOG2E).reshape(NSA_KV_HEADS, NSA_GROUP)
    lanes_tbl = jnp.broadcast_to(jnp.pad(slopes, ((0, 0), (0, 8 - NSA_GROUP)))[:, :, None], (NSA_KV_HEADS, 8, LANES))
    return lanes_tbl, slopes.reshape(-1)


def _mixer_layer(x2, bsz, seq, norm_mix, w_in, ssm_conv_w, ssm_conv_b, ssm_dt_bias, ssm_a_log, ssm_d, ssm_norm,
                 w_ssm_out, sc_conv_w, w_sc_out, q_norm, k_norm, cmp_pos, w_cmp_k, w_cmp_v, w_nsa_out, w_out):
    w_big = jnp.concatenate(
        [w_in[:, _SRC_XBC:_SRC_DT], w_in[:, _SRC_Z:_SRC_XBC], w_in[:, _SRC_SC:_SRC_KV], w_in[:, _SRC_MG:],
         w_in[:, _SRC_KV:_SRC_NG]], axis=1).astype(BF16)
    w_small = jnp.concatenate(
        [w_in[:, _SRC_DT:_SRC_SC], w_in[:, _SRC_NG:_SRC_MG],
         jnp.zeros((D_MODEL, SMALL_W - SSM_HEADS - N_BRANCH * NSA_HEADS), F32)], axis=1).astype(BF16)
    proj, small = _inproj(x2, norm_mix.reshape(1, D_MODEL), w_big, w_small)

    y_ssm = _ssd(proj, small, ssm_conv_w, ssm_conv_b.reshape(1, SSM_XBC), _pad_row(ssm_dt_bias, SMALL_W),
                 _pad_row(ssm_a_log, SMALL_W), jnp.repeat(ssm_d, SSM_HEAD_DIM).reshape(1, D_MODEL),
                 ssm_norm.reshape(1, D_MODEL), bsz, seq)

    G, Dh = NSA_KV_HEADS, NSA_HEAD_DIM
    nb = seq // CMP_STRIDE
    r5 = proj[:, KV_OFF:KV_OFF + 2 * G * Dh].reshape(bsz, nb, CMP_STRIDE, 2, G, Dh)
    r5 = r5.transpose(0, 4, 3, 1, 2, 5).reshape(bsz, G, 2, nb, CMP_STRIDE * Dh)
    lanes_tbl, slopes_flat = _alibi_tables()
    half_k = CMP_STRIDE * Dh
    o_cmp, qn, sel = _nsa_cmp(r5, proj, w_cmp_k.reshape(2, half_k, Dh).astype(BF16),
                              w_cmp_v.reshape(2, half_k, Dh).astype(BF16), cmp_pos.reshape(2, 2, half_k),
                              k_norm[0:1], q_norm.reshape(1, Dh), lanes_tbl, bsz, seq)
    sel_flat = sel[..., :SEL_TOPK].reshape(-1)
    y_nsa = _nsa_attn(sel_flat, slopes_flat, qn, proj, o_cmp, small, k_norm, bsz, seq)

    mixed = _merge(y_ssm, proj, y_nsa, sc_conv_w, w_ssm_out.astype(BF16), w_sc_out.astype(BF16),
                   w_nsa_out.astype(BF16), seq)
    return _outproj(mixed, w_out.astype(BF16), x2)


def kernel(x, norm_mix, w_in, ssm_conv_w, ssm_conv_b, ssm_dt_bias, ssm_a_log, ssm_d, ssm_norm, w_ssm_out,
           sc_conv_w, w_sc_out, q_norm, k_norm, cmp_pos, w_cmp_k, w_cmp_v, w_nsa_out, w_out, norm_ffn,
           ffn_w_gate, ffn_w_up, ffn_w_down, moe_router, moe_w_gate, moe_w_up, moe_w_down):
    bsz, seq, _ = x.shape
    depth = norm_mix.shape[0]
    x2 = x.reshape(bsz * seq, D_MODEL)
    for layer in range(depth):
        x2 = _mixer_layer(x2, bsz, seq, norm_mix[layer], w_in[layer], ssm_conv_w[layer], ssm_conv_b[layer],
                          ssm_dt_bias[layer], ssm_a_log[layer], ssm_d[layer], ssm_norm[layer], w_ssm_out[layer],
                          sc_conv_w[layer], w_sc_out[layer], q_norm[layer], k_norm[layer], cmp_pos[layer],
                          w_cmp_k[layer], w_cmp_v[layer], w_nsa_out[layer], w_out[layer])
        g = norm_ffn[layer].reshape(1, D_MODEL)
        i = layer // 2
        if layer % 2 == 0:
            x2 = _ffn(x2, g, ffn_w_gate[i].astype(BF16), ffn_w_up[i].astype(BF16), ffn_w_down[i].astype(BF16))
        else:
            w_router = jnp.pad(moe_router[i], ((0, 0), (0, LANES - N_EXPERTS)))
            x2 = _moe(x2, g, w_router, moe_w_gate[i], moe_w_up[i], moe_w_down[i])
    return x2.reshape(bsz, seq, D_MODEL)
```

```python
import functools
import math

import jax
import jax.numpy as jnp
from jax import lax
from jax.experimental import pallas as pl
from jax.experimental.pallas import tpu as pltpu

F32 = jnp.float32
BF16 = jnp.bfloat16
I32 = jnp.int32

D_MODEL = 2048
EPS = 1e-6
NEG_INF = -1e30
MASKED_DIST = 1e30
LOG2E = 1.4426950408889634
FORCED_SCORE = 1e4
SSM_HEAD_DIM = 64
SSM_HEADS = D_MODEL // SSM_HEAD_DIM
SSM_GROUPS = 8
SSM_HEADS_PER_GROUP = SSM_HEADS // SSM_GROUPS
SSM_STATE = 128
SSM_CONV = 4
SSM_CHUNK = 128
SSM_XBC = D_MODEL + 2 * SSM_GROUPS * SSM_STATE
SC_CONV = 3
NSA_HEADS = 16
NSA_HEAD_DIM = 128
NSA_KV_HEADS = 4
NSA_GROUP = NSA_HEADS // NSA_KV_HEADS
CMP_STRIDE = 16
CMP_LEN = 2 * CMP_STRIDE
SEL_BLOCK = 64
SEL_TOPK = 8
WINDOW = 512
N_BRANCH = 3
D_FF = 5632
N_EXPERTS = 8
TOP_K = 2

XBC_OFF = 0
Z_OFF = XBC_OFF + SSM_XBC
SC_OFF = Z_OFF + D_MODEL
Q_OFF = SC_OFF + 3 * D_MODEL
MG_OFF = Q_OFF + NSA_HEADS * NSA_HEAD_DIM
KV_OFF = MG_OFF + N_BRANCH * D_MODEL
PROJ_W = KV_OFF + 6 * NSA_KV_HEADS * NSA_HEAD_DIM
LANES = 128
SMALL_W = LANES
GATE_LANE0 = SSM_HEADS
_SRC_Z = 0
_SRC_XBC = D_MODEL
_SRC_DT = _SRC_XBC + SSM_XBC
_SRC_SC = _SRC_DT + SSM_HEADS
_SRC_Q = _SRC_SC + 3 * D_MODEL
_SRC_KV = _SRC_Q + NSA_HEADS * NSA_HEAD_DIM
_SRC_NG = _SRC_KV + 6 * NSA_KV_HEADS * NSA_HEAD_DIM
_SRC_MG = _SRC_NG + N_BRANCH * NSA_HEADS

VMEM_LIMIT = 56 * 1024 * 1024
MOE_TILE = 704
GATHER_ROWS = 512
NSA_Q_BLOCKS_PER_STEP = 8
SWIGLU_SUB_ROWS = 512


def _params(sem, vmem=VMEM_LIMIT):
    return pltpu.CompilerParams(dimension_semantics=sem, vmem_limit_bytes=vmem)


def _sigmoid(x):
    return 1.0 / (1.0 + jnp.exp2(x * (-LOG2E)))


def _silu(x):
    return x * _sigmoid(x)


def _rms_rows(x, g):
    return x * lax.rsqrt(jnp.mean(x * x, axis=-1, keepdims=True) + EPS) * g


def _norm_to_scratch(x_ref, g_ref, h_scr, rows, chunk=256):
    g = g_ref[...]

    def body(i, carry):
        r0 = pl.multiple_of(i * chunk, chunk)
        h_scr[pl.ds(r0, chunk), :] = _rms_rows(x_ref[pl.ds(r0, chunk), :], g).astype(BF16)
        return carry

    lax.fori_loop(0, rows // chunk, body, 0)


def _inproj_kernel(x_ref, g_ref, w_ref, ws_ref, proj_ref, small_ref, h_scr, *, tm):
    @pl.when(pl.program_id(1) == 0)
    def _():
        _norm_to_scratch(x_ref, g_ref, h_scr, tm)
        small_ref[...] = jnp.dot(h_scr[...], ws_ref[...], preferred_element_type=F32)

    proj_ref[...] = jnp.dot(h_scr[...], w_ref[...], preferred_element_type=F32).astype(BF16)


def _inproj(x2, g, w_big, w_small, tm=1024, tn=1024):
    m = x2.shape[0]
    tm = min(tm, m)
    return pl.pallas_call(
        functools.partial(_inproj_kernel, tm=tm),
        grid=(m // tm, PROJ_W // tn),
        in_specs=[
            pl.BlockSpec((tm, D_MODEL), lambda i, j: (i, 0)),
            pl.BlockSpec((1, D_MODEL), lambda i, j: (0, 0)),
            pl.BlockSpec((D_MODEL, tn), lambda i, j: (0, j)),
            pl.BlockSpec((D_MODEL, SMALL_W), lambda i, j: (0, 0)),
        ],
        out_specs=[
            pl.BlockSpec((tm, tn), lambda i, j: (i, j)),
            pl.BlockSpec((tm, SMALL_W), lambda i, j: (i, 0)),
        ],
        out_shape=[
            jax.ShapeDtypeStruct((m, PROJ_W), BF16),
            jax.ShapeDtypeStruct((m, SMALL_W), F32),
        ],
        scratch_shapes=[pltpu.VMEM((tm, D_MODEL), BF16)],
        compiler_params=_params(("arbitrary", "arbitrary")),
        name="inproj",
    )(x2, g, w_big, w_small)


def _shift_rows(cur, prev_tail, s):
    rc = pltpu.roll(cur, s, 0)
    rp = pltpu.roll(prev_tail, s, 0)
    row = lax.broadcasted_iota(I32, (8, cur.shape[1]), 0)
    top = jnp.where(row < s, rp[0:8], rc[0:8])
    return jnp.concatenate([top, rc[8:]], axis=0)


def _ssd_kernel(xbc_ref, prev_ref, z_ref, small_ref, cw_ref, cb_ref, dtb_ref, alog_ref, dskip_ref,
                ng_ref, out_ref, state_scr, y_scr):
    L = SSM_CHUNK
    P = SSM_HEAD_DIM
    R = SSM_HEADS_PER_GROUP
    GW = R * P
    c = pl.program_id(1)

    @pl.when(c == 0)
    def _():
        state_scr[...] = jnp.zeros_like(state_scr)

    keep_prev = (c > 0).astype(F32)

    def conv_silu(lo, w):
        cur = xbc_ref[:, lo:lo + w].astype(F32)
        prev = prev_ref[:, lo:lo + w].astype(F32) * keep_prev
        acc = cur * cw_ref[SSM_CONV - 1:SSM_CONV, lo:lo + w] + cb_ref[:, lo:lo + w]
        for s in range(1, SSM_CONV):
            k = SSM_CONV - 1 - s
            acc = acc + _shift_rows(cur, prev, s) * cw_ref[k:k + 1, lo:lo + w]
        return _silu(acc)

    row = lax.broadcasted_iota(I32, (L, L), 0)
    col = lax.broadcasted_iota(I32, (L, L), 1)
    causal = row >= col
    tril = causal.astype(F32)

    pre = small_ref[...] + dtb_ref[...]
    dt = jnp.maximum(pre, 0.0) + jnp.log(1.0 + jnp.exp(-jnp.abs(pre)))
    dt = jnp.where(col < SSM_HEADS, dt, 0.0)
    a_neg = jnp.exp(alog_ref[...]) * (-LOG2E)
    a_cum = jnp.dot(tril, dt * a_neg, precision=lax.Precision.HIGHEST, preferred_element_type=F32)
    a_cum_t = a_cum.T
    a_last = a_cum[L - 1:L, :]
    dec_end = jnp.exp2(a_last - a_cum)
    dec_in = jnp.exp2(a_cum)
    chunk_dec = jnp.exp2(a_last)

    lane_g = lax.broadcasted_iota(I32, (L, GW), 1)
    lane_g1 = lax.broadcasted_iota(I32, (1, GW), 1)
    head_mask = [jnp.where((lane_g >= r * P) & (lane_g < (r + 1) * P), 1.0, 0.0).astype(BF16) for r in range(R)]

    def expand(mat, g, lanes):
        out = mat[:, R * g + R - 1:R * g + R]
        for r in range(R - 2, -1, -1):
            out = jnp.where(lanes < (r + 1) * P, mat[:, R * g + r:R * g + r + 1], out)
        return out

    for g in range(SSM_GROUPS):
        xs = conv_silu(g * GW, GW)
        b_in = conv_silu(D_MODEL + g * SSM_STATE, SSM_STATE)
        c_out = conv_silu(D_MODEL + (SSM_GROUPS + g) * SSM_STATE, SSM_STATE)
        b_bf = b_in.astype(BF16)
        c_bf = c_out.astype(BF16)
        cb = lax.dot_general(c_bf, b_bf, (((1,), (1,)), ((), ())), preferred_element_type=F32)
        xdt = xs * expand(dt, g, lane_g)
        xdt_bf = xdt.astype(BF16)
        y = xs * dskip_ref[:, g * GW:(g + 1) * GW]
        for r in range(R):
            h = R * g + r
            seg = a_cum[:, h:h + 1] - a_cum_t[h:h + 1, :]
            lmat = jnp.where(causal, jnp.exp2(seg), 0.0)
            y = y + jnp.dot((cb * lmat).astype(BF16), xdt_bf * head_mask[r], preferred_element_type=F32)
        h_prev = state_scr[g]
        y = y + jnp.dot(c_bf, h_prev.astype(BF16), preferred_element_type=F32) * expand(dec_in, g, lane_g)
        xdt_end = (xdt * expand(dec_end, g, lane_g)).astype(BF16)
        new_state = jnp.dot(b_in.T.astype(BF16), xdt_end, preferred_element_type=F32)
        state_scr[g] = h_prev * expand(chunk_dec, g, lane_g1) + new_state
        y_scr[:, g * GW:(g + 1) * GW] = y

    yz = y_scr[...] * _silu(z_ref[...].astype(F32))
    out_ref[...] = _rms_rows(yz, ng_ref[...]).astype(BF16)


def _ssd(proj, small, conv_w, conv_b, dt_bias, a_log, d_skip, norm_g, bsz, seq):
    L = SSM_CHUNK
    nc = seq // L
    tail = 16
    per = L // tail
    return pl.pallas_call(
        _ssd_kernel,
        grid=(bsz, nc),
        in_specs=[
            pl.BlockSpec((L, SSM_XBC), lambda b, c: (b * nc + c, XBC_OFF // SSM_XBC)),
            pl.BlockSpec((tail, SSM_XBC), lambda b, c: (jnp.maximum((b * nc + c) * per - 1, 0), XBC_OFF // SSM_XBC)),
            pl.BlockSpec((L, D_MODEL), lambda b, c: (b * nc + c, Z_OFF // D_MODEL)),
            pl.BlockSpec((L, SMALL_W), lambda b, c: (b * nc + c, 0)),
            pl.BlockSpec((SSM_CONV, SSM_XBC), lambda b, c: (0, 0)),
            pl.BlockSpec((1, SSM_XBC), lambda b, c: (0, 0)),
            pl.BlockSpec((1, SMALL_W), lambda b, c: (0, 0)),
            pl.BlockSpec((1, SMALL_W), lambda b, c: (0, 0)),
            pl.BlockSpec((1, D_MODEL), lambda b, c: (0, 0)),
            pl.BlockSpec((1, D_MODEL), lambda b, c: (0, 0)),
        ],
        out_specs=pl.BlockSpec((L, D_MODEL), lambda b, c: (b * nc + c, 0)),
        out_shape=jax.ShapeDtypeStruct((bsz * seq, D_MODEL), BF16),
        scratch_shapes=[
            pltpu.VMEM((SSM_GROUPS, SSM_STATE, SSM_HEADS_PER_GROUP * SSM_HEAD_DIM), F32),
            pltpu.VMEM((L, D_MODEL), F32),
        ],
        compiler_params=_params(("arbitrary", "arbitrary")),
        name="ssd",
    )(proj, proj, proj, small, conv_w, conv_b, dt_bias, a_log, d_skip, norm_g)


def _nsa_cmp_kernel(r_ref, q_ref, wk_ref, wv_ref, pos_ref, kng_ref, qng_ref, slope_ref,
                    ocmp_ref, qn_ref, idx_ref, kc_scr, vc_scr, *, seq, tq):
    Dh = NSA_HEAD_DIM
    nb = seq // CMP_STRIDE
    nqb = seq // SEL_BLOCK

    def compress(j, w_ref):
        t = r_ref[0, 0, j].astype(F32)
        top = jnp.dot((t + pos_ref[j, 0:1, :]).astype(BF16), w_ref[0], preferred_element_type=F32)
        bot = jnp.dot((t + pos_ref[j, 1:2, :]).astype(BF16), w_ref[1], preferred_element_type=F32)
        return top + pltpu.roll(bot, nb - 1, 0)

    kc_scr[...] = _rms_rows(compress(0, wk_ref), kng_ref[...]).T.astype(BF16)
    vc_scr[...] = compress(1, wv_ref).astype(BF16)
    k_cmp_t = kc_scr[...]
    v_cmp = vc_scr[...]

    n_start = lax.broadcasted_iota(I32, (1, nb), 1) * CMP_STRIDE
    qg = qng_ref[...] * (Dh ** -0.5 * LOG2E)

    def scores(qn, qpos, slope):
        s = jnp.dot(qn, k_cmp_t, preferred_element_type=F32)
        dist = (qpos - n_start).astype(F32) - (CMP_LEN - 1) / 2
        valid = (n_start + (CMP_LEN - 1)) <= qpos
        s = jnp.where(valid, s - slope * dist, NEG_INF)
        p = jnp.exp2(s - jnp.max(s, axis=-1, keepdims=True))
        p = p / jnp.sum(p, axis=-1, keepdims=True)
        return jnp.where(valid, p, 0.0)

    sel_row = lax.broadcasted_iota(I32, (nqb, seq), 0) * SEL_BLOCK
    sel_col = lax.broadcasted_iota(I32, (nqb, seq), 1)
    pick_first = jnp.where(sel_row == sel_col, 1.0, 0.0).astype(BF16)
    first_pos = lax.broadcasted_iota(I32, (nqb, 1), 0) * SEL_BLOCK
    p_first = jnp.zeros((nqb, nb), F32)

    slopes = [slope_ref[0, r:r + 1, 0:1] for r in range(NSA_GROUP)]

    def body(i, carry):
        r0 = pl.multiple_of(i * tq, tq)
        qpos = r0 + lax.broadcasted_iota(I32, (tq, 1), 0)
        for r in range(NSA_GROUP):
            qn = _rms_rows(q_ref[pl.ds(r0, tq), r * Dh:(r + 1) * Dh].astype(F32), qg).astype(BF16)
            qn_ref[pl.ds(r0, tq), r * Dh:(r + 1) * Dh] = qn
            p = scores(qn, qpos, slopes[r])
            ocmp_ref[pl.ds(r0, tq), r * Dh:(r + 1) * Dh] = jnp.dot(
                p.astype(BF16), v_cmp, preferred_element_type=F32).astype(BF16)
        return carry

    lax.fori_loop(0, seq // tq, body, 0)
    for r in range(NSA_GROUP):
        q_first = jnp.dot(pick_first, qn_ref[:, r * Dh:(r + 1) * Dh], preferred_element_type=F32).astype(BF16)
        p_first = p_first + scores(q_first, first_pos, slopes[r])

    n_lo = lax.broadcasted_iota(I32, (nb, nqb), 0) * CMP_STRIDE
    j_lo = lax.broadcasted_iota(I32, (nb, nqb), 1) * SEL_BLOCK
    overlap = jnp.maximum(jnp.minimum(n_lo + CMP_LEN, j_lo + SEL_BLOCK) - jnp.maximum(n_lo, j_lo), 0)
    overlap = overlap.astype(F32) / CMP_LEN
    imp = jnp.dot(p_first, overlap, precision=lax.Precision.HIGHEST, preferred_element_type=F32)
    qb_i = lax.broadcasted_iota(I32, (nqb, nqb), 0)
    blk_j = lax.broadcasted_iota(I32, (nqb, nqb), 1)
    forced = (blk_j == 0) | (blk_j == qb_i) | (blk_j == qb_i - 1)
    imp = jnp.where(forced, FORCED_SCORE, jnp.where(blk_j > qb_i, -FORCED_SCORE, imp))
    out_lane = lax.broadcasted_iota(I32, (nqb, LANES), 1)
    picked = jnp.zeros((nqb, LANES), I32)
    for k in range(SEL_TOPK):
        best = jnp.max(imp, axis=-1, keepdims=True)
        arg = jnp.min(jnp.where(imp == best, blk_j, nqb), axis=-1, keepdims=True)
        picked = jnp.where(out_lane == k, arg, picked)
        imp = jnp.where(blk_j == arg, -jnp.inf, imp)
    idx_ref[0, 0] = picked


def _nsa_cmp(r5, proj, wk, wv, pos, kng, qng, slopes, bsz, seq):
    G = NSA_KV_HEADS
    nb = seq // CMP_STRIDE
    nqb = seq // SEL_BLOCK
    qw = NSA_GROUP * NSA_HEAD_DIM
    return pl.pallas_call(
        functools.partial(_nsa_cmp_kernel, seq=seq, tq=min(512, seq)),
        grid=(bsz, G),
        in_specs=[
            pl.BlockSpec((1, 1, 2, nb, CMP_STRIDE * NSA_HEAD_DIM), lambda b, g: (b, g, 0, 0, 0)),
            pl.BlockSpec((seq, qw), lambda b, g: (b, Q_OFF // qw + g)),
            pl.BlockSpec((2, CMP_STRIDE * NSA_HEAD_DIM, NSA_HEAD_DIM), lambda b, g: (0, 0, 0)),
            pl.BlockSpec((2, CMP_STRIDE * NSA_HEAD_DIM, NSA_HEAD_DIM), lambda b, g: (0, 0, 0)),
            pl.BlockSpec((2, 2, CMP_STRIDE * NSA_HEAD_DIM), lambda b, g: (0, 0, 0)),
            pl.BlockSpec((1, NSA_HEAD_DIM), lambda b, g: (0, 0)),
            pl.BlockSpec((1, NSA_HEAD_DIM), lambda b, g: (0, 0)),
            pl.BlockSpec((1, 8, LANES), lambda b, g: (g, 0, 0)),
        ],
        out_specs=[
            pl.BlockSpec((seq, qw), lambda b, g: (b, g)),
            pl.BlockSpec((seq, qw), lambda b, g: (b, g)),
            pl.BlockSpec((1, 1, nqb, LANES), lambda b, g: (b, g, 0, 0)),
        ],
        out_shape=[
            jax.ShapeDtypeStruct((bsz * seq, NSA_HEADS * NSA_HEAD_DIM), BF16),
            jax.ShapeDtypeStruct((bsz * seq, NSA_HEADS * NSA_HEAD_DIM), BF16),
            jax.ShapeDtypeStruct((bsz, G, nqb, LANES), I32),
        ],
        scratch_shapes=[pltpu.VMEM((NSA_HEAD_DIM, nb), BF16), pltpu.VMEM((nb, NSA_HEAD_DIM), BF16)],
        compiler_params=_params(("arbitrary", "arbitrary")),
        name="nsa_cmp",
    )(r5, proj, wk, wv, pos, kng, qng, slopes)


def _nsa_attn_kernel(idx_ref, slope_ref, qn_ref, ks_ref, vs_ref, kw_ref, vw_ref, ocmp_ref, gate_ref, kng_ref,
                     out_ref, ksn_scr, kwn_scr, vwp_scr, wdist_scr, s_scr, p_scr, *, seq, qps):
    Dh = NSA_HEAD_DIM
    R = NSA_GROUP
    QB = SEL_BLOCK
    nqb = seq // QB
    span = WINDOW + QB
    b = pl.program_id(0)
    g = pl.program_id(1)
    step = pl.program_id(2)
    q_in = lax.broadcasted_iota(I32, (QB, 1), 0)

    @pl.when(step == 0)
    def _():
        kwn_scr[0:WINDOW, :] = jnp.zeros((WINDOW, Dh), BF16)
        vwp_scr[0:WINDOW, :] = jnp.zeros((WINDOW, Dh), BF16)

        def body(i, carry):
            r0 = pl.multiple_of(i * 256, 256)
            ksn_scr[pl.ds(r0, 256), :] = _rms_rows(ks_ref[pl.ds(r0, 256), :].astype(F32), kng_ref[1:2, :]).astype(BF16)
            kwn_scr[pl.ds(WINDOW + r0, 256), :] = _rms_rows(kw_ref[pl.ds(r0, 256), :].astype(F32),
                                                           kng_ref[2:3, :]).astype(BF16)
            vwp_scr[pl.ds(WINDOW + r0, 256), :] = vw_ref[pl.ds(r0, 256), :]
            return carry

        lax.fori_loop(0, seq // 256, body, 0)
        dist_w = q_in + WINDOW - lax.broadcasted_iota(I32, (QB, span), 1)
        wdist_scr[...] = jnp.where((dist_w >= 0) & (dist_w < WINDOW), dist_w.astype(F32), MASKED_DIST)

    nk = SEL_TOPK * QB
    key_lane = lax.broadcasted_iota(I32, (1, nk), 1)
    win_lane = lax.broadcasted_iota(I32, (1, span), 1)
    lane = lax.broadcasted_iota(I32, (QB, LANES), 1)
    slopes = [slope_ref[g * R + r] for r in range(R)]

    def attend(slot, q4, k, v, dist, width):
        s_scr[slot, :, 0:width] = lax.dot_general(q4, k, (((1,), (1,)), ((), ())), preferred_element_type=F32)
        inv = []
        for r in range(R):
            s = s_scr[slot, r * QB:(r + 1) * QB, 0:width] - slopes[r] * dist
            p = jnp.exp2(s - jnp.max(s, axis=-1, keepdims=True))
            inv.append(1.0 / jnp.sum(p, axis=-1, keepdims=True))
            p_scr[slot, r * QB:(r + 1) * QB, 0:width] = p.astype(BF16)
        o = jnp.dot(p_scr[slot, :, 0:width], v, preferred_element_type=F32)
        return [o[r * QB:(r + 1) * QB] * inv[r] for r in range(R)]

    for i in range(qps):
        qb = step * qps + i
        rows = slice(i * QB, (i + 1) * QB)
        q4 = jnp.concatenate([qn_ref[rows, r * Dh:(r + 1) * Dh] for r in range(R)], axis=0)

        base = ((b * NSA_KV_HEADS + g) * nqb + qb) * SEL_TOPK
        kpos_s = key_lane & (QB - 1)
        ks, vs = [], []
        for k in range(SEL_TOPK):
            start = pl.multiple_of(idx_ref[base + k] * QB, QB)
            ks.append(ksn_scr[pl.ds(start, QB), :])
            vs.append(vs_ref[pl.ds(start, QB), :])
            kpos_s = kpos_s + jnp.where((key_lane >= k * QB) & (key_lane < (k + 1) * QB), start, 0)
        dist_s = (q_in + qb * QB - kpos_s).astype(F32)
        dist_s = jnp.where(dist_s >= 0, dist_s, MASKED_DIST)
        o_sel = attend(2 * (i % 2), q4, jnp.concatenate(ks, axis=0), jnp.concatenate(vs, axis=0), dist_s, nk)

        wstart = pl.multiple_of(qb * QB, QB)
        dist_w = jnp.where(win_lane >= WINDOW - qb * QB, wdist_scr[...], MASKED_DIST)
        o_win = attend(2 * (i % 2) + 1, q4, kwn_scr[pl.ds(wstart, span), :], vwp_scr[pl.ds(wstart, span), :],
                       dist_w, span)

        sig = _sigmoid(gate_ref[rows, :])

        def gate(br, r):
            return jnp.sum(jnp.where(lane == GATE_LANE0 + br * NSA_HEADS + g * R + r, sig, 0.0), axis=-1, keepdims=True)

        for r in range(R):
            o_cmp = ocmp_ref[rows, r * Dh:(r + 1) * Dh].astype(F32)
            o = gate(0, r) * o_cmp + gate(1, r) * o_sel[r] + gate(2, r) * o_win[r]
            out_ref[rows, r * Dh:(r + 1) * Dh] = o.astype(BF16)


def _nsa_attn(sel_flat, slopes, qn, proj, ocmp, small, kng, bsz, seq):
    G = NSA_KV_HEADS
    Dh = NSA_HEAD_DIM
    QB = SEL_BLOCK
    nqb = seq // QB
    qw = NSA_GROUP * Dh
    kv0 = KV_OFF // Dh

    def kv_spec(j):
        return pl.BlockSpec((seq, Dh), lambda b, g, q, idx, sl: (b, kv0 + j * G + g))

    qps = NSA_Q_BLOCKS_PER_STEP
    steps = nqb // qps
    tq = qps * QB
    span = WINDOW + QB
    grid_spec = pltpu.PrefetchScalarGridSpec(
        num_scalar_prefetch=2,
        grid=(bsz, G, steps),
        in_specs=[
            pl.BlockSpec((tq, qw), lambda b, g, q, idx, sl: (b * steps + q, g)),
            kv_spec(2), kv_spec(3), kv_spec(4), kv_spec(5),
            pl.BlockSpec((tq, qw), lambda b, g, q, idx, sl: (b * steps + q, g)),
            pl.BlockSpec((tq, SMALL_W), lambda b, g, q, idx, sl: (b * steps + q, 0)),
            pl.BlockSpec((N_BRANCH, Dh), lambda b, g, q, idx, sl: (0, 0)),
        ],
        out_specs=pl.BlockSpec((tq, qw), lambda b, g, q, idx, sl: (b * steps + q, g)),
        scratch_shapes=[
            pltpu.VMEM((seq, Dh), BF16),
            pltpu.VMEM((WINDOW + seq, Dh), BF16),
            pltpu.VMEM((WINDOW + seq, Dh), BF16),
            pltpu.VMEM((QB, span), F32),
            pltpu.VMEM((4, NSA_GROUP * QB, span), F32),
            pltpu.VMEM((4, NSA_GROUP * QB, span), BF16),
        ],
    )
    return pl.pallas_call(
        functools.partial(_nsa_attn_kernel, seq=seq, qps=qps),
        grid_spec=grid_spec,
        out_shape=jax.ShapeDtypeStruct((bsz * seq, NSA_HEADS * Dh), BF16),
        compiler_params=_params(("arbitrary", "arbitrary", "arbitrary")),
        name="nsa_attn",
    )(sel_flat, slopes, qn, proj, proj, proj, proj, ocmp, small, kng)


def _merge_kernel(yssm_ref, bg_ref, cg_ref, xi_ref, cgp_ref, xip_ref, ynsa_ref, scw_ref, w0_ref, w1_ref, w2_ref,
                  g0_ref, g1_ref, g2_ref, out_ref, ysc_scr, *, tm, tiles_per_seq, tn):
    keep_prev = (pl.program_id(0) % tiles_per_seq > 0).astype(F32)
    chunk = 128
    tail = cgp_ref.shape[0]

    def body(i, carry):
        r0 = pl.multiple_of(i * chunk, chunk)
        u = cg_ref[pl.ds(r0, chunk), :].astype(F32) * xi_ref[pl.ds(r0, chunk), :].astype(F32)
        p0 = pl.multiple_of(jnp.maximum(r0 - tail, 0), tail)
        prev_in = cg_ref[pl.ds(p0, tail), :].astype(F32) * xi_ref[pl.ds(p0, tail), :].astype(F32)
        prev_out = cgp_ref[...].astype(F32) * xip_ref[...].astype(F32) * keep_prev
        prev = jnp.where(i > 0, prev_in, prev_out)
        acc = u * scw_ref[SC_CONV - 1:SC_CONV, :]
        for s in range(1, SC_CONV):
            k = SC_CONV - 1 - s
            acc = acc + _shift_rows(u, prev, s) * scw_ref[k:k + 1, :]
        ysc_scr[pl.ds(r0, chunk), :] = (bg_ref[pl.ds(r0, chunk), :].astype(F32) * acc).astype(BF16)
        return carry

    lax.fori_loop(0, tm // chunk, body, 0)

    for n0 in range(0, D_MODEL, tn):
        cols = slice(n0, n0 + tn)
        a0 = jnp.dot(yssm_ref[...], w0_ref[:, cols], preferred_element_type=F32)
        a1 = jnp.dot(ysc_scr[...], w1_ref[:, cols], preferred_element_type=F32)
        a2 = jnp.dot(ynsa_ref[...], w2_ref[:, cols], preferred_element_type=F32)
        mixed = (_sigmoid(g0_ref[:, cols].astype(F32)) * a0 + _sigmoid(g1_ref[:, cols].astype(F32)) * a1
                 + _sigmoid(g2_ref[:, cols].astype(F32)) * a2)
        out_ref[:, cols] = mixed.astype(BF16)


def _merge(yssm, proj, ynsa, sc_w, w_ssm, w_sc, w_nsa, seq, tm=256, tn=512):
    m = yssm.shape[0]
    tm = min(tm, seq)
    tail = 16
    per = tm // tail
    sc0 = SC_OFF // D_MODEL
    mg0 = MG_OFF // D_MODEL
    rows = lambda c: pl.BlockSpec((tm, D_MODEL), lambda i: (i, c))
    tails = lambda c: pl.BlockSpec((tail, D_MODEL), lambda i: (jnp.maximum(i * per - 1, 0), c))
    wspec = pl.BlockSpec((D_MODEL, D_MODEL), lambda i: (0, 0), pipeline_mode=pl.Buffered(1))
    return pl.pallas_call(
        functools.partial(_merge_kernel, tm=tm, tiles_per_seq=seq // tm, tn=tn),
        grid=(m // tm,),
        in_specs=[
            rows(0),
            rows(sc0), rows(sc0 + 1), rows(sc0 + 2),
            tails(sc0 + 1), tails(sc0 + 2),
            rows(0),
            pl.BlockSpec((SC_CONV, D_MODEL), lambda i: (0, 0)),
            wspec, wspec, wspec,
            rows(mg0), rows(mg0 + 1), rows(mg0 + 2),
        ],
        out_specs=rows(0),
        out_shape=jax.ShapeDtypeStruct((m, D_MODEL), BF16),
        scratch_shapes=[pltpu.VMEM((tm, D_MODEL), BF16)],
        compiler_params=_params(("arbitrary",)),
        name="merge",
    )(yssm, proj, proj, proj, proj, proj, ynsa, sc_w, w_ssm, w_sc, w_nsa, proj, proj, proj)


def _outproj_kernel(a_ref, w_ref, x_ref, out_ref, *, tn):
    for n0 in range(0, D_MODEL, tn):
        cols = slice(n0, n0 + tn)
        out_ref[:, cols] = x_ref[:, cols] + jnp.dot(a_ref[...], w_ref[:, cols], preferred_element_type=F32)


def _outproj(a, w, x2, tm=512, tn=512):
    m = a.shape[0]
    tm = min(tm, m)
    rows = pl.BlockSpec((tm, D_MODEL), lambda i: (i, 0))
    return pl.pallas_call(
        functools.partial(_outproj_kernel, tn=tn),
        grid=(m // tm,),
        in_specs=[
            rows,
            pl.BlockSpec((D_MODEL, D_MODEL), lambda i: (0, 0), pipeline_mode=pl.Buffered(1)),
            rows,
        ],
        out_specs=rows,
        out_shape=jax.ShapeDtypeStruct((m, D_MODEL), F32),
        compiler_params=_params(("arbitrary",)),
        name="outproj",
    )(a, w, x2)


def _ffn_kernel(x_ref, g_ref, wg_ref, wu_ref, wd_ref, out_ref, h_scr, *, tm):
    @pl.when(pl.program_id(1) == 0)
    def _():
        _norm_to_scratch(x_ref, g_ref, h_scr, tm)
        out_ref[...] = x_ref[...]

    _swiglu_accumulate(h_scr, wg_ref[...], wu_ref[...], wd_ref[...], out_ref, tm)


def _swiglu_accumulate(h_ref, wg, wu, wd, acc_ref, rows, sub=SWIGLU_SUB_ROWS):
    for r0 in range(0, rows, sub):
        h = h_ref[r0:r0 + sub, :]
        a = jnp.dot(h, wg, preferred_element_type=F32)
        u = jnp.dot(h, wu, preferred_element_type=F32)
        acc_ref[r0:r0 + sub, :] += jnp.dot((_silu(a) * u).astype(BF16), wd, preferred_element_type=F32)


def _ffn(x2, g, wg, wu, wd, tm=1024, tf=512):
    m = x2.shape[0]
    tm = min(tm, m)
    return pl.pallas_call(
        functools.partial(_ffn_kernel, tm=tm),
        grid=(m // tm, D_FF // tf),
        in_specs=[
            pl.BlockSpec((tm, D_MODEL), lambda i, f: (i, 0)),
            pl.BlockSpec((1, D_MODEL), lambda i, f: (0, 0)),
            pl.BlockSpec((D_MODEL, tf), lambda i, f: (0, f)),
            pl.BlockSpec((D_MODEL, tf), lambda i, f: (0, f)),
            pl.BlockSpec((tf, D_MODEL), lambda i, f: (f, 0)),
        ],
        out_specs=pl.BlockSpec((tm, D_MODEL), lambda i, f: (i, 0)),
        out_shape=jax.ShapeDtypeStruct((m, D_MODEL), F32),
        scratch_shapes=[pltpu.VMEM((tm, D_MODEL), BF16)],
        compiler_params=_params(("arbitrary", "arbitrary")),
        name="ffn",
    )(x2, g, wg, wu, wd)


SLABS = D_MODEL // LANES


def _route_kernel(x_ref, g_ref, wr_ref, h_ref, idx_ref, wgt_ref):
    h = _rms_rows(x_ref[...], g_ref[...])
    h_ref[...] = h.astype(BF16)
    logits = jnp.dot(h, wr_ref[...], precision=lax.Precision.HIGHEST, preferred_element_type=F32)
    lane = lax.broadcasted_iota(I32, logits.shape, 1)
    logits = jnp.where(lane < N_EXPERTS, logits, -jnp.inf)
    v0 = jnp.max(logits, axis=-1, keepdims=True)
    i0 = jnp.min(jnp.where(logits == v0, lane, LANES), axis=-1, keepdims=True)
    rest = jnp.where(lane == i0, -jnp.inf, logits)
    v1 = jnp.max(rest, axis=-1, keepdims=True)
    i1 = jnp.min(jnp.where(rest == v1, lane, LANES), axis=-1, keepdims=True)
    e1 = jnp.exp(v1 - v0)
    w0 = 1.0 / (1.0 + e1)
    idx_ref[...] = jnp.where(lane == 0, i0, jnp.where(lane == 1, i1, 0))
    wgt_ref[...] = jnp.where(lane == 0, w0, jnp.where(lane == 1, e1 * w0, 0.0))


def _route(x2, g, w_router, tm=256):
    m = x2.shape[0]
    tm = min(tm, m)
    return pl.pallas_call(
        _route_kernel,
        grid=(m // tm,),
        in_specs=[
            pl.BlockSpec((tm, D_MODEL), lambda i: (i, 0)),
            pl.BlockSpec((1, D_MODEL), lambda i: (0, 0)),
            pl.BlockSpec((D_MODEL, LANES), lambda i: (0, 0)),
        ],
        out_specs=[
            pl.BlockSpec((tm, D_MODEL), lambda i: (i, 0)),
            pl.BlockSpec((tm, LANES), lambda i: (i, 0)),
            pl.BlockSpec((tm, LANES), lambda i: (i, 0)),
        ],
        out_shape=[
            jax.ShapeDtypeStruct((m, D_MODEL), BF16),
            jax.ShapeDtypeStruct((m, LANES), I32),
            jax.ShapeDtypeStruct((m, LANES), F32),
        ],
        compiler_params=_params(("arbitrary",)),
        name="route",
    )(x2, g, w_router)


def _gather_kernel(src_ref, in_ref, out_ref, sem, *, rows):
    base = pl.program_id(0) * rows

    def start(r, carry):
        pltpu.make_async_copy(in_ref.at[src_ref[base + r]], out_ref.at[r], sem).start()
        return carry

    lax.fori_loop(0, rows, start, 0, unroll=8)
    pltpu.make_async_copy(in_ref.at[pl.ds(0, rows)], out_ref, sem).wait()


def _gather_rows(src, table, n_out, rows=GATHER_ROWS):
    grid_spec = pltpu.PrefetchScalarGridSpec(
        num_scalar_prefetch=1,
        grid=(n_out // rows,),
        in_specs=[pl.BlockSpec(memory_space=pl.ANY)],
        out_specs=pl.BlockSpec((rows, SLABS, LANES), lambda i, src: (i, 0, 0)),
        scratch_shapes=[pltpu.SemaphoreType.DMA(())],
    )
    out = pl.pallas_call(
        functools.partial(_gather_kernel, rows=rows),
        grid_spec=grid_spec,
        out_shape=jax.ShapeDtypeStruct((n_out, SLABS, LANES), table.dtype),
        compiler_params=pltpu.CompilerParams(dimension_semantics=("arbitrary",)),
        name="gather_rows",
    )(src, table.reshape(table.shape[0], SLABS, LANES))
    return out.reshape(n_out, D_MODEL)


def _gffn_kernel(te_ref, tv_ref, hs_ref, wg_ref, wu_ref, wd_ref, out_ref, acc_scr, wg_scr, wu_scr, wd_scr):
    i = pl.program_id(0)
    f = pl.program_id(1)
    nf = pl.num_programs(1)

    @pl.when(f == 0)
    def _():
        acc_scr[...] = jnp.zeros_like(acc_scr)

    @pl.when(tv_ref[i] > 0)
    def _():
        wg_scr[...] = wg_ref[0].astype(BF16)
        wu_scr[...] = wu_ref[0].astype(BF16)
        wd_scr[...] = wd_ref[0].astype(BF16)
        rows = hs_ref.shape[0]
        _swiglu_accumulate(hs_ref, wg_scr[...], wu_scr[...], wd_scr[...], acc_scr, rows, sub=rows)

    @pl.when(f == nf - 1)
    def _():
        out_ref[...] = acc_scr[...].astype(BF16)


def _gffn(tile_expert, tile_valid, hs, wg, wu, wd, tm=MOE_TILE, tf=512):
    p = hs.shape[0]
    nf = D_FF // tf

    def fblk(f, tv, i):
        return jnp.where(tv[i] > 0, f, nf - 1)

    grid_spec = pltpu.PrefetchScalarGridSpec(
        num_scalar_prefetch=2,
        grid=(p // tm, nf),
        in_specs=[
            pl.BlockSpec((tm, D_MODEL), lambda i, f, te, tv: (i, 0)),
            pl.BlockSpec((1, D_MODEL, tf), lambda i, f, te, tv: (te[i], 0, fblk(f, tv, i))),
            pl.BlockSpec((1, D_MODEL, tf), lambda i, f, te, tv: (te[i], 0, fblk(f, tv, i))),
            pl.BlockSpec((1, tf, D_MODEL), lambda i, f, te, tv: (te[i], fblk(f, tv, i), 0)),
        ],
        out_specs=pl.BlockSpec((tm, D_MODEL), lambda i, f, te, tv: (i, 0)),
        scratch_shapes=[
            pltpu.VMEM((tm, D_MODEL), F32),
            pltpu.VMEM((D_MODEL, tf), BF16),
            pltpu.VMEM((D_MODEL, tf), BF16),
            pltpu.VMEM((tf, D_MODEL), BF16),
        ],
    )
    return pl.pallas_call(
        _gffn_kernel,
        grid_spec=grid_spec,
        out_shape=jax.ShapeDtypeStruct((p, D_MODEL), BF16),
        compiler_params=_params(("arbitrary", "arbitrary")),
        name="grouped_ffn",
    )(tile_expert, tile_valid, hs, wg, wu, wd)


def _combine_kernel(x_ref, y_ref, w_ref, out_ref):
    w = w_ref[...]
    w0 = w[:, 0:1]
    w1 = w[:, 1:2]
    out_ref[...] = (x_ref[...] + w0 * y_ref[:, :D_MODEL].astype(F32) + w1 * y_ref[:, D_MODEL:].astype(F32))


def _combine(x2, y_pairs, top_w, tm=512):
    m = x2.shape[0]
    tm = min(tm, m)
    return pl.pallas_call(
        _combine_kernel,
        grid=(m // tm,),
        in_specs=[
            pl.BlockSpec((tm, D_MODEL), lambda i: (i, 0)),
            pl.BlockSpec((tm, TOP_K * D_MODEL), lambda i: (i, 0)),
            pl.BlockSpec((tm, LANES), lambda i: (i, 0)),
        ],
        out_specs=pl.BlockSpec((tm, D_MODEL), lambda i: (i, 0)),
        out_shape=jax.ShapeDtypeStruct((m, D_MODEL), F32),
        compiler_params=_params(("arbitrary",)),
        name="combine",
    )(x2, y_pairs, top_w)


def _moe(x2, g, w_router, wg, wu, wd):
    m = x2.shape[0]
    tm = MOE_TILE
    h, top_idx, top_w = _route(x2, g, w_router)
    e_flat = top_idx[:, :TOP_K].reshape(-1)
    onehot = (e_flat[:, None] == jnp.arange(N_EXPERTS, dtype=I32)[None, :]).astype(I32)
    ranks = jnp.cumsum(onehot, axis=0) - onehot
    rank = jnp.sum(ranks * onehot, axis=1)
    counts = jnp.sum(onehot, axis=0)
    padded = ((counts + tm - 1) // tm) * tm
    seg_end = jnp.cumsum(padded)
    seg_start = seg_end - padded
    pos = seg_start[e_flat] + rank
    step = math.lcm(tm, GATHER_ROWS)
    p_max = -(-(TOP_K * m + N_EXPERTS * tm) // step) * step
    n_tiles = p_max // tm
    src = (jnp.arange(p_max, dtype=I32) % m).at[pos].set(jnp.arange(TOP_K * m, dtype=I32) // TOP_K)
    tile_start = jnp.arange(n_tiles, dtype=I32) * tm
    tile_valid = (tile_start < seg_end[-1]).astype(I32)
    last_expert = jnp.max(jnp.where(counts > 0, jnp.arange(N_EXPERTS, dtype=I32), 0))
    tile_expert = jnp.sum((tile_start[:, None] >= seg_end[None, :]).astype(I32), axis=1)
    tile_expert = jnp.where(tile_valid > 0, tile_expert, last_expert)

    hs = _gather_rows(src, h, p_max)
    ys = _gffn(tile_expert, tile_valid, hs, wg, wu, wd)
    y_pairs = _gather_rows(pos, ys, TOP_K * m).reshape(m, TOP_K * D_MODEL)
    return _combine(x2, y_pairs, top_w)


def _pad_row(v, width):
    return jnp.pad(v.astype(F32), (0, width - v.shape[0])).reshape(1, width)


def _alibi_tables():
    h = jnp.arange(1, NSA_HEADS + 1, dtype=F32)
    slopes = (jnp.exp2(-8.0 * h / NSA_HEADS) * LOG2E).reshape(NSA_KV_HEADS, NSA_GROUP)
    lanes_tbl = jnp.broadcast_to(jnp.pad(slopes, ((0, 0), (0, 8 - NSA_GROUP)))[:, :, None], (NSA_KV_HEADS, 8, LANES))
    return lanes_tbl, slopes.reshape(-1)


def _mixer_layer(x2, bsz, seq, norm_mix, w_in, ssm_conv_w, ssm_conv_b, ssm_dt_bias, ssm_a_log, ssm_d, ssm_norm,
                 w_ssm_out, sc_conv_w, w_sc_out, q_norm, k_norm, cmp_pos, w_cmp_k, w_cmp_v, w_nsa_out, w_out):
    w_big = jnp.concatenate(
        [w_in[:, _SRC_XBC:_SRC_DT], w_in[:, _SRC_Z:_SRC_XBC], w_in[:, _SRC_SC:_SRC_KV], w_in[:, _SRC_MG:],
         w_in[:, _SRC_KV:_SRC_NG]], axis=1).astype(BF16)
    w_small = jnp.concatenate(
        [w_in[:, _SRC_DT:_SRC_SC], w_in[:, _SRC_NG:_SRC_MG],
         jnp.zeros((D_MODEL, SMALL_W - SSM_HEADS - N_BRANCH * NSA_HEADS), F32)], axis=1).astype(BF16)
    proj, small = _inproj(x2, norm_mix.reshape(1, D_MODEL), w_big, w_small)

    y_ssm = _ssd(proj, small, ssm_conv_w, ssm_conv_b.reshape(1, SSM_XBC), _pad_row(ssm_dt_bias, SMALL_W),
                 _pad_row(ssm_a_log, SMALL_W), jnp.repeat(ssm_d, SSM_HEAD_DIM).reshape(1, D_MODEL),
                 ssm_norm.reshape(1, D_MODEL), bsz, seq)

    G, Dh = NSA_KV_HEADS, NSA_HEAD_DIM
    nb = seq // CMP_STRIDE
    r5 = proj[:, KV_OFF:KV_OFF + 2 * G * Dh].reshape(bsz, nb, CMP_STRIDE, 2, G, Dh)
    r5 = r5.transpose(0, 4, 3, 1, 2, 5).reshape(bsz, G, 2, nb, CMP_STRIDE * Dh)
    lanes_tbl, slopes_flat = _alibi_tables()
    half_k = CMP_STRIDE * Dh
    o_cmp, qn, sel = _nsa_cmp(r5, proj, w_cmp_k.reshape(2, half_k, Dh).astype(BF16),
                              w_cmp_v.reshape(2, half_k, Dh).astype(BF16), cmp_pos.reshape(2, 2, half_k),
                              k_norm[0:1], q_norm.reshape(1, Dh), lanes_tbl, bsz, seq)
    sel_flat = sel[..., :SEL_TOPK].reshape(-1)
    y_nsa = _nsa_attn(sel_flat, slopes_flat, qn, proj, o_cmp, small, k_norm, bsz, seq)

    mixed = _merge(y_ssm, proj, y_nsa, sc_conv_w, w_ssm_out.astype(BF16), w_sc_out.astype(BF16),
                   w_nsa_out.astype(BF16), seq)
    return _outproj(mixed, w_out.astype(BF16), x2)


def kernel(x, norm_mix, w_in, ssm_conv_w, ssm_conv_b, ssm_dt_bias, ssm_a_log, ssm_d, ssm_norm, w_ssm_out,
           sc_conv_w, w_sc_out, q_norm, k_norm, cmp_pos, w_cmp_k, w_cmp_v, w_nsa_out, w_out, norm_ffn,
           ffn_w_gate, ffn_w_up, ffn_w_down, moe_router, moe_w_gate, moe_w_up, moe_w_down):
    bsz, seq, _ = x.shape
    depth = norm_mix.shape[0]
    x2 = x.reshape(bsz * seq, D_MODEL)
    for layer in range(depth):
        x2 = _mixer_layer(x2, bsz, seq, norm_mix[layer], w_in[layer], ssm_conv_w[layer], ssm_conv_b[layer],
                          ssm_dt_bias[layer], ssm_a_log[layer], ssm_d[layer], ssm_norm[layer], w_ssm_out[layer],
                          sc_conv_w[layer], w_sc_out[layer], q_norm[layer], k_norm[layer], cmp_pos[layer],
                          w_cmp_k[layer], w_cmp_v[layer], w_nsa_out[layer], w_out[layer])
        g = norm_ffn[layer].reshape(1, D_MODEL)
        i = layer // 2
        if layer % 2 == 0:
            x2 = _ffn(x2, g, ffn_w_gate[i].astype(BF16), ffn_w_up[i].astype(BF16), ffn_w_down[i].astype(BF16))
        else:
            w_router = jnp.pad(moe_router[i], ((0, 0), (0, LANES - N_EXPERTS)))
            x2 = _moe(x2, g, w_router, moe_w_gate[i], moe_w_up[i], moe_w_down[i])
    return x2.reshape(bsz, seq, D_MODEL)
```

```python
import functools
import math

import jax
import jax.numpy as jnp
from jax import lax
from jax.experimental import pallas as pl
from jax.experimental.pallas import tpu as pltpu

F32 = jnp.float32
BF16 = jnp.bfloat16
I32 = jnp.int32

D_MODEL = 2048
EPS = 1e-6
NEG_INF = -1e30
MASKED_DIST = 1e30
LOG2E = 1.4426950408889634
FORCED_SCORE = 1e4
SSM_HEAD_DIM = 64
SSM_HEADS = D_MODEL // SSM_HEAD_DIM
SSM_GROUPS = 8
SSM_HEADS_PER_GROUP = SSM_HEADS // SSM_GROUPS
SSM_STATE = 128
SSM_CONV = 4
SSM_CHUNK = 128
SSM_XBC = D_MODEL + 2 * SSM_GROUPS * SSM_STATE
SC_CONV = 3
NSA_HEADS = 16
NSA_HEAD_DIM = 128
NSA_KV_HEADS = 4
NSA_GROUP = NSA_HEADS // NSA_KV_HEADS
CMP_STRIDE = 16
CMP_LEN = 2 * CMP_STRIDE
SEL_BLOCK = 64
SEL_TOPK = 8
WINDOW = 512
N_BRANCH = 3
D_FF = 5632
N_EXPERTS = 8
TOP_K = 2

XBC_OFF = 0
Z_OFF = XBC_OFF + SSM_XBC
SC_OFF = Z_OFF + D_MODEL
MG_OFF = SC_OFF + 3 * D_MODEL
Q_OFF = MG_OFF + N_BRANCH * D_MODEL
KV_OFF = Q_OFF + NSA_HEADS * NSA_HEAD_DIM
PROJ_W = KV_OFF + 6 * NSA_KV_HEADS * NSA_HEAD_DIM
LANES = 128
SMALL_W = LANES
GATE_LANE0 = SSM_HEADS
_SRC_Z = 0
_SRC_XBC = D_MODEL
_SRC_DT = _SRC_XBC + SSM_XBC
_SRC_SC = _SRC_DT + SSM_HEADS
_SRC_Q = _SRC_SC + 3 * D_MODEL
_SRC_KV = _SRC_Q + NSA_HEADS * NSA_HEAD_DIM
_SRC_NG = _SRC_KV + 6 * NSA_KV_HEADS * NSA_HEAD_DIM
_SRC_MG = _SRC_NG + N_BRANCH * NSA_HEADS

VMEM_LIMIT = 56 * 1024 * 1024
MOE_TILE = 704
GATHER_ROWS = 512
NSA_Q_BLOCKS_PER_STEP = 8
SWIGLU_SUB_ROWS = 512


def _params(sem, vmem=VMEM_LIMIT):
    return pltpu.CompilerParams(dimension_semantics=sem, vmem_limit_bytes=vmem)


def _sigmoid(x):
    return 1.0 / (1.0 + jnp.exp2(x * (-LOG2E)))


def _silu(x):
    return x * _sigmoid(x)


def _rms_rows(x, g):
    return x * lax.rsqrt(jnp.mean(x * x, axis=-1, keepdims=True) + EPS) * g


def _norm_to_scratch(x_ref, g_ref, h_scr, rows, chunk=256):
    g = g_ref[...]

    def body(i, carry):
        r0 = pl.multiple_of(i * chunk, chunk)
        h_scr[pl.ds(r0, chunk), :] = _rms_rows(x_ref[pl.ds(r0, chunk), :], g).astype(BF16)
        return carry

    lax.fori_loop(0, rows // chunk, body, 0)


def _inproj_kernel(x_ref, g_ref, w_ref, ws_ref, proj_ref, small_ref, h_scr, *, tm):
    @pl.when(pl.program_id(1) == 0)
    def _():
        _norm_to_scratch(x_ref, g_ref, h_scr, tm)
        small_ref[...] = jnp.dot(h_scr[...], ws_ref[...], preferred_element_type=F32)

    proj_ref[...] = jnp.dot(h_scr[...], w_ref[...], preferred_element_type=F32).astype(BF16)


def _inproj(x2, g, w_big, w_small, tm=1024, tn=1024):
    m = x2.shape[0]
    tm = min(tm, m)
    return pl.pallas_call(
        functools.partial(_inproj_kernel, tm=tm),
        grid=(m // tm, PROJ_W // tn),
        in_specs=[
            pl.BlockSpec((tm, D_MODEL), lambda i, j: (i, 0)),
            pl.BlockSpec((1, D_MODEL), lambda i, j: (0, 0)),
            pl.BlockSpec((D_MODEL, tn), lambda i, j: (0, j)),
            pl.BlockSpec((D_MODEL, SMALL_W), lambda i, j: (0, 0)),
        ],
        out_specs=[
            pl.BlockSpec((tm, tn), lambda i, j: (i, j)),
            pl.BlockSpec((tm, SMALL_W), lambda i, j: (i, 0)),
        ],
        out_shape=[
            jax.ShapeDtypeStruct((m, PROJ_W), BF16),
            jax.ShapeDtypeStruct((m, SMALL_W), F32),
        ],
        scratch_shapes=[pltpu.VMEM((tm, D_MODEL), BF16)],
        compiler_params=_params(("arbitrary", "arbitrary")),
        name="inproj",
    )(x2, g, w_big, w_small)


def _shift_rows(cur, prev_tail, s):
    rc = pltpu.roll(cur, s, 0)
    rp = pltpu.roll(prev_tail, s, 0)
    row = lax.broadcasted_iota(I32, (8, cur.shape[1]), 0)
    top = jnp.where(row < s, rp[0:8], rc[0:8])
    return jnp.concatenate([top, rc[8:]], axis=0)


def _ssd_kernel(xbc_ref, prev_ref, z_ref, small_ref, cw_ref, cb_ref, dtb_ref, alog_ref, dskip_ref,
                ng_ref, out_ref, state_scr, y_scr):
    L = SSM_CHUNK
    P = SSM_HEAD_DIM
    R = SSM_HEADS_PER_GROUP
    GW = R * P
    c = pl.program_id(1)

    @pl.when(c == 0)
    def _():
        state_scr[...] = jnp.zeros_like(state_scr)

    keep_prev = (c > 0).astype(F32)

    def conv_silu(lo, w):
        cur = xbc_ref[:, lo:lo + w].astype(F32)
        prev = prev_ref[:, lo:lo + w].astype(F32) * keep_prev
        acc = cur * cw_ref[SSM_CONV - 1:SSM_CONV, lo:lo + w] + cb_ref[:, lo:lo + w]
        for s in range(1, SSM_CONV):
            k = SSM_CONV - 1 - s
            acc = acc + _shift_rows(cur, prev, s) * cw_ref[k:k + 1, lo:lo + w]
        return _silu(acc)

    row = lax.broadcasted_iota(I32, (L, L), 0)
    col = lax.broadcasted_iota(I32, (L, L), 1)
    causal = row >= col
    tril = causal.astype(F32)

    pre = small_ref[...] + dtb_ref[...]
    dt = jnp.maximum(pre, 0.0) + jnp.log(1.0 + jnp.exp(-jnp.abs(pre)))
    dt = jnp.where(col < SSM_HEADS, dt, 0.0)
    a_neg = jnp.exp(alog_ref[...]) * (-LOG2E)
    a_cum = jnp.dot(tril, dt * a_neg, precision=lax.Precision.HIGHEST, preferred_element_type=F32)
    a_cum_t = a_cum.T
    a_last = a_cum[L - 1:L, :]
    dec_end = jnp.exp2(a_last - a_cum)
    dec_in = jnp.exp2(a_cum)
    chunk_dec = jnp.exp2(a_last)

    lane_g = lax.broadcasted_iota(I32, (L, GW), 1)
    lane_g1 = lax.broadcasted_iota(I32, (1, GW), 1)
    head_mask = [jnp.where((lane_g >= r * P) & (lane_g < (r + 1) * P), 1.0, 0.0).astype(BF16) for r in range(R)]

    def expand(mat, g, lanes):
        out = mat[:, R * g + R - 1:R * g + R]
        for r in range(R - 2, -1, -1):
            out = jnp.where(lanes < (r + 1) * P, mat[:, R * g + r:R * g + r + 1], out)
        return out

    for g in range(SSM_GROUPS):
        xs = conv_silu(g * GW, GW)
        b_in = conv_silu(D_MODEL + g * SSM_STATE, SSM_STATE)
        c_out = conv_silu(D_MODEL + (SSM_GROUPS + g) * SSM_STATE, SSM_STATE)
        b_bf = b_in.astype(BF16)
        c_bf = c_out.astype(BF16)
        cb = lax.dot_general(c_bf, b_bf, (((1,), (1,)), ((), ())), preferred_element_type=F32)
        xdt = xs * expand(dt, g, lane_g)
        xdt_bf = xdt.astype(BF16)
        y = xs * dskip_ref[:, g * GW:(g + 1) * GW]
        for r in range(R):
            h = R * g + r
            seg = a_cum[:, h:h + 1] - a_cum_t[h:h + 1, :]
            lmat = jnp.where(causal, jnp.exp2(seg), 0.0)
            y = y + jnp.dot((cb * lmat).astype(BF16), xdt_bf * head_mask[r], preferred_element_type=F32)
        h_prev = state_scr[g]
        y = y + jnp.dot(c_bf, h_prev.astype(BF16), preferred_element_type=F32) * expand(dec_in, g, lane_g)
        xdt_end = (xdt * expand(dec_end, g, lane_g)).astype(BF16)
        new_state = jnp.dot(b_in.T.astype(BF16), xdt_end, preferred_element_type=F32)
        state_scr[g] = h_prev * expand(chunk_dec, g, lane_g1) + new_state
        y_scr[:, g * GW:(g + 1) * GW] = y

    yz = y_scr[...] * _silu(z_ref[...].astype(F32))
    out_ref[...] = _rms_rows(yz, ng_ref[...]).astype(BF16)


def _ssd(proj, small, conv_w, conv_b, dt_bias, a_log, d_skip, norm_g, bsz, seq):
    L = SSM_CHUNK
    nc = seq // L
    tail = 16
    per = L // tail
    return pl.pallas_call(
        _ssd_kernel,
        grid=(bsz, nc),
        in_specs=[
            pl.BlockSpec((L, SSM_XBC), lambda b, c: (b * nc + c, XBC_OFF // SSM_XBC)),
            pl.BlockSpec((tail, SSM_XBC), lambda b, c: (jnp.maximum((b * nc + c) * per - 1, 0), XBC_OFF // SSM_XBC)),
            pl.BlockSpec((L, D_MODEL), lambda b, c: (b * nc + c, Z_OFF // D_MODEL)),
            pl.BlockSpec((L, SMALL_W), lambda b, c: (b * nc + c, 0)),
            pl.BlockSpec((SSM_CONV, SSM_XBC), lambda b, c: (0, 0)),
            pl.BlockSpec((1, SSM_XBC), lambda b, c: (0, 0)),
            pl.BlockSpec((1, SMALL_W), lambda b, c: (0, 0)),
            pl.BlockSpec((1, SMALL_W), lambda b, c: (0, 0)),
            pl.BlockSpec((1, D_MODEL), lambda b, c: (0, 0)),
            pl.BlockSpec((1, D_MODEL), lambda b, c: (0, 0)),
        ],
        out_specs=pl.BlockSpec((L, D_MODEL), lambda b, c: (b * nc + c, 0)),
        out_shape=jax.ShapeDtypeStruct((bsz * seq, D_MODEL), BF16),
        scratch_shapes=[
            pltpu.VMEM((SSM_GROUPS, SSM_STATE, SSM_HEADS_PER_GROUP * SSM_HEAD_DIM), F32),
            pltpu.VMEM((L, D_MODEL), F32),
        ],
        compiler_params=_params(("arbitrary", "arbitrary")),
        name="ssd",
    )(proj, proj, proj, small, conv_w, conv_b, dt_bias, a_log, d_skip, norm_g)


def _nsa_cmp_kernel(kc_ref, vc_ref, q_ref, wk_ref, wv_ref, pos_ref, kng_ref, qng_ref, slope_ref,
                    ocmp_ref, qn_ref, idx_ref, kc_scr, vc_scr, t_scr, top_scr, bot_scr, *, seq, tq):
    Dh = NSA_HEAD_DIM
    nb = seq // CMP_STRIDE
    nqb = seq // SEL_BLOCK

    def compress(j, t_ref, w_ref):
        t_scr[...] = t_ref[0, 0].astype(F32)
        for l in range(CMP_STRIDE):
            cols = slice(l * Dh, (l + 1) * Dh)
            t = t_scr[pl.ds(l, nb, stride=CMP_STRIDE), :]
            top_scr[:, cols] = (t + pos_ref[j, 0:1, cols]).astype(BF16)
            bot_scr[:, cols] = (t + pos_ref[j, 1:2, cols]).astype(BF16)
        top = jnp.dot(top_scr[...], w_ref[0], preferred_element_type=F32)
        bot = jnp.dot(bot_scr[...], w_ref[1], preferred_element_type=F32)
        return top + pltpu.roll(bot, nb - 1, 0)

    kc_scr[...] = _rms_rows(compress(0, kc_ref, wk_ref), kng_ref[...]).T.astype(BF16)
    vc_scr[...] = compress(1, vc_ref, wv_ref).astype(BF16)
    k_cmp_t = kc_scr[...]
    v_cmp = vc_scr[...]

    n_start = lax.broadcasted_iota(I32, (1, nb), 1) * CMP_STRIDE
    qg = qng_ref[...] * (Dh ** -0.5 * LOG2E)

    def scores(qn, qpos, slope):
        s = jnp.dot(qn, k_cmp_t, preferred_element_type=F32)
        dist = (qpos - n_start).astype(F32) - (CMP_LEN - 1) / 2
        valid = (n_start + (CMP_LEN - 1)) <= qpos
        s = jnp.where(valid, s - slope * dist, NEG_INF)
        p = jnp.exp2(s - jnp.max(s, axis=-1, keepdims=True))
        p = p / jnp.sum(p, axis=-1, keepdims=True)
        return jnp.where(valid, p, 0.0)

    sel_row = lax.broadcasted_iota(I32, (nqb, seq), 0) * SEL_BLOCK
    sel_col = lax.broadcasted_iota(I32, (nqb, seq), 1)
    pick_first = jnp.where(sel_row == sel_col, 1.0, 0.0).astype(BF16)
    first_pos = lax.broadcasted_iota(I32, (nqb, 1), 0) * SEL_BLOCK
    p_first = jnp.zeros((nqb, nb), F32)

    slopes = [slope_ref[0, r:r + 1, 0:1] for r in range(NSA_GROUP)]

    def body(i, carry):
        r0 = pl.multiple_of(i * tq, tq)
        qpos = r0 + lax.broadcasted_iota(I32, (tq, 1), 0)
        for r in range(NSA_GROUP):
            qn = _rms_rows(q_ref[pl.ds(r0, tq), r * Dh:(r + 1) * Dh].astype(F32), qg).astype(BF16)
            qn_ref[pl.ds(r0, tq), r * Dh:(r + 1) * Dh] = qn
            p = scores(qn, qpos, slopes[r])
            ocmp_ref[pl.ds(r0, tq), r * Dh:(r + 1) * Dh] = jnp.dot(
                p.astype(BF16), v_cmp, preferred_element_type=F32).astype(BF16)
        return carry

    lax.fori_loop(0, seq // tq, body, 0)
    for r in range(NSA_GROUP):
        q_first = jnp.dot(pick_first, qn_ref[:, r * Dh:(r + 1) * Dh], preferred_element_type=F32).astype(BF16)
        p_first = p_first + scores(q_first, first_pos, slopes[r])

    n_lo = lax.broadcasted_iota(I32, (nb, nqb), 0) * CMP_STRIDE
    j_lo = lax.broadcasted_iota(I32, (nb, nqb), 1) * SEL_BLOCK
    overlap = jnp.maximum(jnp.minimum(n_lo + CMP_LEN, j_lo + SEL_BLOCK) - jnp.maximum(n_lo, j_lo), 0)
    overlap = overlap.astype(F32) / CMP_LEN
    imp = jnp.dot(p_first, overlap, precision=lax.Precision.HIGHEST, preferred_element_type=F32)
    qb_i = lax.broadcasted_iota(I32, (nqb, nqb), 0)
    blk_j = lax.broadcasted_iota(I32, (nqb, nqb), 1)
    forced = (blk_j == 0) | (blk_j == qb_i) | (blk_j == qb_i - 1)
    imp = jnp.where(forced, FORCED_SCORE, jnp.where(blk_j > qb_i, -FORCED_SCORE, imp))
    out_lane = lax.broadcasted_iota(I32, (nqb, LANES), 1)
    picked = jnp.zeros((nqb, LANES), I32)
    for k in range(SEL_TOPK):
        best = jnp.max(imp, axis=-1, keepdims=True)
        arg = jnp.min(jnp.where(imp == best, blk_j, nqb), axis=-1, keepdims=True)
        picked = jnp.where(out_lane == k, arg, picked)
        imp = jnp.where(blk_j == arg, -jnp.inf, imp)
    idx_ref[0, 0] = picked


def _nsa_cmp(kv_t, proj, wk, wv, pos, kng, qng, slopes, bsz, seq):
    G = NSA_KV_HEADS
    nb = seq // CMP_STRIDE
    nqb = seq // SEL_BLOCK
    qw = NSA_GROUP * NSA_HEAD_DIM
    half_k = CMP_STRIDE * NSA_HEAD_DIM
    return pl.pallas_call(
        functools.partial(_nsa_cmp_kernel, seq=seq, tq=min(512, seq)),
        grid=(bsz, G),
        in_specs=[
            pl.BlockSpec((1, 1, seq, NSA_HEAD_DIM), lambda b, g: (b, g, 0, 0)),
            pl.BlockSpec((1, 1, seq, NSA_HEAD_DIM), lambda b, g: (b, G + g, 0, 0)),
            pl.BlockSpec((seq, qw), lambda b, g: (b, Q_OFF // qw + g)),
            pl.BlockSpec((2, CMP_STRIDE * NSA_HEAD_DIM, NSA_HEAD_DIM), lambda b, g: (0, 0, 0)),
            pl.BlockSpec((2, CMP_STRIDE * NSA_HEAD_DIM, NSA_HEAD_DIM), lambda b, g: (0, 0, 0)),
            pl.BlockSpec((2, 2, CMP_STRIDE * NSA_HEAD_DIM), lambda b, g: (0, 0, 0)),
            pl.BlockSpec((1, NSA_HEAD_DIM), lambda b, g: (0, 0)),
            pl.BlockSpec((1, NSA_HEAD_DIM), lambda b, g: (0, 0)),
            pl.BlockSpec((1, 8, LANES), lambda b, g: (g, 0, 0)),
        ],
        out_specs=[
            pl.BlockSpec((seq, qw), lambda b, g: (b, g)),
            pl.BlockSpec((seq, qw), lambda b, g: (b, g)),
            pl.BlockSpec((1, 1, nqb, LANES), lambda b, g: (b, g, 0, 0)),
        ],
        out_shape=[
            jax.ShapeDtypeStruct((bsz * seq, NSA_HEADS * NSA_HEAD_DIM), BF16),
            jax.ShapeDtypeStruct((bsz * seq, NSA_HEADS * NSA_HEAD_DIM), BF16),
            jax.ShapeDtypeStruct((bsz, G, nqb, LANES), I32),
        ],
        scratch_shapes=[
            pltpu.VMEM((NSA_HEAD_DIM, nb), BF16),
            pltpu.VMEM((nb, NSA_HEAD_DIM), BF16),
            pltpu.VMEM((seq, NSA_HEAD_DIM), F32),
            pltpu.VMEM((nb, half_k), BF16),
            pltpu.VMEM((nb, half_k), BF16),
        ],
        compiler_params=_params(("arbitrary", "arbitrary")),
        name="nsa_cmp",
    )(kv_t, kv_t, proj, wk, wv, pos, kng, qng, slopes)


def _nsa_attn_kernel(idx_ref, slope_ref, qn_ref, ks_ref, vs_ref, kw_ref, vw_ref, ocmp_ref, gate_ref, kng_ref,
                     out_ref, ksn_scr, kwn_scr, vwp_scr, wdist_scr, s_scr, p_scr, *, seq, qps):
    Dh = NSA_HEAD_DIM
    R = NSA_GROUP
    QB = SEL_BLOCK
    nqb = seq // QB
    span = WINDOW + QB
    b = pl.program_id(0)
    g = pl.program_id(1)
    step = pl.program_id(2)
    q_in = lax.broadcasted_iota(I32, (QB, 1), 0)

    @pl.when(step == 0)
    def _():
        kwn_scr[0:WINDOW, :] = jnp.zeros((WINDOW, Dh), BF16)
        vwp_scr[0:WINDOW, :] = jnp.zeros((WINDOW, Dh), BF16)

        def body(i, carry):
            r0 = pl.multiple_of(i * 256, 256)
            ksn_scr[pl.ds(r0, 256), :] = _rms_rows(ks_ref[0, 0, pl.ds(r0, 256), :].astype(F32),
                                                   kng_ref[1:2, :]).astype(BF16)
            kwn_scr[pl.ds(WINDOW + r0, 256), :] = _rms_rows(kw_ref[0, 0, pl.ds(r0, 256), :].astype(F32),
                                                           kng_ref[2:3, :]).astype(BF16)
            vwp_scr[pl.ds(WINDOW + r0, 256), :] = vw_ref[0, 0, pl.ds(r0, 256), :]
            return carry

        lax.fori_loop(0, seq // 256, body, 0)
        dist_w = q_in + WINDOW - lax.broadcasted_iota(I32, (QB, span), 1)
        wdist_scr[...] = jnp.where((dist_w >= 0) & (dist_w < WINDOW), dist_w.astype(F32), MASKED_DIST)

    nk = SEL_TOPK * QB
    key_lane = lax.broadcasted_iota(I32, (1, nk), 1)
    win_lane = lax.broadcasted_iota(I32, (1, span), 1)
    lane = lax.broadcasted_iota(I32, (QB, LANES), 1)
    slopes = [slope_ref[g * R + r] for r in range(R)]

    def attend(slot, q4, k, v, dist, width):
        s_scr[slot, :, 0:width] = lax.dot_general(q4, k, (((1,), (1,)), ((), ())), preferred_element_type=F32)
        inv = []
        for r in range(R):
            s = s_scr[slot, r * QB:(r + 1) * QB, 0:width] - slopes[r] * dist
            p = jnp.exp2(s - jnp.max(s, axis=-1, keepdims=True))
            inv.append(1.0 / jnp.sum(p, axis=-1, keepdims=True))
            p_scr[slot, r * QB:(r + 1) * QB, 0:width] = p.astype(BF16)
        o = jnp.dot(p_scr[slot, :, 0:width], v, preferred_element_type=F32)
        return [o[r * QB:(r + 1) * QB] * inv[r] for r in range(R)]

    for i in range(qps):
        qb = step * qps + i
        rows = slice(i * QB, (i + 1) * QB)
        q4 = jnp.concatenate([qn_ref[rows, r * Dh:(r + 1) * Dh] for r in range(R)], axis=0)

        base = ((b * NSA_KV_HEADS + g) * nqb + qb) * SEL_TOPK
        kpos_s = key_lane & (QB - 1)
        ks, vs = [], []
        for k in range(SEL_TOPK):
            start = pl.multiple_of(idx_ref[base + k] * QB, QB)
            ks.append(ksn_scr[pl.ds(start, QB), :])
            vs.append(vs_ref[0, 0, pl.ds(start, QB), :])
            kpos_s = kpos_s + jnp.where((key_lane >= k * QB) & (key_lane < (k + 1) * QB), start, 0)
        dist_s = (q_in + qb * QB - kpos_s).astype(F32)
        dist_s = jnp.where(dist_s >= 0, dist_s, MASKED_DIST)
        o_sel = attend(2 * (i % 2), q4, jnp.concatenate(ks, axis=0), jnp.concatenate(vs, axis=0), dist_s, nk)

        wstart = pl.multiple_of(qb * QB, QB)
        dist_w = jnp.where(win_lane >= WINDOW - qb * QB, wdist_scr[...], MASKED_DIST)
        o_win = attend(2 * (i % 2) + 1, q4, kwn_scr[pl.ds(wstart, span), :], vwp_scr[pl.ds(wstart, span), :],
                       dist_w, span)

        sig = _sigmoid(gate_ref[rows, :])

        def gate(br, r):
            return jnp.sum(jnp.where(lane == GATE_LANE0 + br * NSA_HEADS + g * R + r, sig, 0.0), axis=-1, keepdims=True)

        for r in range(R):
            o_cmp = ocmp_ref[rows, r * Dh:(r + 1) * Dh].astype(F32)
            o = gate(0, r) * o_cmp + gate(1, r) * o_sel[r] + gate(2, r) * o_win[r]
            out_ref[rows, r * Dh:(r + 1) * Dh] = o.astype(BF16)


def _nsa_attn(sel_flat, slopes, qn, kv_t, ocmp, small, kng, bsz, seq):
    G = NSA_KV_HEADS
    Dh = NSA_HEAD_DIM
    QB = SEL_BLOCK
    nqb = seq // QB
    qw = NSA_GROUP * Dh

    def kv_spec(j):
        return pl.BlockSpec((1, 1, seq, Dh), lambda b, g, q, idx, sl: (b, j * G + g, 0, 0))

    qps = NSA_Q_BLOCKS_PER_STEP
    steps = nqb // qps
    tq = qps * QB
    span = WINDOW + QB
    grid_spec = pltpu.PrefetchScalarGridSpec(
        num_scalar_prefetch=2,
        grid=(bsz, G, steps),
        in_specs=[
            pl.BlockSpec((tq, qw), lambda b, g, q, idx, sl: (b * steps + q, g)),
            kv_spec(2), kv_spec(3), kv_spec(4), kv_spec(5),
            pl.BlockSpec((tq, qw), lambda b, g, q, idx, sl: (b * steps + q, g)),
            pl.BlockSpec((tq, SMALL_W), lambda b, g, q, idx, sl: (b * steps + q, 0)),
            pl.BlockSpec((N_BRANCH, Dh), lambda b, g, q, idx, sl: (0, 0)),
        ],
        out_specs=pl.BlockSpec((tq, qw), lambda b, g, q, idx, sl: (b * steps + q, g)),
        scratch_shapes=[
            pltpu.VMEM((seq, Dh), BF16),
            pltpu.VMEM((WINDOW + seq, Dh), BF16),
            pltpu.VMEM((WINDOW + seq, Dh), BF16),
            pltpu.VMEM((QB, span), F32),
            pltpu.VMEM((4, NSA_GROUP * QB, span), F32),
            pltpu.VMEM((4, NSA_GROUP * QB, span), BF16),
        ],
    )
    return pl.pallas_call(
        functools.partial(_nsa_attn_kernel, seq=seq, qps=qps),
        grid_spec=grid_spec,
        out_shape=jax.ShapeDtypeStruct((bsz * seq, NSA_HEADS * Dh), BF16),
        compiler_params=_params(("arbitrary", "arbitrary", "arbitrary")),
        name="nsa_attn",
    )(sel_flat, slopes, qn, kv_t, kv_t, kv_t, kv_t, ocmp, small, kng)


def _merge_kernel(yssm_ref, bcx_ref, bcxp_ref, ynsa_ref, scw_ref, w0_ref, w1_ref, w2_ref, g_ref, out_ref, ysc_scr,
                  *, tm, tiles_per_seq, tn):
    keep_prev = (pl.program_id(0) % tiles_per_seq > 0).astype(F32)
    chunk = 128
    tail = bcxp_ref.shape[0]
    B = slice(0, D_MODEL)
    C = slice(D_MODEL, 2 * D_MODEL)
    X = slice(2 * D_MODEL, 3 * D_MODEL)

    def body(i, carry):
        r0 = pl.multiple_of(i * chunk, chunk)
        u = bcx_ref[pl.ds(r0, chunk), C].astype(F32) * bcx_ref[pl.ds(r0, chunk), X].astype(F32)
        p0 = pl.multiple_of(jnp.maximum(r0 - tail, 0), tail)
        prev_in = bcx_ref[pl.ds(p0, tail), C].astype(F32) * bcx_ref[pl.ds(p0, tail), X].astype(F32)
        prev_out = bcxp_ref[:, C].astype(F32) * bcxp_ref[:, X].astype(F32) * keep_prev
        prev = jnp.where(i > 0, prev_in, prev_out)
        acc = u * scw_ref[SC_CONV - 1:SC_CONV, :]
        for s in range(1, SC_CONV):
            k = SC_CONV - 1 - s
            acc = acc + _shift_rows(u, prev, s) * scw_ref[k:k + 1, :]
        ysc_scr[pl.ds(r0, chunk), :] = (bcx_ref[pl.ds(r0, chunk), B].astype(F32) * acc).astype(BF16)
        return carry

    lax.fori_loop(0, tm // chunk, body, 0)

    for n0 in range(0, D_MODEL, tn):
        cols = slice(n0, n0 + tn)
        a0 = jnp.dot(yssm_ref[...], w0_ref[:, cols], preferred_element_type=F32)
        a1 = jnp.dot(ysc_scr[...], w1_ref[:, cols], preferred_element_type=F32)
        a2 = jnp.dot(ynsa_ref[...], w2_ref[:, cols], preferred_element_type=F32)
        gate = [_sigmoid(g_ref[:, k * D_MODEL + n0:k * D_MODEL + n0 + tn].astype(F32)) for k in range(N_BRANCH)]
        out_ref[:, cols] = (gate[0] * a0 + gate[1] * a1 + gate[2] * a2).astype(BF16)


def _merge(yssm, proj, ynsa, sc_w, w_ssm, w_sc, w_nsa, seq, tm=256, tn=512):
    m = yssm.shape[0]
    tm = min(tm, seq)
    tail = 16
    per = tm // tail
    wide = N_BRANCH * D_MODEL
    rows = pl.BlockSpec((tm, D_MODEL), lambda i: (i, 0))
    wspec = pl.BlockSpec((D_MODEL, D_MODEL), lambda i: (0, 0), pipeline_mode=pl.Buffered(1))
    return pl.pallas_call(
        functools.partial(_merge_kernel, tm=tm, tiles_per_seq=seq // tm, tn=tn),
        grid=(m // tm,),
        in_specs=[
            rows,
            pl.BlockSpec((tm, wide), lambda i: (i, SC_OFF // wide)),
            pl.BlockSpec((tail, wide), lambda i: (jnp.maximum(i * per - 1, 0), SC_OFF // wide)),
            rows,
            pl.BlockSpec((SC_CONV, D_MODEL), lambda i: (0, 0)),
            wspec, wspec, wspec,
            pl.BlockSpec((tm, wide), lambda i: (i, MG_OFF // wide)),
        ],
        out_specs=rows,
        out_shape=jax.ShapeDtypeStruct((m, D_MODEL), BF16),
        scratch_shapes=[pltpu.VMEM((tm, D_MODEL), BF16)],
        compiler_params=_params(("arbitrary",)),
        name="merge",
    )(yssm, proj, proj, ynsa, sc_w, w_ssm, w_sc, w_nsa, proj)


def _outproj_kernel(a_ref, w_ref, x_ref, out_ref, *, tn):
    for n0 in range(0, D_MODEL, tn):
        cols = slice(n0, n0 + tn)
        out_ref[:, cols] = x_ref[:, cols] + jnp.dot(a_ref[...], w_ref[:, cols], preferred_element_type=F32)


def _outproj(a, w, x2, tm=512, tn=512):
    m = a.shape[0]
    tm = min(tm, m)
    rows = pl.BlockSpec((tm, D_MODEL), lambda i: (i, 0))
    return pl.pallas_call(
        functools.partial(_outproj_kernel, tn=tn),
        grid=(m // tm,),
        in_specs=[
            rows,
            pl.BlockSpec((D_MODEL, D_MODEL), lambda i: (0, 0), pipeline_mode=pl.Buffered(1)),
            rows,
        ],
        out_specs=rows,
        out_shape=jax.ShapeDtypeStruct((m, D_MODEL), F32),
        compiler_params=_params(("arbitrary",)),
        name="outproj",
    )(a, w, x2)


def _ffn_kernel(x_ref, g_ref, wg_ref, wu_ref, wd_ref, out_ref, h_scr, *, tm):
    @pl.when(pl.program_id(1) == 0)
    def _():
        _norm_to_scratch(x_ref, g_ref, h_scr, tm)
        out_ref[...] = x_ref[...]

    _swiglu_accumulate(h_scr, wg_ref[...], wu_ref[...], wd_ref[...], out_ref, tm)


def _swiglu_accumulate(h_ref, wg, wu, wd, acc_ref, rows, sub=SWIGLU_SUB_ROWS):
    for r0 in range(0, rows, sub):
        h = h_ref[r0:r0 + sub, :]
        a = jnp.dot(h, wg, preferred_element_type=F32)
        u = jnp.dot(h, wu, preferred_element_type=F32)
        acc_ref[r0:r0 + sub, :] += jnp.dot((_silu(a) * u).astype(BF16), wd, preferred_element_type=F32)


def _ffn(x2, g, wg, wu, wd, tm=1024, tf=512):
    m = x2.shape[0]
    tm = min(tm, m)
    return pl.pallas_call(
        functools.partial(_ffn_kernel, tm=tm),
        grid=(m // tm, D_FF // tf),
        in_specs=[
            pl.BlockSpec((tm, D_MODEL), lambda i, f: (i, 0)),
            pl.BlockSpec((1, D_MODEL), lambda i, f: (0, 0)),
            pl.BlockSpec((D_MODEL, tf), lambda i, f: (0, f)),
            pl.BlockSpec((D_MODEL, tf), lambda i, f: (0, f)),
            pl.BlockSpec((tf, D_MODEL), lambda i, f: (f, 0)),
        ],
        out_specs=pl.BlockSpec((tm, D_MODEL), lambda i, f: (i, 0)),
        out_shape=jax.ShapeDtypeStruct((m, D_MODEL), F32),
        scratch_shapes=[pltpu.VMEM((tm, D_MODEL), BF16)],
        compiler_params=_params(("arbitrary", "arbitrary")),
        name="ffn",
    )(x2, g, wg, wu, wd)


SLABS = D_MODEL // LANES


def _route_kernel(x_ref, g_ref, wr_ref, h_ref, idx_ref, wgt_ref):
    h = _rms_rows(x_ref[...], g_ref[...])
    h_ref[...] = h.astype(BF16)
    logits = jnp.dot(h, wr_ref[...], precision=lax.Precision.HIGHEST, preferred_element_type=F32)
    lane = lax.broadcasted_iota(I32, logits.shape, 1)
    logits = jnp.where(lane < N_EXPERTS, logits, -jnp.inf)
    v0 = jnp.max(logits, axis=-1, keepdims=True)
    i0 = jnp.min(jnp.where(logits == v0, lane, LANES), axis=-1, keepdims=True)
    rest = jnp.where(lane == i0, -jnp.inf, logits)
    v1 = jnp.max(rest, axis=-1, keepdims=True)
    i1 = jnp.min(jnp.where(rest == v1, lane, LANES), axis=-1, keepdims=True)
    e1 = jnp.exp(v1 - v0)
    w0 = 1.0 / (1.0 + e1)
    idx_ref[...] = jnp.where(lane == 0, i0, jnp.where(lane == 1, i1, 0))
    wgt_ref[...] = jnp.where(lane == 0, w0, jnp.where(lane == 1, e1 * w0, 0.0))


def _route(x2, g, w_router, tm=256):
    m = x2.shape[0]
    tm = min(tm, m)
    return pl.pallas_call(
        _route_kernel,
        grid=(m // tm,),
        in_specs=[
            pl.BlockSpec((tm, D_MODEL), lambda i: (i, 0)),
            pl.BlockSpec((1, D_MODEL), lambda i: (0, 0)),
            pl.BlockSpec((D_MODEL, LANES), lambda i: (0, 0)),
        ],
        out_specs=[
            pl.BlockSpec((tm, D_MODEL), lambda i: (i, 0)),
            pl.BlockSpec((tm, LANES), lambda i: (i, 0)),
            pl.BlockSpec((tm, LANES), lambda i: (i, 0)),
        ],
        out_shape=[
            jax.ShapeDtypeStruct((m, D_MODEL), BF16),
            jax.ShapeDtypeStruct((m, LANES), I32),
            jax.ShapeDtypeStruct((m, LANES), F32),
        ],
        compiler_params=_params(("arbitrary",)),
        name="route",
    )(x2, g, w_router)


def _gather_kernel(src_ref, in_ref, out_ref, sem, *, rows):
    base = pl.program_id(0) * rows

    def start(r, carry):
        pltpu.make_async_copy(in_ref.at[src_ref[base + r]], out_ref.at[r], sem).start()
        return carry

    lax.fori_loop(0, rows, start, 0, unroll=8)
    pltpu.make_async_copy(in_ref.at[pl.ds(0, rows)], out_ref, sem).wait()


def _gather_rows(src, table, n_out, rows=GATHER_ROWS):
    grid_spec = pltpu.PrefetchScalarGridSpec(
        num_scalar_prefetch=1,
        grid=(n_out // rows,),
        in_specs=[pl.BlockSpec(memory_space=pl.ANY)],
        out_specs=pl.BlockSpec((rows, SLABS, LANES), lambda i, src: (i, 0, 0)),
        scratch_shapes=[pltpu.SemaphoreType.DMA(())],
    )
    out = pl.pallas_call(
        functools.partial(_gather_kernel, rows=rows),
        grid_spec=grid_spec,
        out_shape=jax.ShapeDtypeStruct((n_out, SLABS, LANES), table.dtype),
        compiler_params=pltpu.CompilerParams(dimension_semantics=("arbitrary",)),
        name="gather_rows",
    )(src, table.reshape(table.shape[0], SLABS, LANES))
    return out.reshape(n_out, D_MODEL)


def _gffn_kernel(te_ref, tv_ref, hs_ref, wg_ref, wu_ref, wd_ref, out_ref, acc_scr, wg_scr, wu_scr, wd_scr):
    i = pl.program_id(0)
    f = pl.program_id(1)
    nf = pl.num_programs(1)

    @pl.when(f == 0)
    def _():
        acc_scr[...] = jnp.zeros_like(acc_scr)

    @pl.when(tv_ref[i] > 0)
    def _():
        wg_scr[...] = wg_ref[0].astype(BF16)
        wu_scr[...] = wu_ref[0].astype(BF16)
        wd_scr[...] = wd_ref[0].astype(BF16)
        rows = hs_ref.shape[0]
        _swiglu_accumulate(hs_ref, wg_scr[...], wu_scr[...], wd_scr[...], acc_scr, rows, sub=rows)

    @pl.when(f == nf - 1)
    def _():
        out_ref[...] = acc_scr[...].astype(BF16)


def _gffn(tile_expert, tile_valid, hs, wg, wu, wd, tm=MOE_TILE, tf=512):
    p = hs.shape[0]
    nf = D_FF // tf

    def fblk(f, tv, i):
        return jnp.where(tv[i] > 0, f, nf - 1)

    grid_spec = pltpu.PrefetchScalarGridSpec(
        num_scalar_prefetch=2,
        grid=(p // tm, nf),
        in_specs=[
            pl.BlockSpec((tm, D_MODEL), lambda i, f, te, tv: (i, 0)),
            pl.BlockSpec((1, D_MODEL, tf), lambda i, f, te, tv: (te[i], 0, fblk(f, tv, i))),
            pl.BlockSpec((1, D_MODEL, tf), lambda i, f, te, tv: (te[i], 0, fblk(f, tv, i))),
            pl.BlockSpec((1, tf, D_MODEL), lambda i, f, te, tv: (te[i], fblk(f, tv, i), 0)),
        ],
        out_specs=pl.BlockSpec((tm, D_MODEL), lambda i, f, te, tv: (i, 0)),
        scratch_shapes=[
            pltpu.VMEM((tm, D_MODEL), F32),
            pltpu.VMEM((D_MODEL, tf), BF16),
            pltpu.VMEM((D_MODEL, tf), BF16),
            pltpu.VMEM((tf, D_MODEL), BF16),
        ],
    )
    return pl.pallas_call(
        _gffn_kernel,
        grid_spec=grid_spec,
        out_shape=jax.ShapeDtypeStruct((p, D_MODEL), BF16),
        compiler_params=_params(("arbitrary", "arbitrary")),
        name="grouped_ffn",
    )(tile_expert, tile_valid, hs, wg, wu, wd)


def _combine_kernel(x_ref, y_ref, w_ref, out_ref):
    w = w_ref[...]
    w0 = w[:, 0:1]
    w1 = w[:, 1:2]
    out_ref[...] = (x_ref[...] + w0 * y_ref[:, :D_MODEL].astype(F32) + w1 * y_ref[:, D_MODEL:].astype(F32))


def _combine(x2, y_pairs, top_w, tm=512):
    m = x2.shape[0]
    tm = min(tm, m)
    return pl.pallas_call(
        _combine_kernel,
        grid=(m // tm,),
        in_specs=[
            pl.BlockSpec((tm, D_MODEL), lambda i: (i, 0)),
            pl.BlockSpec((tm, TOP_K * D_MODEL), lambda i: (i, 0)),
            pl.BlockSpec((tm, LANES), lambda i: (i, 0)),
        ],
        out_specs=pl.BlockSpec((tm, D_MODEL), lambda i: (i, 0)),
        out_shape=jax.ShapeDtypeStruct((m, D_MODEL), F32),
        compiler_params=_params(("arbitrary",)),
        name="combine",
    )(x2, y_pairs, top_w)


def _moe(x2, g, w_router, wg, wu, wd):
    m = x2.shape[0]
    tm = MOE_TILE
    h, top_idx, top_w = _route(x2, g, w_router)
    e_flat = top_idx[:, :TOP_K].reshape(-1)
    onehot = (e_flat[:, None] == jnp.arange(N_EXPERTS, dtype=I32)[None, :]).astype(I32)
    ranks = jnp.cumsum(onehot, axis=0) - onehot
    rank = jnp.sum(ranks * onehot, axis=1)
    counts = jnp.sum(onehot, axis=0)
    padded = ((counts + tm - 1) // tm) * tm
    seg_end = jnp.cumsum(padded)
    seg_start = seg_end - padded
    pos = seg_start[e_flat] + rank
    step = math.lcm(tm, GATHER_ROWS)
    p_max = -(-(TOP_K * m + N_EXPERTS * tm) // step) * step
    n_tiles = p_max // tm
    src = (jnp.arange(p_max, dtype=I32) % m).at[pos].set(jnp.arange(TOP_K * m, dtype=I32) // TOP_K)
    tile_start = jnp.arange(n_tiles, dtype=I32) * tm
    tile_valid = (tile_start < seg_end[-1]).astype(I32)
    last_expert = jnp.max(jnp.where(counts > 0, jnp.arange(N_EXPERTS, dtype=I32), 0))
    tile_expert = jnp.sum((tile_start[:, None] >= seg_end[None, :]).astype(I32), axis=1)
    tile_expert = jnp.where(tile_valid > 0, tile_expert, last_expert)

    hs = _gather_rows(src, h, p_max)
    ys = _gffn(tile_expert, tile_valid, hs, wg, wu, wd)
    y_pairs = _gather_rows(pos, ys, TOP_K * m).reshape(m, TOP_K * D_MODEL)
    return _combine(x2, y_pairs, top_w)


def _pad_row(v, width):
    return jnp.pad(v.astype(F32), (0, width - v.shape[0])).reshape(1, width)


def _alibi_tables():
    h = jnp.arange(1, NSA_HEADS + 1, dtype=F32)
    slopes = (jnp.exp2(-8.0 * h / NSA_HEADS) * LOG2E).reshape(NSA_KV_HEADS, NSA_GROUP)
    lanes_tbl = jnp.broadcast_to(jnp.pad(slopes, ((0, 0), (0, 8 - NSA_GROUP)))[:, :, None], (NSA_KV_HEADS, 8, LANES))
    return lanes_tbl, slopes.reshape(-1)


def _mixer_layer(x2, bsz, seq, norm_mix, w_in, ssm_conv_w, ssm_conv_b, ssm_dt_bias, ssm_a_log, ssm_d, ssm_norm,
                 w_ssm_out, sc_conv_w, w_sc_out, q_norm, k_norm, cmp_pos, w_cmp_k, w_cmp_v, w_nsa_out, w_out):
    w_big = jnp.concatenate(
        [w_in[:, _SRC_XBC:_SRC_DT], w_in[:, _SRC_Z:_SRC_XBC], w_in[:, _SRC_SC:_SRC_Q], w_in[:, _SRC_MG:],
         w_in[:, _SRC_Q:_SRC_NG]], axis=1).astype(BF16)
    w_small = jnp.concatenate(
        [w_in[:, _SRC_DT:_SRC_SC], w_in[:, _SRC_NG:_SRC_MG],
         jnp.zeros((D_MODEL, SMALL_W - SSM_HEADS - N_BRANCH * NSA_HEADS), F32)], axis=1).astype(BF16)
    proj, small = _inproj(x2, norm_mix.reshape(1, D_MODEL), w_big, w_small)

    y_ssm = _ssd(proj, small, ssm_conv_w, ssm_conv_b.reshape(1, SSM_XBC), _pad_row(ssm_dt_bias, SMALL_W),
                 _pad_row(ssm_a_log, SMALL_W), jnp.repeat(ssm_d, SSM_HEAD_DIM).reshape(1, D_MODEL),
                 ssm_norm.reshape(1, D_MODEL), bsz, seq)

    G, Dh = NSA_KV_HEADS, NSA_HEAD_DIM
    kv_t = proj[:, KV_OFF:].reshape(bsz, seq, 6 * G, Dh).transpose(0, 2, 1, 3)
    lanes_tbl, slopes_flat = _alibi_tables()
    half_k = CMP_STRIDE * Dh
    o_cmp, qn, sel = _nsa_cmp(kv_t, proj, w_cmp_k.reshape(2, half_k, Dh).astype(BF16),
                              w_cmp_v.reshape(2, half_k, Dh).astype(BF16), cmp_pos.reshape(2, 2, half_k),
                              k_norm[0:1], q_norm.reshape(1, Dh), lanes_tbl, bsz, seq)
    sel_flat = sel[..., :SEL_TOPK].reshape(-1)
    y_nsa = _nsa_attn(sel_flat, slopes_flat, qn, kv_t, o_cmp, small, k_norm, bsz, seq)

    mixed = _merge(y_ssm, proj, y_nsa, sc_conv_w, w_ssm_out.astype(BF16), w_sc_out.astype(BF16),
                   w_nsa_out.astype(BF16), seq)
    return _outproj(mixed, w_out.astype(BF16), x2)


def kernel(x, norm_mix, w_in, ssm_conv_w, ssm_conv_b, ssm_dt_bias, ssm_a_log, ssm_d, ssm_norm, w_ssm_out,
           sc_conv_w, w_sc_out, q_norm, k_norm, cmp_pos, w_cmp_k, w_cmp_v, w_nsa_out, w_out, norm_ffn,
           ffn_w_gate, ffn_w_up, ffn_w_down, moe_router, moe_w_gate, moe_w_up, moe_w_down):
    bsz, seq, _ = x.shape
    depth = norm_mix.shape[0]
    x2 = x.reshape(bsz * seq, D_MODEL)
    for layer in range(depth):
        x2 = _mixer_layer(x2, bsz, seq, norm_mix[layer], w_in[layer], ssm_conv_w[layer], ssm_conv_b[layer],
                          ssm_dt_bias[layer], ssm_a_log[layer], ssm_d[layer], ssm_norm[layer], w_ssm_out[layer],
                          sc_conv_w[layer], w_sc_out[layer], q_norm[layer], k_norm[layer], cmp_pos[layer],
                          w_cmp_k[layer], w_cmp_v[layer], w_nsa_out[layer], w_out[layer])
        g = norm_ffn[layer].reshape(1, D_MODEL)
        i = layer // 2
        if layer % 2 == 0:
            x2 = _ffn(x2, g, ffn_w_gate[i].astype(BF16), ffn_w_up[i].astype(BF16), ffn_w_down[i].astype(BF16))
        else:
            w_router = jnp.pad(moe_router[i], ((0, 0), (0, LANES - N_EXPERTS)))
            x2 = _moe(x2, g, w_router, moe_w_gate[i], moe_w_up[i], moe_w_down[i])
    return x2.reshape(bsz, seq, D_MODEL)
```

```python
import functools
import math

import jax
import jax.numpy as jnp
from jax import lax
from jax.experimental import pallas as pl
from jax.experimental.pallas import tpu as pltpu

F32 = jnp.float32
BF16 = jnp.bfloat16
I32 = jnp.int32

D_MODEL = 2048
EPS = 1e-6
NEG_INF = -1e30
MASKED_DIST = 1e30
LOG2E = 1.4426950408889634
FORCED_SCORE = 1e4
SSM_HEAD_DIM = 64
SSM_HEADS = D_MODEL // SSM_HEAD_DIM
SSM_GROUPS = 8
SSM_HEADS_PER_GROUP = SSM_HEADS // SSM_GROUPS
SSM_STATE = 128
SSM_CONV = 4
SSM_CHUNK = 128
SSM_XBC = D_MODEL + 2 * SSM_GROUPS * SSM_STATE
SC_CONV = 3
NSA_HEADS = 16
NSA_HEAD_DIM = 128
NSA_KV_HEADS = 4
NSA_GROUP = NSA_HEADS // NSA_KV_HEADS
CMP_STRIDE = 16
CMP_LEN = 2 * CMP_STRIDE
SEL_BLOCK = 64
SEL_TOPK = 8
WINDOW = 512
N_BRANCH = 3
D_FF = 5632
N_EXPERTS = 8
TOP_K = 2

XBC_OFF = 0
Z_OFF = XBC_OFF + SSM_XBC
SC_OFF = Z_OFF + D_MODEL
MG_OFF = SC_OFF + 3 * D_MODEL
Q_OFF = MG_OFF + N_BRANCH * D_MODEL
KV_OFF = Q_OFF + NSA_HEADS * NSA_HEAD_DIM
PROJ_W = KV_OFF + 6 * NSA_KV_HEADS * NSA_HEAD_DIM
LANES = 128
SMALL_W = LANES
GATE_LANE0 = SSM_HEADS
_SRC_Z = 0
_SRC_XBC = D_MODEL
_SRC_DT = _SRC_XBC + SSM_XBC
_SRC_SC = _SRC_DT + SSM_HEADS
_SRC_Q = _SRC_SC + 3 * D_MODEL
_SRC_KV = _SRC_Q + NSA_HEADS * NSA_HEAD_DIM
_SRC_NG = _SRC_KV + 6 * NSA_KV_HEADS * NSA_HEAD_DIM
_SRC_MG = _SRC_NG + N_BRANCH * NSA_HEADS

VMEM_LIMIT = 56 * 1024 * 1024
MOE_TILE = 704
GATHER_ROWS = 512
NSA_Q_BLOCKS_PER_STEP = 8
NSA_ATTN_SLOTS = 8
SWIGLU_SUB_ROWS = 256


def _params(sem, vmem=VMEM_LIMIT):
    return pltpu.CompilerParams(dimension_semantics=sem, vmem_limit_bytes=vmem)


def _sigmoid(x):
    return 1.0 / (1.0 + jnp.exp2(x * (-LOG2E)))


def _silu(x):
    return x * _sigmoid(x)


def _rms_rows(x, g):
    return x * lax.rsqrt(jnp.mean(x * x, axis=-1, keepdims=True) + EPS) * g


def _norm_to_scratch(x_ref, g_ref, h_scr, rows, chunk=256):
    g = g_ref[...]

    def body(i, carry):
        r0 = pl.multiple_of(i * chunk, chunk)
        h_scr[pl.ds(r0, chunk), :] = _rms_rows(x_ref[pl.ds(r0, chunk), :], g).astype(BF16)
        return carry

    lax.fori_loop(0, rows // chunk, body, 0)


def _inproj_kernel(x_ref, g_ref, w_ref, ws_ref, proj_ref, small_ref, h_scr, *, tm):
    @pl.when(pl.program_id(1) == 0)
    def _():
        _norm_to_scratch(x_ref, g_ref, h_scr, tm)
        small_ref[...] = jnp.dot(h_scr[...], ws_ref[...], preferred_element_type=F32)

    proj_ref[...] = jnp.dot(h_scr[...], w_ref[...], preferred_element_type=F32).astype(BF16)


def _inproj(x2, g, w_big, w_small, tm=1024, tn=1024):
    m = x2.shape[0]
    tm = min(tm, m)
    return pl.pallas_call(
        functools.partial(_inproj_kernel, tm=tm),
        grid=(m // tm, PROJ_W // tn),
        in_specs=[
            pl.BlockSpec((tm, D_MODEL), lambda i, j: (i, 0)),
            pl.BlockSpec((1, D_MODEL), lambda i, j: (0, 0)),
            pl.BlockSpec((D_MODEL, tn), lambda i, j: (0, j)),
            pl.BlockSpec((D_MODEL, SMALL_W), lambda i, j: (0, 0)),
        ],
        out_specs=[
            pl.BlockSpec((tm, tn), lambda i, j: (i, j)),
            pl.BlockSpec((tm, SMALL_W), lambda i, j: (i, 0)),
        ],
        out_shape=[
            jax.ShapeDtypeStruct((m, PROJ_W), BF16),
            jax.ShapeDtypeStruct((m, SMALL_W), F32),
        ],
        scratch_shapes=[pltpu.VMEM((tm, D_MODEL), BF16)],
        compiler_params=_params(("arbitrary", "arbitrary")),
        name="inproj",
    )(x2, g, w_big, w_small)


def _shift_rows(cur, prev_tail, s):
    rc = pltpu.roll(cur, s, 0)
    rp = pltpu.roll(prev_tail, s, 0)
    row = lax.broadcasted_iota(I32, (8, cur.shape[1]), 0)
    top = jnp.where(row < s, rp[0:8], rc[0:8])
    return jnp.concatenate([top, rc[8:]], axis=0)


def _ssd_kernel(xbc_ref, prev_ref, z_ref, small_ref, cw_ref, cb_ref, dtb_ref, alog_ref, dskip_ref,
                ng_ref, out_ref, state_scr, y_scr):
    L = SSM_CHUNK
    P = SSM_HEAD_DIM
    R = SSM_HEADS_PER_GROUP
    GW = R * P
    c = pl.program_id(1)

    @pl.when(c == 0)
    def _():
        state_scr[...] = jnp.zeros_like(state_scr)

    keep_prev = (c > 0).astype(F32)

    def conv_silu(lo, w):
        cur = xbc_ref[:, lo:lo + w].astype(F32)
        prev = prev_ref[:, lo:lo + w].astype(F32) * keep_prev
        acc = cur * cw_ref[SSM_CONV - 1:SSM_CONV, lo:lo + w] + cb_ref[:, lo:lo + w]
        for s in range(1, SSM_CONV):
            k = SSM_CONV - 1 - s
            acc = acc + _shift_rows(cur, prev, s) * cw_ref[k:k + 1, lo:lo + w]
        return _silu(acc)

    row = lax.broadcasted_iota(I32, (L, L), 0)
    col = lax.broadcasted_iota(I32, (L, L), 1)
    causal = row >= col
    tril = causal.astype(F32)

    pre = small_ref[...] + dtb_ref[...]
    dt = jnp.maximum(pre, 0.0) + jnp.log(1.0 + jnp.exp(-jnp.abs(pre)))
    dt = jnp.where(col < SSM_HEADS, dt, 0.0)
    a_neg = jnp.exp(alog_ref[...]) * (-LOG2E)
    a_cum = jnp.dot(tril, dt * a_neg, precision=lax.Precision.HIGHEST, preferred_element_type=F32)
    a_cum_t = a_cum.T
    a_last = a_cum[L - 1:L, :]
    dec_end = jnp.exp2(a_last - a_cum)
    dec_in = jnp.exp2(a_cum)
    chunk_dec = jnp.exp2(a_last)

    lane_g = lax.broadcasted_iota(I32, (L, GW), 1)
    lane_g1 = lax.broadcasted_iota(I32, (1, GW), 1)
    head_mask = [jnp.where((lane_g >= r * P) & (lane_g < (r + 1) * P), 1.0, 0.0).astype(BF16) for r in range(R)]

    def expand(mat, g, lanes):
        out = mat[:, R * g + R - 1:R * g + R]
        for r in range(R - 2, -1, -1):
            out = jnp.where(lanes < (r + 1) * P, mat[:, R * g + r:R * g + r + 1], out)
        return out

    for g in range(SSM_GROUPS):
        xs = conv_silu(g * GW, GW)
        b_in = conv_silu(D_MODEL + g * SSM_STATE, SSM_STATE)
        c_out = conv_silu(D_MODEL + (SSM_GROUPS + g) * SSM_STATE, SSM_STATE)
        b_bf = b_in.astype(BF16)
        c_bf = c_out.astype(BF16)
        cb = lax.dot_general(c_bf, b_bf, (((1,), (1,)), ((), ())), preferred_element_type=F32)
        xdt = xs * expand(dt, g, lane_g)
        xdt_bf = xdt.astype(BF16)
        y = xs * dskip_ref[:, g * GW:(g + 1) * GW]
        for r in range(R):
            h = R * g + r
            seg = a_cum[:, h:h + 1] - a_cum_t[h:h + 1, :]
            lmat = jnp.where(causal, jnp.exp2(seg), 0.0)
            y = y + jnp.dot((cb * lmat).astype(BF16), xdt_bf * head_mask[r], preferred_element_type=F32)
        h_prev = state_scr[g]
        y = y + jnp.dot(c_bf, h_prev.astype(BF16), preferred_element_type=F32) * expand(dec_in, g, lane_g)
        xdt_end = (xdt * expand(dec_end, g, lane_g)).astype(BF16)
        new_state = jnp.dot(b_in.T.astype(BF16), xdt_end, preferred_element_type=F32)
        state_scr[g] = h_prev * expand(chunk_dec, g, lane_g1) + new_state
        y_scr[:, g * GW:(g + 1) * GW] = y

    yz = y_scr[...] * _silu(z_ref[...].astype(F32))
    out_ref[...] = _rms_rows(yz, ng_ref[...]).astype(BF16)


def _ssd(proj, small, conv_w, conv_b, dt_bias, a_log, d_skip, norm_g, bsz, seq):
    L = SSM_CHUNK
    nc = seq // L
    tail = 16
    per = L // tail
    return pl.pallas_call(
        _ssd_kernel,
        grid=(bsz, nc),
        in_specs=[
            pl.BlockSpec((L, SSM_XBC), lambda b, c: (b * nc + c, XBC_OFF // SSM_XBC)),
            pl.BlockSpec((tail, SSM_XBC), lambda b, c: (jnp.maximum((b * nc + c) * per - 1, 0), XBC_OFF // SSM_XBC)),
            pl.BlockSpec((L, D_MODEL), lambda b, c: (b * nc + c, Z_OFF // D_MODEL)),
            pl.BlockSpec((L, SMALL_W), lambda b, c: (b * nc + c, 0)),
            pl.BlockSpec((SSM_CONV, SSM_XBC), lambda b, c: (0, 0)),
            pl.BlockSpec((1, SSM_XBC), lambda b, c: (0, 0)),
            pl.BlockSpec((1, SMALL_W), lambda b, c: (0, 0)),
            pl.BlockSpec((1, SMALL_W), lambda b, c: (0, 0)),
            pl.BlockSpec((1, D_MODEL), lambda b, c: (0, 0)),
            pl.BlockSpec((1, D_MODEL), lambda b, c: (0, 0)),
        ],
        out_specs=pl.BlockSpec((L, D_MODEL), lambda b, c: (b * nc + c, 0)),
        out_shape=jax.ShapeDtypeStruct((bsz * seq, D_MODEL), BF16),
        scratch_shapes=[
            pltpu.VMEM((SSM_GROUPS, SSM_STATE, SSM_HEADS_PER_GROUP * SSM_HEAD_DIM), F32),
            pltpu.VMEM((L, D_MODEL), F32),
        ],
        compiler_params=_params(("arbitrary", "arbitrary")),
        name="ssd",
    )(proj, proj, proj, small, conv_w, conv_b, dt_bias, a_log, d_skip, norm_g)


def _nsa_cmp_kernel(r_ref, q_ref, wk_ref, wv_ref, pos_ref, kng_ref, qng_ref, slope_ref,
                    ocmp_ref, qn_ref, idx_ref, kc_scr, vc_scr, *, seq, tq):
    Dh = NSA_HEAD_DIM
    nb = seq // CMP_STRIDE
    nqb = seq // SEL_BLOCK

    def compress(j, w_ref):
        t = r_ref[0, 0, j].astype(F32)
        top = jnp.dot((t + pos_ref[j, 0:1, :]).astype(BF16), w_ref[0], preferred_element_type=F32)
        bot = jnp.dot((t + pos_ref[j, 1:2, :]).astype(BF16), w_ref[1], preferred_element_type=F32)
        return top + pltpu.roll(bot, nb - 1, 0)

    kc_scr[...] = _rms_rows(compress(0, wk_ref), kng_ref[...]).T.astype(BF16)
    vc_scr[...] = compress(1, wv_ref).astype(BF16)
    k_cmp_t = kc_scr[...]
    v_cmp = vc_scr[...]

    n_start = lax.broadcasted_iota(I32, (1, nb), 1) * CMP_STRIDE
    qg = qng_ref[...] * (Dh ** -0.5 * LOG2E)

    def scores(qn, qpos, slope):
        return probs(jnp.dot(qn, k_cmp_t, preferred_element_type=F32), qpos, slope)

    def probs(s, qpos, slope):
        dist = (qpos - n_start).astype(F32) - (CMP_LEN - 1) / 2
        valid = (n_start + (CMP_LEN - 1)) <= qpos
        s = jnp.where(valid, s - slope * dist, NEG_INF)
        p = jnp.exp2(s - jnp.max(s, axis=-1, keepdims=True))
        p = p / jnp.sum(p, axis=-1, keepdims=True)
        return jnp.where(valid, p, 0.0)

    sel_row = lax.broadcasted_iota(I32, (nqb, seq), 0) * SEL_BLOCK
    sel_col = lax.broadcasted_iota(I32, (nqb, seq), 1)
    pick_first = jnp.where(sel_row == sel_col, 1.0, 0.0).astype(BF16)
    first_pos = lax.broadcasted_iota(I32, (nqb, 1), 0) * SEL_BLOCK
    p_first = jnp.zeros((nqb, nb), F32)

    slopes = [slope_ref[0, r:r + 1, 0:1] for r in range(NSA_GROUP)]

    def body(i, carry):
        r0 = pl.multiple_of(i * tq, tq)
        qpos = r0 + lax.broadcasted_iota(I32, (tq, 1), 0)
        def raw_scores(r):
            qn = _rms_rows(q_ref[pl.ds(r0, tq), r * Dh:(r + 1) * Dh].astype(F32), qg).astype(BF16)
            qn_ref[pl.ds(r0, tq), r * Dh:(r + 1) * Dh] = qn
            return jnp.dot(qn, k_cmp_t, preferred_element_type=F32)

        staged = raw_scores(0)
        for r in range(NSA_GROUP):
            s = staged
            if r + 1 < NSA_GROUP:
                staged = raw_scores(r + 1)
            p = probs(s, qpos, slopes[r])
            ocmp_ref[pl.ds(r0, tq), r * Dh:(r + 1) * Dh] = jnp.dot(
                p.astype(BF16), v_cmp, preferred_element_type=F32).astype(BF16)
        return carry

    lax.fori_loop(0, seq // tq, body, 0)
    for r in range(NSA_GROUP):
        q_first = jnp.dot(pick_first, qn_ref[:, r * Dh:(r + 1) * Dh], preferred_element_type=F32).astype(BF16)
        p_first = p_first + scores(q_first, first_pos, slopes[r])

    n_lo = lax.broadcasted_iota(I32, (nb, nqb), 0) * CMP_STRIDE
    j_lo = lax.broadcasted_iota(I32, (nb, nqb), 1) * SEL_BLOCK
    overlap = jnp.maximum(jnp.minimum(n_lo + CMP_LEN, j_lo + SEL_BLOCK) - jnp.maximum(n_lo, j_lo), 0)
    overlap = overlap.astype(F32) / CMP_LEN
    imp = jnp.dot(p_first, overlap, precision=lax.Precision.HIGHEST, preferred_element_type=F32)
    qb_i = lax.broadcasted_iota(I32, (nqb, nqb), 0)
    blk_j = lax.broadcasted_iota(I32, (nqb, nqb), 1)
    forced = (blk_j == 0) | (blk_j == qb_i) | (blk_j == qb_i - 1)
    imp = jnp.where(forced, FORCED_SCORE, jnp.where(blk_j > qb_i, -FORCED_SCORE, imp))
    out_lane = lax.broadcasted_iota(I32, (nqb, LANES), 1)
    picked = jnp.zeros((nqb, LANES), I32)
    for k in range(SEL_TOPK):
        best = jnp.max(imp, axis=-1, keepdims=True)
        arg = jnp.min(jnp.where(imp == best, blk_j, nqb), axis=-1, keepdims=True)
        picked = jnp.where(out_lane == k, arg, picked)
        imp = jnp.where(blk_j == arg, -jnp.inf, imp)
    idx_ref[0, 0] = picked


def _nsa_cmp(r5, proj, wk, wv, pos, kng, qng, slopes, bsz, seq):
    G = NSA_KV_HEADS
    nb = seq // CMP_STRIDE
    nqb = seq // SEL_BLOCK
    qw = NSA_GROUP * NSA_HEAD_DIM
    return pl.pallas_call(
        functools.partial(_nsa_cmp_kernel, seq=seq, tq=min(512, seq)),
        grid=(bsz, G),
        in_specs=[
            pl.BlockSpec((1, 1, 2, nb, CMP_STRIDE * NSA_HEAD_DIM), lambda b, g: (b, g, 0, 0, 0)),
            pl.BlockSpec((seq, qw), lambda b, g: (b, Q_OFF // qw + g)),
            pl.BlockSpec((2, CMP_STRIDE * NSA_HEAD_DIM, NSA_HEAD_DIM), lambda b, g: (0, 0, 0)),
            pl.BlockSpec((2, CMP_STRIDE * NSA_HEAD_DIM, NSA_HEAD_DIM), lambda b, g: (0, 0, 0)),
            pl.BlockSpec((2, 2, CMP_STRIDE * NSA_HEAD_DIM), lambda b, g: (0, 0, 0)),
            pl.BlockSpec((1, NSA_HEAD_DIM), lambda b, g: (0, 0)),
            pl.BlockSpec((1, NSA_HEAD_DIM), lambda b, g: (0, 0)),
            pl.BlockSpec((1, 8, LANES), lambda b, g: (g, 0, 0)),
        ],
        out_specs=[
            pl.BlockSpec((seq, qw), lambda b, g: (b, g)),
            pl.BlockSpec((seq, qw), lambda b, g: (b, g)),
            pl.BlockSpec((1, 1, nqb, LANES), lambda b, g: (b, g, 0, 0)),
        ],
        out_shape=[
            jax.ShapeDtypeStruct((bsz * seq, NSA_HEADS * NSA_HEAD_DIM), BF16),
            jax.ShapeDtypeStruct((bsz * seq, NSA_HEADS * NSA_HEAD_DIM), BF16),
            jax.ShapeDtypeStruct((bsz, G, nqb, LANES), I32),
        ],
        scratch_shapes=[pltpu.VMEM((NSA_HEAD_DIM, nb), BF16), pltpu.VMEM((nb, NSA_HEAD_DIM), BF16)],
        compiler_params=_params(("arbitrary", "arbitrary")),
        name="nsa_cmp",
    )(r5, proj, wk, wv, pos, kng, qng, slopes)


def _nsa_attn_kernel(idx_ref, slope_ref, qn_ref, ks_ref, vs_ref, kw_ref, vw_ref, ocmp_ref, gate_ref, kng_ref,
                     out_ref, ksn_scr, kwn_scr, vwp_scr, wdist_scr, *sp_scr, seq, qps):
    s_scrs = sp_scr[:NSA_ATTN_SLOTS]
    p_scrs = sp_scr[NSA_ATTN_SLOTS:]
    Dh = NSA_HEAD_DIM
    R = NSA_GROUP
    QB = SEL_BLOCK
    nqb = seq // QB
    span = WINDOW + QB
    b = pl.program_id(0)
    g = pl.program_id(1)
    step = pl.program_id(2)
    q_in = lax.broadcasted_iota(I32, (QB, 1), 0)

    @pl.when(step == 0)
    def _():
        kwn_scr[0:WINDOW, :] = jnp.zeros((WINDOW, Dh), BF16)
        vwp_scr[0:WINDOW, :] = jnp.zeros((WINDOW, Dh), BF16)

        def body(i, carry):
            r0 = pl.multiple_of(i * 256, 256)
            ksn_scr[pl.ds(r0, 256), :] = _rms_rows(ks_ref[pl.ds(r0, 256), :].astype(F32), kng_ref[1:2, :]).astype(BF16)
            kwn_scr[pl.ds(WINDOW + r0, 256), :] = _rms_rows(kw_ref[pl.ds(r0, 256), :].astype(F32),
                                                           kng_ref[2:3, :]).astype(BF16)
            vwp_scr[pl.ds(WINDOW + r0, 256), :] = vw_ref[pl.ds(r0, 256), :]
            return carry

        lax.fori_loop(0, seq // 256, body, 0)
        dist_w = q_in + WINDOW - lax.broadcasted_iota(I32, (QB, span), 1)
        wdist_scr[...] = jnp.where((dist_w >= 0) & (dist_w < WINDOW), dist_w.astype(F32), MASKED_DIST)

    nk = SEL_TOPK * QB
    key_lane = lax.broadcasted_iota(I32, (1, nk), 1)
    win_lane = lax.broadcasted_iota(I32, (1, span), 1)
    lane = lax.broadcasted_iota(I32, (QB, LANES), 1)
    slopes = [slope_ref[g * R + r] for r in range(R)]

    def qk(slot, q4, k, width):
        s_scrs[slot][:, 0:width] = lax.dot_general(q4, k, (((1,), (1,)), ((), ())), preferred_element_type=F32)

    def softmax(slot, dist, width):
        s_scr = s_scrs[slot]
        p_scr = p_scrs[slot]
        inv = []
        for r in range(R):
            s = s_scr[r * QB:(r + 1) * QB, 0:width] - slopes[r] * dist
            p = jnp.exp2(s - jnp.max(s, axis=-1, keepdims=True))
            inv.append(1.0 / jnp.sum(p, axis=-1, keepdims=True))
            p_scr[r * QB:(r + 1) * QB, 0:width] = p.astype(BF16)
        return inv

    def score_stage(i):
        qb = step * qps + i
        q4 = jnp.concatenate([qn_ref[i * QB:(i + 1) * QB, r * Dh:(r + 1) * Dh] for r in range(R)], axis=0)
        base = ((b * NSA_KV_HEADS + g) * nqb + qb) * SEL_TOPK
        starts = [pl.multiple_of(idx_ref[base + k] * QB, QB) for k in range(SEL_TOPK)]
        slot = (2 * i) % NSA_ATTN_SLOTS
        qk(slot, q4, jnp.concatenate([ksn_scr[pl.ds(st, QB), :] for st in starts], axis=0), nk)
        wstart = pl.multiple_of(qb * QB, QB)
        qk(slot + 1, q4, kwn_scr[pl.ds(wstart, span), :], span)
        return qb, starts, slot, wstart

    def softmax_stage(ctx):
        qb, starts, slot, wstart = ctx
        kpos_s = key_lane & (QB - 1)
        for k in range(SEL_TOPK):
            kpos_s = kpos_s + jnp.where((key_lane >= k * QB) & (key_lane < (k + 1) * QB), starts[k], 0)
        dist_s = (q_in + qb * QB - kpos_s).astype(F32)
        dist_s = jnp.where(dist_s >= 0, dist_s, MASKED_DIST)
        dist_w = jnp.where(win_lane >= WINDOW - qb * QB, wdist_scr[...], MASKED_DIST)
        return softmax(slot, dist_s, nk), softmax(slot + 1, dist_w, span)

    def output_stage(i, ctx, inv):
        qb, starts, slot, wstart = ctx
        rows = slice(i * QB, (i + 1) * QB)
        v_sel = jnp.concatenate([vs_ref[pl.ds(st, QB), :] for st in starts], axis=0)
        o_sel = jnp.dot(p_scrs[slot][:, 0:nk], v_sel, preferred_element_type=F32)
        o_win = jnp.dot(p_scrs[slot + 1][:, 0:span], vwp_scr[pl.ds(wstart, span), :], preferred_element_type=F32)
        sig = _sigmoid(gate_ref[rows, :])

        def gate(br, r):
            return jnp.sum(jnp.where(lane == GATE_LANE0 + br * NSA_HEADS + g * R + r, sig, 0.0), axis=-1, keepdims=True)

        for r in range(R):
            hr = slice(r * QB, (r + 1) * QB)
            o_cmp = ocmp_ref[rows, r * Dh:(r + 1) * Dh].astype(F32)
            o = (gate(0, r) * o_cmp + (gate(1, r) * inv[0][r]) * o_sel[hr] + (gate(2, r) * inv[1][r]) * o_win[hr])
            out_ref[rows, r * Dh:(r + 1) * Dh] = o.astype(BF16)

    ctx = score_stage(0)
    for i in range(qps):
        nxt = score_stage(i + 1) if i + 1 < qps else None
        output_stage(i, ctx, softmax_stage(ctx))
        ctx = nxt


def _nsa_attn(sel_flat, slopes, qn, proj, ocmp, small, kng, bsz, seq):
    G = NSA_KV_HEADS
    Dh = NSA_HEAD_DIM
    QB = SEL_BLOCK
    nqb = seq // QB
    qw = NSA_GROUP * Dh
    kv0 = KV_OFF // Dh

    def kv_spec(j):
        return pl.BlockSpec((seq, Dh), lambda b, g, q, idx, sl: (b, kv0 + j * G + g))

    qps = NSA_Q_BLOCKS_PER_STEP
    steps = nqb // qps
    tq = qps * QB
    span = WINDOW + QB
    grid_spec = pltpu.PrefetchScalarGridSpec(
        num_scalar_prefetch=2,
        grid=(bsz, G, steps),
        in_specs=[
            pl.BlockSpec((tq, qw), lambda b, g, q, idx, sl: (b * steps + q, g)),
            kv_spec(2), kv_spec(3), kv_spec(4), kv_spec(5),
            pl.BlockSpec((tq, qw), lambda b, g, q, idx, sl: (b * steps + q, g)),
            pl.BlockSpec((tq, SMALL_W), lambda b, g, q, idx, sl: (b * steps + q, 0)),
            pl.BlockSpec((N_BRANCH, Dh), lambda b, g, q, idx, sl: (0, 0)),
        ],
        out_specs=pl.BlockSpec((tq, qw), lambda b, g, q, idx, sl: (b * steps + q, g)),
        scratch_shapes=[
            pltpu.VMEM((seq, Dh), BF16),
            pltpu.VMEM((WINDOW + seq, Dh), BF16),
            pltpu.VMEM((WINDOW + seq, Dh), BF16),
            pltpu.VMEM((QB, span), F32),
        ] + [pltpu.VMEM((NSA_GROUP * QB, span), F32)] * NSA_ATTN_SLOTS
          + [pltpu.VMEM((NSA_GROUP * QB, span), BF16)] * NSA_ATTN_SLOTS,
    )
    return pl.pallas_call(
        functools.partial(_nsa_attn_kernel, seq=seq, qps=qps),
        grid_spec=grid_spec,
        out_shape=jax.ShapeDtypeStruct((bsz * seq, NSA_HEADS * Dh), BF16),
        compiler_params=_params(("arbitrary", "arbitrary", "arbitrary")),
        name="nsa_attn",
    )(sel_flat, slopes, qn, proj, proj, proj, proj, ocmp, small, kng)


def _merge_kernel(yssm_ref, bcx_ref, bcxp_ref, ynsa_ref, scw_ref, w0_ref, w1_ref, w2_ref, g_ref, out_ref, ysc_scr,
                  *, tm, tiles_per_seq, tn):
    keep_prev = (pl.program_id(0) % tiles_per_seq > 0).astype(F32)
    chunk = 128
    tail = bcxp_ref.shape[0]
    B = slice(0, D_MODEL)
    C = slice(D_MODEL, 2 * D_MODEL)
    X = slice(2 * D_MODEL, 3 * D_MODEL)

    def body(i, carry):
        r0 = pl.multiple_of(i * chunk, chunk)
        u = bcx_ref[pl.ds(r0, chunk), C].astype(F32) * bcx_ref[pl.ds(r0, chunk), X].astype(F32)
        p0 = pl.multiple_of(jnp.maximum(r0 - tail, 0), tail)
        prev_in = bcx_ref[pl.ds(p0, tail), C].astype(F32) * bcx_ref[pl.ds(p0, tail), X].astype(F32)
        prev_out = bcxp_ref[:, C].astype(F32) * bcxp_ref[:, X].astype(F32) * keep_prev
        prev = jnp.where(i > 0, prev_in, prev_out)
        acc = u * scw_ref[SC_CONV - 1:SC_CONV, :]
        for s in range(1, SC_CONV):
            k = SC_CONV - 1 - s
            acc = acc + _shift_rows(u, prev, s) * scw_ref[k:k + 1, :]
        ysc_scr[pl.ds(r0, chunk), :] = (bcx_ref[pl.ds(r0, chunk), B].astype(F32) * acc).astype(BF16)
        return carry

    lax.fori_loop(0, tm // chunk, body, 0)

    def branch_dots(n0):
        cols = slice(n0, n0 + tn)
        return (jnp.dot(yssm_ref[...], w0_ref[:, cols], preferred_element_type=F32),
                jnp.dot(ysc_scr[...], w1_ref[:, cols], preferred_element_type=F32),
                jnp.dot(ynsa_ref[...], w2_ref[:, cols], preferred_element_type=F32))

    staged = branch_dots(0)
    for n0 in range(0, D_MODEL, tn):
        a = staged
        if n0 + tn < D_MODEL:
            staged = branch_dots(n0 + tn)
        gate = [_sigmoid(g_ref[:, k * D_MODEL + n0:k * D_MODEL + n0 + tn].astype(F32)) for k in range(N_BRANCH)]
        out_ref[:, n0:n0 + tn] = (gate[0] * a[0] + gate[1] * a[1] + gate[2] * a[2]).astype(BF16)


def _merge(yssm, proj, ynsa, sc_w, w_ssm, w_sc, w_nsa, seq, tm=256, tn=512):
    m = yssm.shape[0]
    tm = min(tm, seq)
    tail = 16
    per = tm // tail
    wide = N_BRANCH * D_MODEL
    rows = pl.BlockSpec((tm, D_MODEL), lambda i: (i, 0))
    wspec = pl.BlockSpec((D_MODEL, D_MODEL), lambda i: (0, 0), pipeline_mode=pl.Buffered(1))
    return pl.pallas_call(
        functools.partial(_merge_kernel, tm=tm, tiles_per_seq=seq // tm, tn=tn),
        grid=(m // tm,),
        in_specs=[
            rows,
            pl.BlockSpec((tm, wide), lambda i: (i, SC_OFF // wide)),
            pl.BlockSpec((tail, wide), lambda i: (jnp.maximum(i * per - 1, 0), SC_OFF // wide)),
            rows,
            pl.BlockSpec((SC_CONV, D_MODEL), lambda i: (0, 0)),
            wspec, wspec, wspec,
            pl.BlockSpec((tm, wide), lambda i: (i, MG_OFF // wide)),
        ],
        out_specs=rows,
        out_shape=jax.ShapeDtypeStruct((m, D_MODEL), BF16),
        scratch_shapes=[pltpu.VMEM((tm, D_MODEL), BF16)],
        compiler_params=_params(("arbitrary",)),
        name="merge",
    )(yssm, proj, proj, ynsa, sc_w, w_ssm, w_sc, w_nsa, proj)


def _outproj_kernel(a_ref, w_ref, x_ref, out_ref, *, tn):
    for n0 in range(0, D_MODEL, tn):
        cols = slice(n0, n0 + tn)
        out_ref[:, cols] = x_ref[:, cols] + jnp.dot(a_ref[...], w_ref[:, cols], preferred_element_type=F32)


def _outproj(a, w, x2, tm=512, tn=512):
    m = a.shape[0]
    tm = min(tm, m)
    rows = pl.BlockSpec((tm, D_MODEL), lambda i: (i, 0))
    return pl.pallas_call(
        functools.partial(_outproj_kernel, tn=tn),
        grid=(m // tm,),
        in_specs=[
            rows,
            pl.BlockSpec((D_MODEL, D_MODEL), lambda i: (0, 0), pipeline_mode=pl.Buffered(1)),
            rows,
        ],
        out_specs=rows,
        out_shape=jax.ShapeDtypeStruct((m, D_MODEL), F32),
        compiler_params=_params(("arbitrary",)),
        name="outproj",
    )(a, w, x2)


def _ffn_kernel(x_ref, g_ref, wg_ref, wu_ref, wd_ref, out_ref, h_scr, *, tm):
    @pl.when(pl.program_id(1) == 0)
    def _():
        _norm_to_scratch(x_ref, g_ref, h_scr, tm)
        out_ref[...] = x_ref[...]

    _swiglu_accumulate(h_scr, wg_ref[...], wu_ref[...], wd_ref[...], out_ref, tm)


def _swiglu_accumulate(h_ref, wg, wu, wd, acc_ref, rows, sub=SWIGLU_SUB_ROWS):
    def gate_up(r0):
        h = h_ref[r0:r0 + sub, :]
        return jnp.dot(h, wg, preferred_element_type=F32), jnp.dot(h, wu, preferred_element_type=F32)

    starts = list(range(0, rows, sub))
    staged = gate_up(starts[0])
    for n, r0 in enumerate(starts):
        a, u = staged
        if n + 1 < len(starts):
            staged = gate_up(starts[n + 1])
        acc_ref[r0:r0 + sub, :] += jnp.dot((_silu(a) * u).astype(BF16), wd, preferred_element_type=F32)


def _ffn(x2, g, wg, wu, wd, tm=1024, tf=512):
    m = x2.shape[0]
    tm = min(tm, m)
    return pl.pallas_call(
        functools.partial(_ffn_kernel, tm=tm),
        grid=(m // tm, D_FF // tf),
        in_specs=[
            pl.BlockSpec((tm, D_MODEL), lambda i, f: (i, 0)),
            pl.BlockSpec((1, D_MODEL), lambda i, f: (0, 0)),
            pl.BlockSpec((D_MODEL, tf), lambda i, f: (0, f)),
            pl.BlockSpec((D_MODEL, tf), lambda i, f: (0, f)),
            pl.BlockSpec((tf, D_MODEL), lambda i, f: (f, 0)),
        ],
        out_specs=pl.BlockSpec((tm, D_MODEL), lambda i, f: (i, 0)),
        out_shape=jax.ShapeDtypeStruct((m, D_MODEL), F32),
        scratch_shapes=[pltpu.VMEM((tm, D_MODEL), BF16)],
        compiler_params=_params(("arbitrary", "arbitrary")),
        name="ffn",
    )(x2, g, wg, wu, wd)


SLABS = D_MODEL // LANES


def _route_kernel(x_ref, g_ref, wr_ref, h_ref, idx_ref, wgt_ref):
    h = _rms_rows(x_ref[...], g_ref[...])
    h_ref[...] = h.astype(BF16)
    logits = jnp.dot(h, wr_ref[...], precision=lax.Precision.HIGHEST, preferred_element_type=F32)
    lane = lax.broadcasted_iota(I32, logits.shape, 1)
    logits = jnp.where(lane < N_EXPERTS, logits, -jnp.inf)
    v0 = jnp.max(logits, axis=-1, keepdims=True)
    i0 = jnp.min(jnp.where(logits == v0, lane, LANES), axis=-1, keepdims=True)
    rest = jnp.where(lane == i0, -jnp.inf, logits)
    v1 = jnp.max(rest, axis=-1, keepdims=True)
    i1 = jnp.min(jnp.where(rest == v1, lane, LANES), axis=-1, keepdims=True)
    e1 = jnp.exp(v1 - v0)
    w0 = 1.0 / (1.0 + e1)
    idx_ref[...] = jnp.where(lane == 0, i0, jnp.where(lane == 1, i1, 0))
    wgt_ref[...] = jnp.where(lane == 0, w0, jnp.where(lane == 1, e1 * w0, 0.0))


def _route(x2, g, w_router, tm=256):
    m = x2.shape[0]
    tm = min(tm, m)
    return pl.pallas_call(
        _route_kernel,
        grid=(m // tm,),
        in_specs=[
            pl.BlockSpec((tm, D_MODEL), lambda i: (i, 0)),
            pl.BlockSpec((1, D_MODEL), lambda i: (0, 0)),
            pl.BlockSpec((D_MODEL, LANES), lambda i: (0, 0)),
        ],
        out_specs=[
            pl.BlockSpec((tm, D_MODEL), lambda i: (i, 0)),
            pl.BlockSpec((tm, LANES), lambda i: (i, 0)),
            pl.BlockSpec((tm, LANES), lambda i: (i, 0)),
        ],
        out_shape=[
            jax.ShapeDtypeStruct((m, D_MODEL), BF16),
            jax.ShapeDtypeStruct((m, LANES), I32),
            jax.ShapeDtypeStruct((m, LANES), F32),
        ],
        compiler_params=_params(("arbitrary",)),
        name="route",
    )(x2, g, w_router)


def _gather_kernel(src_ref, in_ref, out_ref, sem, *, rows):
    base = pl.program_id(0) * rows

    def start(r, carry):
        pltpu.make_async_copy(in_ref.at[src_ref[base + r]], out_ref.at[r], sem).start()
        return carry

    lax.fori_loop(0, rows, start, 0, unroll=8)
    pltpu.make_async_copy(in_ref.at[pl.ds(0, rows)], out_ref, sem).wait()


def _gather_rows(src, table, n_out, rows=GATHER_ROWS):
    grid_spec = pltpu.PrefetchScalarGridSpec(
        num_scalar_prefetch=1,
        grid=(n_out // rows,),
        in_specs=[pl.BlockSpec(memory_space=pl.ANY)],
        out_specs=pl.BlockSpec((rows, SLABS, LANES), lambda i, src: (i, 0, 0)),
        scratch_shapes=[pltpu.SemaphoreType.DMA(())],
    )
    out = pl.pallas_call(
        functools.partial(_gather_kernel, rows=rows),
        grid_spec=grid_spec,
        out_shape=jax.ShapeDtypeStruct((n_out, SLABS, LANES), table.dtype),
        compiler_params=pltpu.CompilerParams(dimension_semantics=("arbitrary",)),
        name="gather_rows",
    )(src, table.reshape(table.shape[0], SLABS, LANES))
    return out.reshape(n_out, D_MODEL)


def _gffn_kernel(te_ref, tv_ref, hs_ref, wg_ref, wu_ref, wd_ref, out_ref, acc_scr, wg_scr, wu_scr, wd_scr):
    i = pl.program_id(0)
    f = pl.program_id(1)
    nf = pl.num_programs(1)

    @pl.when(f == 0)
    def _():
        acc_scr[...] = jnp.zeros_like(acc_scr)

    @pl.when(tv_ref[i] > 0)
    def _():
        wg_scr[...] = wg_ref[0].astype(BF16)
        wu_scr[...] = wu_ref[0].astype(BF16)
        wd_scr[...] = wd_ref[0].astype(BF16)
        rows = hs_ref.shape[0]
        _swiglu_accumulate(hs_ref, wg_scr[...], wu_scr[...], wd_scr[...], acc_scr, rows, sub=rows)

    @pl.when(f == nf - 1)
    def _():
        out_ref[...] = acc_scr[...].astype(BF16)


def _gffn(tile_expert, tile_valid, hs, wg, wu, wd, tm=MOE_TILE, tf=512):
    p = hs.shape[0]
    nf = D_FF // tf

    def fblk(f, tv, i):
        return jnp.where(tv[i] > 0, f, nf - 1)

    grid_spec = pltpu.PrefetchScalarGridSpec(
        num_scalar_prefetch=2,
        grid=(p // tm, nf),
        in_specs=[
            pl.BlockSpec((tm, D_MODEL), lambda i, f, te, tv: (i, 0)),
            pl.BlockSpec((1, D_MODEL, tf), lambda i, f, te, tv: (te[i], 0, fblk(f, tv, i))),
            pl.BlockSpec((1, D_MODEL, tf), lambda i, f, te, tv: (te[i], 0, fblk(f, tv, i))),
            pl.BlockSpec((1, tf, D_MODEL), lambda i, f, te, tv: (te[i], fblk(f, tv, i), 0)),
        ],
        out_specs=pl.BlockSpec((tm, D_MODEL), lambda i, f, te, tv: (i, 0)),
        scratch_shapes=[
            pltpu.VMEM((tm, D_MODEL), F32),
            pltpu.VMEM((D_MODEL, tf), BF16),
            pltpu.VMEM((D_MODEL, tf), BF16),
            pltpu.VMEM((tf, D_MODEL), BF16),
        ],
    )
    return pl.pallas_call(
        _gffn_kernel,
        grid_spec=grid_spec,
        out_shape=jax.ShapeDtypeStruct((p, D_MODEL), BF16),
        compiler_params=_params(("arbitrary", "arbitrary")),
        name="grouped_ffn",
    )(tile_expert, tile_valid, hs, wg, wu, wd)


def _combine_kernel(x_ref, y_ref, w_ref, out_ref):
    w = w_ref[...]
    w0 = w[:, 0:1]
    w1 = w[:, 1:2]
    out_ref[...] = (x_ref[...] + w0 * y_ref[:, :D_MODEL].astype(F32) + w1 * y_ref[:, D_MODEL:].astype(F32))


def _combine(x2, y_pairs, top_w, tm=512):
    m = x2.shape[0]
    tm = min(tm, m)
    return pl.pallas_call(
        _combine_kernel,
        grid=(m // tm,),
        in_specs=[
            pl.BlockSpec((tm, D_MODEL), lambda i: (i, 0)),
            pl.BlockSpec((tm, TOP_K * D_MODEL), lambda i: (i, 0)),
            pl.BlockSpec((tm, LANES), lambda i: (i, 0)),
        ],
        out_specs=pl.BlockSpec((tm, D_MODEL), lambda i: (i, 0)),
        out_shape=jax.ShapeDtypeStruct((m, D_MODEL), F32),
        compiler_params=_params(("arbitrary",)),
        name="combine",
    )(x2, y_pairs, top_w)


def _moe(x2, g, w_router, wg, wu, wd):
    m = x2.shape[0]
    tm = MOE_TILE
    h, top_idx, top_w = _route(x2, g, w_router)
    e_flat = top_idx[:, :TOP_K].reshape(-1)
    onehot = (e_flat[:, None] == jnp.arange(N_EXPERTS, dtype=I32)[None, :]).astype(I32)
    ranks = jnp.cumsum(onehot, axis=0) - onehot
    rank = jnp.sum(ranks * onehot, axis=1)
    counts = jnp.sum(onehot, axis=0)
    padded = ((counts + tm - 1) // tm) * tm
    seg_end = jnp.cumsum(padded)
    seg_start = seg_end - padded
    pos = seg_start[e_flat] + rank
    step = math.lcm(tm, GATHER_ROWS)
    p_max = -(-(TOP_K * m + N_EXPERTS * tm) // step) * step
    n_tiles = p_max // tm
    src = (jnp.arange(p_max, dtype=I32) % m).at[pos].set(jnp.arange(TOP_K * m, dtype=I32) // TOP_K)
    tile_start = jnp.arange(n_tiles, dtype=I32) * tm
    tile_valid = (tile_start < seg_end[-1]).astype(I32)
    last_expert = jnp.max(jnp.where(counts > 0, jnp.arange(N_EXPERTS, dtype=I32), 0))
    tile_expert = jnp.sum((tile_start[:, None] >= seg_end[None, :]).astype(I32), axis=1)
    tile_expert = jnp.where(tile_valid > 0, tile_expert, last_expert)

    hs = _gather_rows(src, h, p_max)
    ys = _gffn(tile_expert, tile_valid, hs, wg, wu, wd)
    y_pairs = _gather_rows(pos, ys, TOP_K * m).reshape(m, TOP_K * D_MODEL)
    return _combine(x2, y_pairs, top_w)


def _pad_row(v, width):
    return jnp.pad(v.astype(F32), (0, width - v.shape[0])).reshape(1, width)


def _alibi_tables():
    h = jnp.arange(1, NSA_HEADS + 1, dtype=F32)
    slopes = (jnp.exp2(-8.0 * h / NSA_HEADS) * LOG2E).reshape(NSA_KV_HEADS, NSA_GROUP)
    lanes_tbl = jnp.broadcast_to(jnp.pad(slopes, ((0, 0), (0, 8 - NSA_GROUP)))[:, :, None], (NSA_KV_HEADS, 8, LANES))
    return lanes_tbl, slopes.reshape(-1)


def _mixer_layer(x2, bsz, seq, norm_mix, w_in, ssm_conv_w, ssm_conv_b, ssm_dt_bias, ssm_a_log, ssm_d, ssm_norm,
                 w_ssm_out, sc_conv_w, w_sc_out, q_norm, k_norm, cmp_pos, w_cmp_k, w_cmp_v, w_nsa_out, w_out):
    w_big = jnp.concatenate(
        [w_in[:, _SRC_XBC:_SRC_DT], w_in[:, _SRC_Z:_SRC_XBC], w_in[:, _SRC_SC:_SRC_Q], w_in[:, _SRC_MG:],
         w_in[:, _SRC_Q:_SRC_NG]], axis=1).astype(BF16)
    w_small = jnp.concatenate(
        [w_in[:, _SRC_DT:_SRC_SC], w_in[:, _SRC_NG:_SRC_MG],
         jnp.zeros((D_MODEL, SMALL_W - SSM_HEADS - N_BRANCH * NSA_HEADS), F32)], axis=1).astype(BF16)
    proj, small = _inproj(x2, norm_mix.reshape(1, D_MODEL), w_big, w_small)

    y_ssm = _ssd(proj, small, ssm_conv_w, ssm_conv_b.reshape(1, SSM_XBC), _pad_row(ssm_dt_bias, SMALL_W),
                 _pad_row(ssm_a_log, SMALL_W), jnp.repeat(ssm_d, SSM_HEAD_DIM).reshape(1, D_MODEL),
                 ssm_norm.reshape(1, D_MODEL), bsz, seq)

    G, Dh = NSA_KV_HEADS, NSA_HEAD_DIM
    nb = seq // CMP_STRIDE
    r5 = proj[:, KV_OFF:KV_OFF + 2 * G * Dh].reshape(bsz, nb, CMP_STRIDE, 2, G, Dh)
    r5 = r5.transpose(0, 4, 3, 1, 2, 5).reshape(bsz, G, 2, nb, CMP_STRIDE * Dh)
    lanes_tbl, slopes_flat = _alibi_tables()
    half_k = CMP_STRIDE * Dh
    o_cmp, qn, sel = _nsa_cmp(r5, proj, w_cmp_k.reshape(2, half_k, Dh).astype(BF16),
                              w_cmp_v.reshape(2, half_k, Dh).astype(BF16), cmp_pos.reshape(2, 2, half_k),
                              k_norm[0:1], q_norm.reshape(1, Dh), lanes_tbl, bsz, seq)
    sel_flat = sel[..., :SEL_TOPK].reshape(-1)
    y_nsa = _nsa_attn(sel_flat, slopes_flat, qn, proj, o_cmp, small, k_norm, bsz, seq)

    mixed = _merge(y_ssm, proj, y_nsa, sc_conv_w, w_ssm_out.astype(BF16), w_sc_out.astype(BF16),
                   w_nsa_out.astype(BF16), seq)
    return _outproj(mixed, w_out.astype(BF16), x2)


def kernel(x, norm_mix, w_in, ssm_conv_w, ssm_conv_b, ssm_dt_bias, ssm_a_log, ssm_d, ssm_norm, w_ssm_out,
           sc_conv_w, w_sc_out, q_norm, k_norm, cmp_pos, w_cmp_k, w_cmp_v, w_nsa_out, w_out, norm_ffn,
           ffn_w_gate, ffn_w_up, ffn_w_down, moe_router, moe_w_gate, moe_w_up, moe_w_down):
    bsz, seq, _ = x.shape
    depth = norm_mix.shape[0]
    x2 = x.reshape(bsz * seq, D_MODEL)
    for layer in range(depth):
        x2 = _mixer_layer(x2, bsz, seq, norm_mix[layer], w_in[layer], ssm_conv_w[layer], ssm_conv_b[layer],
                          ssm_dt_bias[layer], ssm_a_log[layer], ssm_d[layer], ssm_norm[layer], w_ssm_out[layer],
                          sc_conv_w[layer], w_sc_out[layer], q_norm[layer], k_norm[layer], cmp_pos[layer],
                          w_cmp_k[layer], w_cmp_v[layer], w_nsa_out[layer], w_out[layer])
        g = norm_ffn[layer].reshape(1, D_MODEL)
        i = layer // 2
        if layer % 2 == 0:
            x2 = _ffn(x2, g, ffn_w_gate[i].astype(BF16), ffn_w_up[i].astype(BF16), ffn_w_down[i].astype(BF16))
        else:
            w_router = jnp.pad(moe_router[i], ((0, 0), (0, LANES - N_EXPERTS)))
            x2 = _moe(x2, g, w_router, moe_w_gate[i], moe_w_up[i], moe_w_down[i])
    return x2.reshape(bsz, seq, D_MODEL)
```

```python
import functools
import math

import jax
import jax.numpy as jnp
from jax import lax
from jax.experimental import pallas as pl
from jax.experimental.pallas import tpu as pltpu

F32 = jnp.float32
BF16 = jnp.bfloat16
I32 = jnp.int32

D_MODEL = 2048
EPS = 1e-6
NEG_INF = -1e30
MASKED_DIST = 1e30
LOG2E = 1.4426950408889634
FORCED_SCORE = 1e4
SSM_HEAD_DIM = 64
SSM_HEADS = D_MODEL // SSM_HEAD_DIM
SSM_GROUPS = 8
SSM_HEADS_PER_GROUP = SSM_HEADS // SSM_GROUPS
SSM_STATE = 128
SSM_CONV = 4
SSM_CHUNK = 128
SSM_XBC = D_MODEL + 2 * SSM_GROUPS * SSM_STATE
SC_CONV = 3
NSA_HEADS = 16
NSA_HEAD_DIM = 128
NSA_KV_HEADS = 4
NSA_GROUP = NSA_HEADS // NSA_KV_HEADS
CMP_STRIDE = 16
CMP_LEN = 2 * CMP_STRIDE
SEL_BLOCK = 64
SEL_TOPK = 8
WINDOW = 512
N_BRANCH = 3
D_FF = 5632
N_EXPERTS = 8
TOP_K = 2

XBC_OFF = 0
Z_OFF = XBC_OFF + SSM_XBC
SC_OFF = Z_OFF + D_MODEL
MG_OFF = SC_OFF + 3 * D_MODEL
Q_OFF = MG_OFF + N_BRANCH * D_MODEL
KV_OFF = Q_OFF + NSA_HEADS * NSA_HEAD_DIM
PROJ_W = KV_OFF + 6 * NSA_KV_HEADS * NSA_HEAD_DIM
LANES = 128
SMALL_W = LANES
GATE_LANE0 = SSM_HEADS
_SRC_Z = 0
_SRC_XBC = D_MODEL
_SRC_DT = _SRC_XBC + SSM_XBC
_SRC_SC = _SRC_DT + SSM_HEADS
_SRC_Q = _SRC_SC + 3 * D_MODEL
_SRC_KV = _SRC_Q + NSA_HEADS * NSA_HEAD_DIM
_SRC_NG = _SRC_KV + 6 * NSA_KV_HEADS * NSA_HEAD_DIM
_SRC_MG = _SRC_NG + N_BRANCH * NSA_HEADS

VMEM_LIMIT = 56 * 1024 * 1024
MOE_TILE = 704
GATHER_ROWS = 512
NSA_Q_BLOCKS_PER_STEP = 8
NSA_ATTN_SLOTS = 8
SWIGLU_SUB_ROWS = 256


def _params(sem, vmem=VMEM_LIMIT):
    return pltpu.CompilerParams(dimension_semantics=sem, vmem_limit_bytes=vmem)


def _sigmoid(x):
    return 1.0 / (1.0 + jnp.exp2(x * (-LOG2E)))


def _silu(x):
    return x * _sigmoid(x)


def _rms_rows(x, g):
    return x * lax.rsqrt(jnp.mean(x * x, axis=-1, keepdims=True) + EPS) * g


def _norm_to_scratch(x_ref, g_ref, h_scr, rows, chunk=256):
    g = g_ref[...]

    def body(i, carry):
        r0 = pl.multiple_of(i * chunk, chunk)
        h_scr[pl.ds(r0, chunk), :] = _rms_rows(x_ref[pl.ds(r0, chunk), :], g).astype(BF16)
        return carry

    lax.fori_loop(0, rows // chunk, body, 0)


def _inproj_kernel(x_ref, g_ref, w_ref, ws_ref, proj_ref, small_ref, h_scr, *, tm):
    @pl.when(pl.program_id(1) == 0)
    def _():
        _norm_to_scratch(x_ref, g_ref, h_scr, tm)
        small_ref[...] = jnp.dot(h_scr[...], ws_ref[...], preferred_element_type=F32)

    proj_ref[...] = jnp.dot(h_scr[...], w_ref[...], preferred_element_type=F32).astype(BF16)


def _inproj(x2, g, w_big, w_small, tm=1024, tn=1024):
    m = x2.shape[0]
    tm = min(tm, m)
    return pl.pallas_call(
        functools.partial(_inproj_kernel, tm=tm),
        grid=(m // tm, PROJ_W // tn),
        in_specs=[
            pl.BlockSpec((tm, D_MODEL), lambda i, j: (i, 0)),
            pl.BlockSpec((1, D_MODEL), lambda i, j: (0, 0)),
            pl.BlockSpec((D_MODEL, tn), lambda i, j: (0, j)),
            pl.BlockSpec((D_MODEL, SMALL_W), lambda i, j: (0, 0)),
        ],
        out_specs=[
            pl.BlockSpec((tm, tn), lambda i, j: (i, j)),
            pl.BlockSpec((tm, SMALL_W), lambda i, j: (i, 0)),
        ],
        out_shape=[
            jax.ShapeDtypeStruct((m, PROJ_W), BF16),
            jax.ShapeDtypeStruct((m, SMALL_W), F32),
        ],
        scratch_shapes=[pltpu.VMEM((tm, D_MODEL), BF16)],
        compiler_params=_params(("arbitrary", "arbitrary")),
        name="inproj",
    )(x2, g, w_big, w_small)


def _shift_rows(cur, prev_tail, s):
    rc = pltpu.roll(cur, s, 0)
    rp = pltpu.roll(prev_tail, s, 0)
    row = lax.broadcasted_iota(I32, (8, cur.shape[1]), 0)
    top = jnp.where(row < s, rp[0:8], rc[0:8])
    return jnp.concatenate([top, rc[8:]], axis=0)


def _ssd_kernel(xbc_ref, prev_ref, z_ref, small_ref, cw_ref, cb_ref, dtb_ref, alog_ref, dskip_ref,
                ng_ref, out_ref, state_scr, y_scr):
    L = SSM_CHUNK
    P = SSM_HEAD_DIM
    R = SSM_HEADS_PER_GROUP
    GW = R * P
    c = pl.program_id(1)

    @pl.when(c == 0)
    def _():
        state_scr[...] = jnp.zeros_like(state_scr)

    keep_prev = (c > 0).astype(F32)

    def conv_silu(lo, w):
        cur = xbc_ref[:, lo:lo + w].astype(F32)
        prev = prev_ref[:, lo:lo + w].astype(F32) * keep_prev
        acc = cur * cw_ref[SSM_CONV - 1:SSM_CONV, lo:lo + w] + cb_ref[:, lo:lo + w]
        for s in range(1, SSM_CONV):
            k = SSM_CONV - 1 - s
            acc = acc + _shift_rows(cur, prev, s) * cw_ref[k:k + 1, lo:lo + w]
        return _silu(acc)

    row = lax.broadcasted_iota(I32, (L, L), 0)
    col = lax.broadcasted_iota(I32, (L, L), 1)
    causal = row >= col
    tril = causal.astype(F32)

    pre = small_ref[...] + dtb_ref[...]
    dt = jnp.maximum(pre, 0.0) + jnp.log(1.0 + jnp.exp(-jnp.abs(pre)))
    dt = jnp.where(col < SSM_HEADS, dt, 0.0)
    a_neg = jnp.exp(alog_ref[...]) * (-LOG2E)
    a_cum = jnp.dot(tril, dt * a_neg, precision=lax.Precision.HIGHEST, preferred_element_type=F32)
    a_cum_t = a_cum.T
    a_last = a_cum[L - 1:L, :]
    dec_end = jnp.exp2(a_last - a_cum)
    dec_in = jnp.exp2(a_cum)
    chunk_dec = jnp.exp2(a_last)

    lane_g = lax.broadcasted_iota(I32, (L, GW), 1)
    lane_g1 = lax.broadcasted_iota(I32, (1, GW), 1)
    head_mask = [jnp.where((lane_g >= r * P) & (lane_g < (r + 1) * P), 1.0, 0.0).astype(BF16) for r in range(R)]

    def expand(mat, g, lanes):
        out = mat[:, R * g + R - 1:R * g + R]
        for r in range(R - 2, -1, -1):
            out = jnp.where(lanes < (r + 1) * P, mat[:, R * g + r:R * g + r + 1], out)
        return out

    for g in range(SSM_GROUPS):
        xs = conv_silu(g * GW, GW)
        b_in = conv_silu(D_MODEL + g * SSM_STATE, SSM_STATE)
        c_out = conv_silu(D_MODEL + (SSM_GROUPS + g) * SSM_STATE, SSM_STATE)
        b_bf = b_in.astype(BF16)
        c_bf = c_out.astype(BF16)
        cb = lax.dot_general(c_bf, b_bf, (((1,), (1,)), ((), ())), preferred_element_type=F32)
        xdt = xs * expand(dt, g, lane_g)
        xdt_bf = xdt.astype(BF16)
        y = xs * dskip_ref[:, g * GW:(g + 1) * GW]
        for r in range(R):
            h = R * g + r
            seg = a_cum[:, h:h + 1] - a_cum_t[h:h + 1, :]
            lmat = jnp.where(causal, jnp.exp2(seg), 0.0)
            y = y + jnp.dot((cb * lmat).astype(BF16), xdt_bf * head_mask[r], preferred_element_type=F32)
        h_prev = state_scr[g]
        y = y + jnp.dot(c_bf, h_prev.astype(BF16), preferred_element_type=F32) * expand(dec_in, g, lane_g)
        xdt_end = (xdt * expand(dec_end, g, lane_g)).astype(BF16)
        new_state = jnp.dot(b_in.T.astype(BF16), xdt_end, preferred_element_type=F32)
        state_scr[g] = h_prev * expand(chunk_dec, g, lane_g1) + new_state
        y_scr[:, g * GW:(g + 1) * GW] = y

    yz = y_scr[...] * _silu(z_ref[...].astype(F32))
    out_ref[...] = _rms_rows(yz, ng_ref[...]).astype(BF16)


def _ssd(proj, small, conv_w, conv_b, dt_bias, a_log, d_skip, norm_g, bsz, seq):
    L = SSM_CHUNK
    nc = seq // L
    tail = 16
    per = L // tail
    return pl.pallas_call(
        _ssd_kernel,
        grid=(bsz, nc),
        in_specs=[
            pl.BlockSpec((L, SSM_XBC), lambda b, c: (b * nc + c, XBC_OFF // SSM_XBC)),
            pl.BlockSpec((tail, SSM_XBC), lambda b, c: (jnp.maximum((b * nc + c) * per - 1, 0), XBC_OFF // SSM_XBC)),
            pl.BlockSpec((L, D_MODEL), lambda b, c: (b * nc + c, Z_OFF // D_MODEL)),
            pl.BlockSpec((L, SMALL_W), lambda b, c: (b * nc + c, 0)),
            pl.BlockSpec((SSM_CONV, SSM_XBC), lambda b, c: (0, 0)),
            pl.BlockSpec((1, SSM_XBC), lambda b, c: (0, 0)),
            pl.BlockSpec((1, SMALL_W), lambda b, c: (0, 0)),
            pl.BlockSpec((1, SMALL_W), lambda b, c: (0, 0)),
            pl.BlockSpec((1, D_MODEL), lambda b, c: (0, 0)),
            pl.BlockSpec((1, D_MODEL), lambda b, c: (0, 0)),
        ],
        out_specs=pl.BlockSpec((L, D_MODEL), lambda b, c: (b * nc + c, 0)),
        out_shape=jax.ShapeDtypeStruct((bsz * seq, D_MODEL), BF16),
        scratch_shapes=[
            pltpu.VMEM((SSM_GROUPS, SSM_STATE, SSM_HEADS_PER_GROUP * SSM_HEAD_DIM), F32),
            pltpu.VMEM((L, D_MODEL), F32),
        ],
        compiler_params=_params(("arbitrary", "arbitrary")),
        name="ssd",
    )(proj, proj, proj, small, conv_w, conv_b, dt_bias, a_log, d_skip, norm_g)


def _nsa_cmp_kernel(r_ref, q_ref, wk_ref, wv_ref, pos_ref, kng_ref, qng_ref, slope_ref,
                    ocmp_ref, qn_ref, idx_ref, kc_scr, vc_scr, *, seq, tq):
    Dh = NSA_HEAD_DIM
    nb = seq // CMP_STRIDE
    nqb = seq // SEL_BLOCK

    def compress(j, w_ref):
        t = r_ref[0, 0, j].astype(F32)
        top = jnp.dot((t + pos_ref[j, 0:1, :]).astype(BF16), w_ref[0], preferred_element_type=F32)
        bot = jnp.dot((t + pos_ref[j, 1:2, :]).astype(BF16), w_ref[1], preferred_element_type=F32)
        return top + pltpu.roll(bot, nb - 1, 0)

    kc_scr[...] = _rms_rows(compress(0, wk_ref), kng_ref[...]).T.astype(BF16)
    vc_scr[...] = compress(1, wv_ref).astype(BF16)
    k_cmp_t = kc_scr[...]
    v_cmp = vc_scr[...]

    n_start = lax.broadcasted_iota(I32, (1, nb), 1) * CMP_STRIDE
    qg = qng_ref[...] * (Dh ** -0.5 * LOG2E)

    def scores(qn, qpos, slope):
        return probs(jnp.dot(qn, k_cmp_t, preferred_element_type=F32), qpos, slope)

    def probs(s, qpos, slope):
        dist = (qpos - n_start).astype(F32) - (CMP_LEN - 1) / 2
        valid = (n_start + (CMP_LEN - 1)) <= qpos
        s = jnp.where(valid, s - slope * dist, NEG_INF)
        p = jnp.exp2(s - jnp.max(s, axis=-1, keepdims=True))
        p = p / jnp.sum(p, axis=-1, keepdims=True)
        return jnp.where(valid, p, 0.0)

    sel_row = lax.broadcasted_iota(I32, (nqb, seq), 0) * SEL_BLOCK
    sel_col = lax.broadcasted_iota(I32, (nqb, seq), 1)
    pick_first = jnp.where(sel_row == sel_col, 1.0, 0.0).astype(BF16)
    first_pos = lax.broadcasted_iota(I32, (nqb, 1), 0) * SEL_BLOCK
    p_first = jnp.zeros((nqb, nb), F32)

    slopes = [slope_ref[0, r:r + 1, 0:1] for r in range(NSA_GROUP)]

    def body(i, carry):
        r0 = pl.multiple_of(i * tq, tq)
        qpos = r0 + lax.broadcasted_iota(I32, (tq, 1), 0)
        def raw_scores(r):
            qn = _rms_rows(q_ref[pl.ds(r0, tq), r * Dh:(r + 1) * Dh].astype(F32), qg).astype(BF16)
            qn_ref[pl.ds(r0, tq), r * Dh:(r + 1) * Dh] = qn
            return jnp.dot(qn, k_cmp_t, preferred_element_type=F32)

        staged = raw_scores(0)
        for r in range(NSA_GROUP):
            s = staged
            if r + 1 < NSA_GROUP:
                staged = raw_scores(r + 1)
            p = probs(s, qpos, slopes[r])
            ocmp_ref[pl.ds(r0, tq), r * Dh:(r + 1) * Dh] = jnp.dot(
                p.astype(BF16), v_cmp, preferred_element_type=F32).astype(BF16)
        return carry

    lax.fori_loop(0, seq // tq, body, 0)
    for r in range(NSA_GROUP):
        q_first = jnp.dot(pick_first, qn_ref[:, r * Dh:(r + 1) * Dh], preferred_element_type=F32).astype(BF16)
        p_first = p_first + scores(q_first, first_pos, slopes[r])

    n_lo = lax.broadcasted_iota(I32, (nb, nqb), 0) * CMP_STRIDE
    j_lo = lax.broadcasted_iota(I32, (nb, nqb), 1) * SEL_BLOCK
    overlap = jnp.maximum(jnp.minimum(n_lo + CMP_LEN, j_lo + SEL_BLOCK) - jnp.maximum(n_lo, j_lo), 0)
    overlap = overlap.astype(F32) / CMP_LEN
    imp = jnp.dot(p_first, overlap, precision=lax.Precision.HIGHEST, preferred_element_type=F32)
    qb_i = lax.broadcasted_iota(I32, (nqb, nqb), 0)
    blk_j = lax.broadcasted_iota(I32, (nqb, nqb), 1)
    forced = (blk_j == 0) | (blk_j == qb_i) | (blk_j == qb_i - 1)
    imp = jnp.where(forced, FORCED_SCORE, jnp.where(blk_j > qb_i, -FORCED_SCORE, imp))
    out_lane = lax.broadcasted_iota(I32, (nqb, LANES), 1)
    picked = jnp.zeros((nqb, LANES), I32)
    for k in range(SEL_TOPK):
        best = jnp.max(imp, axis=-1, keepdims=True)
        arg = jnp.min(jnp.where(imp == best, blk_j, nqb), axis=-1, keepdims=True)
        picked = jnp.where(out_lane == k, arg, picked)
        imp = jnp.where(blk_j == arg, -jnp.inf, imp)
    idx_ref[0, 0] = picked


def _nsa_cmp(r5, proj, wk, wv, pos, kng, qng, slopes, bsz, seq):
    G = NSA_KV_HEADS
    nb = seq // CMP_STRIDE
    nqb = seq // SEL_BLOCK
    qw = NSA_GROUP * NSA_HEAD_DIM
    return pl.pallas_call(
        functools.partial(_nsa_cmp_kernel, seq=seq, tq=min(512, seq)),
        grid=(bsz, G),
        in_specs=[
            pl.BlockSpec((1, 1, 2, nb, CMP_STRIDE * NSA_HEAD_DIM), lambda b, g: (b, g, 0, 0, 0)),
            pl.BlockSpec((seq, qw), lambda b, g: (b, Q_OFF // qw + g)),
            pl.BlockSpec((2, CMP_STRIDE * NSA_HEAD_DIM, NSA_HEAD_DIM), lambda b, g: (0, 0, 0)),
            pl.BlockSpec((2, CMP_STRIDE * NSA_HEAD_DIM, NSA_HEAD_DIM), lambda b, g: (0, 0, 0)),
            pl.BlockSpec((2, 2, CMP_STRIDE * NSA_HEAD_DIM), lambda b, g: (0, 0, 0)),
            pl.BlockSpec((1, NSA_HEAD_DIM), lambda b, g: (0, 0)),
            pl.BlockSpec((1, NSA_HEAD_DIM), lambda b, g: (0, 0)),
            pl.BlockSpec((1, 8, LANES), lambda b, g: (g, 0, 0)),
        ],
        out_specs=[
            pl.BlockSpec((seq, qw), lambda b, g: (b, g)),
            pl.BlockSpec((seq, qw), lambda b, g: (b, g)),
            pl.BlockSpec((1, 1, nqb, LANES), lambda b, g: (b, g, 0, 0)),
        ],
        out_shape=[
            jax.ShapeDtypeStruct((bsz * seq, NSA_HEADS * NSA_HEAD_DIM), BF16),
            jax.ShapeDtypeStruct((bsz * seq, NSA_HEADS * NSA_HEAD_DIM), BF16),
            jax.ShapeDtypeStruct((bsz, G, nqb, LANES), I32),
        ],
        scratch_shapes=[pltpu.VMEM((NSA_HEAD_DIM, nb), BF16), pltpu.VMEM((nb, NSA_HEAD_DIM), BF16)],
        compiler_params=_params(("arbitrary", "arbitrary")),
        name="nsa_cmp",
    )(r5, proj, wk, wv, pos, kng, qng, slopes)


def _nsa_attn_kernel(idx_ref, slope_ref, qn_ref, ks_ref, vs_ref, kw_ref, vw_ref, ocmp_ref, gate_ref, kng_ref,
                     out_ref, ksn_scr, kwn_scr, vwp_scr, wdist_scr, *sp_scr, seq, qps):
    s_scrs = sp_scr[:NSA_ATTN_SLOTS]
    p_scrs = sp_scr[NSA_ATTN_SLOTS:]
    Dh = NSA_HEAD_DIM
    R = NSA_GROUP
    QB = SEL_BLOCK
    nqb = seq // QB
    span = WINDOW + QB
    b = pl.program_id(0)
    g = pl.program_id(1)
    step = pl.program_id(2)
    q_in = lax.broadcasted_iota(I32, (QB, 1), 0)

    @pl.when(step == 0)
    def _():
        kwn_scr[0:WINDOW, :] = jnp.zeros((WINDOW, Dh), BF16)
        vwp_scr[0:WINDOW, :] = jnp.zeros((WINDOW, Dh), BF16)

        def body(i, carry):
            r0 = pl.multiple_of(i * 256, 256)
            ksn_scr[pl.ds(r0, 256), :] = _rms_rows(ks_ref[pl.ds(r0, 256), :].astype(F32), kng_ref[1:2, :]).astype(BF16)
            kwn_scr[pl.ds(WINDOW + r0, 256), :] = _rms_rows(kw_ref[pl.ds(r0, 256), :].astype(F32),
                                                           kng_ref[2:3, :]).astype(BF16)
            vwp_scr[pl.ds(WINDOW + r0, 256), :] = vw_ref[pl.ds(r0, 256), :]
            return carry

        lax.fori_loop(0, seq // 256, body, 0)
        dist_w = q_in + WINDOW - lax.broadcasted_iota(I32, (QB, span), 1)
        wdist_scr[...] = jnp.where((dist_w >= 0) & (dist_w < WINDOW), dist_w.astype(F32), MASKED_DIST)

    nk = SEL_TOPK * QB
    key_lane = lax.broadcasted_iota(I32, (1, nk), 1)
    win_lane = lax.broadcasted_iota(I32, (1, span), 1)
    lane = lax.broadcasted_iota(I32, (QB, LANES), 1)
    slopes = [slope_ref[g * R + r] for r in range(R)]

    def qk(slot, q4, k, width):
        s_scrs[slot][:, 0:width] = lax.dot_general(q4, k, (((1,), (1,)), ((), ())), preferred_element_type=F32)

    def softmax(slot, dist, width):
        s_scr = s_scrs[slot]
        p_scr = p_scrs[slot]
        inv = []
        for r in range(R):
            s = s_scr[r * QB:(r + 1) * QB, 0:width] - slopes[r] * dist
            p = jnp.exp2(s - jnp.max(s, axis=-1, keepdims=True))
            inv.append(1.0 / jnp.sum(p, axis=-1, keepdims=True))
            p_scr[r * QB:(r + 1) * QB, 0:width] = p.astype(BF16)
        return inv

    def score_stage(i):
        qb = step * qps + i
        q4 = jnp.concatenate([qn_ref[i * QB:(i + 1) * QB, r * Dh:(r + 1) * Dh] for r in range(R)], axis=0)
        base = ((b * NSA_KV_HEADS + g) * nqb + qb) * SEL_TOPK
        starts = [pl.multiple_of(idx_ref[base + k] * QB, QB) for k in range(SEL_TOPK)]
        slot = (2 * i) % NSA_ATTN_SLOTS
        qk(slot, q4, jnp.concatenate([ksn_scr[pl.ds(st, QB), :] for st in starts], axis=0), nk)
        wstart = pl.multiple_of(qb * QB, QB)
        qk(slot + 1, q4, kwn_scr[pl.ds(wstart, span), :], span)
        return qb, starts, slot, wstart

    def softmax_stage(ctx):
        qb, starts, slot, wstart = ctx
        kpos_s = key_lane & (QB - 1)
        for k in range(SEL_TOPK):
            kpos_s = kpos_s + jnp.where((key_lane >= k * QB) & (key_lane < (k + 1) * QB), starts[k], 0)
        dist_s = (q_in + qb * QB - kpos_s).astype(F32)
        dist_s = jnp.where(dist_s >= 0, dist_s, MASKED_DIST)
        dist_w = jnp.where(win_lane >= WINDOW - qb * QB, wdist_scr[...], MASKED_DIST)
        return softmax(slot, dist_s, nk), softmax(slot + 1, dist_w, span)

    def output_stage(i, ctx, inv):
        qb, starts, slot, wstart = ctx
        rows = slice(i * QB, (i + 1) * QB)
        v_sel = jnp.concatenate([vs_ref[pl.ds(st, QB), :] for st in starts], axis=0)
        o_sel = jnp.dot(p_scrs[slot][:, 0:nk], v_sel, preferred_element_type=F32)
        o_win = jnp.dot(p_scrs[slot + 1][:, 0:span], vwp_scr[pl.ds(wstart, span), :], preferred_element_type=F32)
        sig = _sigmoid(gate_ref[rows, :])

        def gate(br, r):
            return jnp.sum(jnp.where(lane == GATE_LANE0 + br * NSA_HEADS + g * R + r, sig, 0.0), axis=-1, keepdims=True)

        for r in range(R):
            hr = slice(r * QB, (r + 1) * QB)
            o_cmp = ocmp_ref[rows, r * Dh:(r + 1) * Dh].astype(F32)
            o = (gate(0, r) * o_cmp + (gate(1, r) * inv[0][r]) * o_sel[hr] + (gate(2, r) * inv[1][r]) * o_win[hr])
            out_ref[rows, r * Dh:(r + 1) * Dh] = o.astype(BF16)

    ctx = score_stage(0)
    for i in range(qps):
        nxt = score_stage(i + 1) if i + 1 < qps else None
        output_stage(i, ctx, softmax_stage(ctx))
        ctx = nxt


def _nsa_attn(sel_flat, slopes, qn, proj, ocmp, small, kng, bsz, seq):
    G = NSA_KV_HEADS
    Dh = NSA_HEAD_DIM
    QB = SEL_BLOCK
    nqb = seq // QB
    qw = NSA_GROUP * Dh
    kv0 = KV_OFF // Dh

    def kv_spec(j):
        return pl.BlockSpec((seq, Dh), lambda b, g, q, idx, sl: (b, kv0 + j * G + g))

    qps = NSA_Q_BLOCKS_PER_STEP
    steps = nqb // qps
    tq = qps * QB
    span = WINDOW + QB
    grid_spec = pltpu.PrefetchScalarGridSpec(
        num_scalar_prefetch=2,
        grid=(bsz, G, steps),
        in_specs=[
            pl.BlockSpec((tq, qw), lambda b, g, q, idx, sl: (b * steps + q, g)),
            kv_spec(2), kv_spec(3), kv_spec(4), kv_spec(5),
            pl.BlockSpec((tq, qw), lambda b, g, q, idx, sl: (b * steps + q, g)),
            pl.BlockSpec((tq, SMALL_W), lambda b, g, q, idx, sl: (b * steps + q, 0)),
            pl.BlockSpec((N_BRANCH, Dh), lambda b, g, q, idx, sl: (0, 0)),
        ],
        out_specs=pl.BlockSpec((tq, qw), lambda b, g, q, idx, sl: (b * steps + q, g)),
        scratch_shapes=[
            pltpu.VMEM((seq, Dh), BF16),
            pltpu.VMEM((WINDOW + seq, Dh), BF16),
            pltpu.VMEM((WINDOW + seq, Dh), BF16),
            pltpu.VMEM((QB, span), F32),
        ] + [pltpu.VMEM((NSA_GROUP * QB, span), F32)] * NSA_ATTN_SLOTS
          + [pltpu.VMEM((NSA_GROUP * QB, span), BF16)] * NSA_ATTN_SLOTS,
    )
    return pl.pallas_call(
        functools.partial(_nsa_attn_kernel, seq=seq, qps=qps),
        grid_spec=grid_spec,
        out_shape=jax.ShapeDtypeStruct((bsz * seq, NSA_HEADS * Dh), BF16),
        compiler_params=_params(("arbitrary", "arbitrary", "arbitrary")),
        name="nsa_attn",
    )(sel_flat, slopes, qn, proj, proj, proj, proj, ocmp, small, kng)


def _merge_kernel(yssm_ref, bcx_ref, bcxp_ref, ynsa_ref, scw_ref, w0_ref, w1_ref, w2_ref, g_ref, out_ref, ysc_scr,
                  *, tm, tiles_per_seq, tn):
    keep_prev = (pl.program_id(0) % tiles_per_seq > 0).astype(F32)
    chunk = 128
    tail = bcxp_ref.shape[0]
    B = slice(0, D_MODEL)
    C = slice(D_MODEL, 2 * D_MODEL)
    X = slice(2 * D_MODEL, 3 * D_MODEL)

    def body(i, carry):
        r0 = pl.multiple_of(i * chunk, chunk)
        u = bcx_ref[pl.ds(r0, chunk), C].astype(F32) * bcx_ref[pl.ds(r0, chunk), X].astype(F32)
        p0 = pl.multiple_of(jnp.maximum(r0 - tail, 0), tail)
        prev_in = bcx_ref[pl.ds(p0, tail), C].astype(F32) * bcx_ref[pl.ds(p0, tail), X].astype(F32)
        prev_out = bcxp_ref[:, C].astype(F32) * bcxp_ref[:, X].astype(F32) * keep_prev
        prev = jnp.where(i > 0, prev_in, prev_out)
        acc = u * scw_ref[SC_CONV - 1:SC_CONV, :]
        for s in range(1, SC_CONV):
            k = SC_CONV - 1 - s
            acc = acc + _shift_rows(u, prev, s) * scw_ref[k:k + 1, :]
        ysc_scr[pl.ds(r0, chunk), :] = (bcx_ref[pl.ds(r0, chunk), B].astype(F32) * acc).astype(BF16)
        return carry

    lax.fori_loop(0, tm // chunk, body, 0)

    def branch_dots(n0):
        cols = slice(n0, n0 + tn)
        return (jnp.dot(yssm_ref[...], w0_ref[:, cols], preferred_element_type=F32),
                jnp.dot(ysc_scr[...], w1_ref[:, cols], preferred_element_type=F32),
                jnp.dot(ynsa_ref[...], w2_ref[:, cols], preferred_element_type=F32))

    staged = branch_dots(0)
    for n0 in range(0, D_MODEL, tn):
        a = staged
        if n0 + tn < D_MODEL:
            staged = branch_dots(n0 + tn)
        gate = [_sigmoid(g_ref[:, k * D_MODEL + n0:k * D_MODEL + n0 + tn].astype(F32)) for k in range(N_BRANCH)]
        out_ref[:, n0:n0 + tn] = (gate[0] * a[0] + gate[1] * a[1] + gate[2] * a[2]).astype(BF16)


def _merge(yssm, proj, ynsa, sc_w, w_ssm, w_sc, w_nsa, seq, tm=256, tn=512):
    m = yssm.shape[0]
    tm = min(tm, seq)
    tail = 16
    per = tm // tail
    wide = N_BRANCH * D_MODEL
    rows = pl.BlockSpec((tm, D_MODEL), lambda i: (i, 0))
    wspec = pl.BlockSpec((D_MODEL, D_MODEL), lambda i: (0, 0), pipeline_mode=pl.Buffered(1))
    return pl.pallas_call(
        functools.partial(_merge_kernel, tm=tm, tiles_per_seq=seq // tm, tn=tn),
        grid=(m // tm,),
        in_specs=[
            rows,
            pl.BlockSpec((tm, wide), lambda i: (i, SC_OFF // wide)),
            pl.BlockSpec((tail, wide), lambda i: (jnp.maximum(i * per - 1, 0), SC_OFF // wide)),
            rows,
            pl.BlockSpec((SC_CONV, D_MODEL), lambda i: (0, 0)),
            wspec, wspec, wspec,
            pl.BlockSpec((tm, wide), lambda i: (i, MG_OFF // wide)),
        ],
        out_specs=rows,
        out_shape=jax.ShapeDtypeStruct((m, D_MODEL), BF16),
        scratch_shapes=[pltpu.VMEM((tm, D_MODEL), BF16)],
        compiler_params=_params(("arbitrary",)),
        name="merge",
    )(yssm, proj, proj, ynsa, sc_w, w_ssm, w_sc, w_nsa, proj)


def _outproj_kernel(a_ref, w_ref, x_ref, out_ref, *, tn):
    for n0 in range(0, D_MODEL, tn):
        cols = slice(n0, n0 + tn)
        out_ref[:, cols] = x_ref[:, cols] + jnp.dot(a_ref[...], w_ref[:, cols], preferred_element_type=F32)


def _outproj(a, w, x2, tm=512, tn=512):
    m = a.shape[0]
    tm = min(tm, m)
    rows = pl.BlockSpec((tm, D_MODEL), lambda i: (i, 0))
    return pl.pallas_call(
        functools.partial(_outproj_kernel, tn=tn),
        grid=(m // tm,),
        in_specs=[
            rows,
            pl.BlockSpec((D_MODEL, D_MODEL), lambda i: (0, 0), pipeline_mode=pl.Buffered(1)),
            rows,
        ],
        out_specs=rows,
        out_shape=jax.ShapeDtypeStruct((m, D_MODEL), F32),
        compiler_params=_params(("arbitrary",)),
        name="outproj",
    )(a, w, x2)


def _ffn_kernel(x_ref, g_ref, wg_ref, wu_ref, wd_ref, out_ref, h_scr, *, tm):
    @pl.when(pl.program_id(1) == 0)
    def _():
        _norm_to_scratch(x_ref, g_ref, h_scr, tm)
        out_ref[...] = x_ref[...]

    _swiglu_accumulate(h_scr, wg_ref[...], wu_ref[...], wd_ref[...], out_ref, tm)


def _swiglu_accumulate(h_ref, wg, wu, wd, acc_ref, rows, sub=SWIGLU_SUB_ROWS):
    def gate_up(r0):
        h = h_ref[r0:r0 + sub, :]
        return jnp.dot(h, wg, preferred_element_type=F32), jnp.dot(h, wu, preferred_element_type=F32)

    starts = list(range(0, rows, sub))
    staged = gate_up(starts[0])
    for n, r0 in enumerate(starts):
        a, u = staged
        if n + 1 < len(starts):
            staged = gate_up(starts[n + 1])
        acc_ref[r0:r0 + sub, :] += jnp.dot((_silu(a) * u).astype(BF16), wd, preferred_element_type=F32)


def _ffn(x2, g, wg, wu, wd, tm=1024, tf=512):
    m = x2.shape[0]
    tm = min(tm, m)
    return pl.pallas_call(
        functools.partial(_ffn_kernel, tm=tm),
        grid=(m // tm, D_FF // tf),
        in_specs=[
            pl.BlockSpec((tm, D_MODEL), lambda i, f: (i, 0)),
            pl.BlockSpec((1, D_MODEL), lambda i, f: (0, 0)),
            pl.BlockSpec((D_MODEL, tf), lambda i, f: (0, f)),
            pl.BlockSpec((D_MODEL, tf), lambda i, f: (0, f)),
            pl.BlockSpec((tf, D_MODEL), lambda i, f: (f, 0)),
        ],
        out_specs=pl.BlockSpec((tm, D_MODEL), lambda i, f: (i, 0)),
        out_shape=jax.ShapeDtypeStruct((m, D_MODEL), F32),
        scratch_shapes=[pltpu.VMEM((tm, D_MODEL), BF16)],
        compiler_params=_params(("arbitrary", "arbitrary")),
        name="ffn",
    )(x2, g, wg, wu, wd)


SLABS = D_MODEL // LANES


def _route_kernel(x_ref, g_ref, wr_ref, h_ref, idx_ref, wgt_ref):
    h = _rms_rows(x_ref[...], g_ref[...])
    h_ref[...] = h.astype(BF16)
    logits = jnp.dot(h, wr_ref[...], precision=lax.Precision.HIGHEST, preferred_element_type=F32)
    lane = lax.broadcasted_iota(I32, logits.shape, 1)
    logits = jnp.where(lane < N_EXPERTS, logits, -jnp.inf)
    v0 = jnp.max(logits, axis=-1, keepdims=True)
    i0 = jnp.min(jnp.where(logits == v0, lane, LANES), axis=-1, keepdims=True)
    rest = jnp.where(lane == i0, -jnp.inf, logits)
    v1 = jnp.max(rest, axis=-1, keepdims=True)
    i1 = jnp.min(jnp.where(rest == v1, lane, LANES), axis=-1, keepdims=True)
    e1 = jnp.exp(v1 - v0)
    w0 = 1.0 / (1.0 + e1)
    idx_ref[...] = jnp.where(lane == 0, i0, jnp.where(lane == 1, i1, 0))
    wgt_ref[...] = jnp.where(lane == 0, w0, jnp.where(lane == 1, e1 * w0, 0.0))


def _route(x2, g, w_router, tm=256):
    m = x2.shape[0]
    tm = min(tm, m)
    return pl.pallas_call(
        _route_kernel,
        grid=(m // tm,),
        in_specs=[
            pl.BlockSpec((tm, D_MODEL), lambda i: (i, 0)),
            pl.BlockSpec((1, D_MODEL), lambda i: (0, 0)),
            pl.BlockSpec((D_MODEL, LANES), lambda i: (0, 0)),
        ],
        out_specs=[
            pl.BlockSpec((tm, D_MODEL), lambda i: (i, 0)),
            pl.BlockSpec((tm, LANES), lambda i: (i, 0)),
            pl.BlockSpec((tm, LANES), lambda i: (i, 0)),
        ],
        out_shape=[
            jax.ShapeDtypeStruct((m, D_MODEL), BF16),
            jax.ShapeDtypeStruct((m, LANES), I32),
            jax.ShapeDtypeStruct((m, LANES), F32),
        ],
        compiler_params=_params(("arbitrary",)),
        name="route",
    )(x2, g, w_router)


def _gather_kernel(src_ref, in_ref, out_ref, sem, *, rows):
    base = pl.program_id(0) * rows

    def start(pair, carry):
        for prio in range(2):
            r = 2 * pair + prio
            pltpu.make_async_copy(in_ref.at[src_ref[base + r]], out_ref.at[r], sem).start(priority=prio)
        return carry

    lax.fori_loop(0, rows // 2, start, 0, unroll=4)
    pltpu.make_async_copy(in_ref.at[pl.ds(0, rows)], out_ref, sem).wait()


def _gather_rows(src, table, n_out, rows=GATHER_ROWS):
    grid_spec = pltpu.PrefetchScalarGridSpec(
        num_scalar_prefetch=1,
        grid=(n_out // rows,),
        in_specs=[pl.BlockSpec(memory_space=pl.ANY)],
        out_specs=pl.BlockSpec((rows, SLABS, LANES), lambda i, src: (i, 0, 0)),
        scratch_shapes=[pltpu.SemaphoreType.DMA(())],
    )
    out = pl.pallas_call(
        functools.partial(_gather_kernel, rows=rows),
        grid_spec=grid_spec,
        out_shape=jax.ShapeDtypeStruct((n_out, SLABS, LANES), table.dtype),
        compiler_params=pltpu.CompilerParams(dimension_semantics=("arbitrary",)),
        name="gather_rows",
    )(src, table.reshape(table.shape[0], SLABS, LANES))
    return out.reshape(n_out, D_MODEL)


def _gffn_kernel(te_ref, tv_ref, hs_ref, wg_ref, wu_ref, wd_ref, out_ref, acc_scr, wg_scr, wu_scr, wd_scr):
    i = pl.program_id(0)
    f = pl.program_id(1)
    nf = pl.num_programs(1)

    @pl.when(f == 0)
    def _():
        acc_scr[...] = jnp.zeros_like(acc_scr)

    @pl.when(tv_ref[i] > 0)
    def _():
        wg_scr[...] = wg_ref[0].astype(BF16)
        wu_scr[...] = wu_ref[0].astype(BF16)
        wd_scr[...] = wd_ref[0].astype(BF16)
        rows = hs_ref.shape[0]
        _swiglu_accumulate(hs_ref, wg_scr[...], wu_scr[...], wd_scr[...], acc_scr, rows, sub=rows)

    @pl.when(f == nf - 1)
    def _():
        out_ref[...] = acc_scr[...].astype(BF16)


def _gffn(tile_expert, tile_valid, hs, wg, wu, wd, tm=MOE_TILE, tf=512):
    p = hs.shape[0]
    nf = D_FF // tf

    def fblk(f, tv, i):
        return jnp.where(tv[i] > 0, f, nf - 1)

    grid_spec = pltpu.PrefetchScalarGridSpec(
        num_scalar_prefetch=2,
        grid=(p // tm, nf),
        in_specs=[
            pl.BlockSpec((tm, D_MODEL), lambda i, f, te, tv: (i, 0)),
            pl.BlockSpec((1, D_MODEL, tf), lambda i, f, te, tv: (te[i], 0, fblk(f, tv, i))),
            pl.BlockSpec((1, D_MODEL, tf), lambda i, f, te, tv: (te[i], 0, fblk(f, tv, i))),
            pl.BlockSpec((1, tf, D_MODEL), lambda i, f, te, tv: (te[i], fblk(f, tv, i), 0)),
        ],
        out_specs=pl.BlockSpec((tm, D_MODEL), lambda i, f, te, tv: (i, 0)),
        scratch_shapes=[
            pltpu.VMEM((tm, D_MODEL), F32),
            pltpu.VMEM((D_MODEL, tf), BF16),
            pltpu.VMEM((D_MODEL, tf), BF16),
            pltpu.VMEM((tf, D_MODEL), BF16),
        ],
    )
    return pl.pallas_call(
        _gffn_kernel,
        grid_spec=grid_spec,
        out_shape=jax.ShapeDtypeStruct((p, D_MODEL), BF16),
        compiler_params=_params(("arbitrary", "arbitrary")),
        name="grouped_ffn",
    )(tile_expert, tile_valid, hs, wg, wu, wd)


def _combine_kernel(x_ref, y_ref, w_ref, out_ref):
    w = w_ref[...]
    w0 = w[:, 0:1]
    w1 = w[:, 1:2]
    out_ref[...] = (x_ref[...] + w0 * y_ref[:, :D_MODEL].astype(F32) + w1 * y_ref[:, D_MODEL:].astype(F32))


def _combine(x2, y_pairs, top_w, tm=512):
    m = x2.shape[0]
    tm = min(tm, m)
    return pl.pallas_call(
        _combine_kernel,
        grid=(m // tm,),
        in_specs=[
            pl.BlockSpec((tm, D_MODEL), lambda i: (i, 0)),
            pl.BlockSpec((tm, TOP_K * D_MODEL), lambda i: (i, 0)),
            pl.BlockSpec((tm, LANES), lambda i: (i, 0)),
        ],
        out_specs=pl.BlockSpec((tm, D_MODEL), lambda i: (i, 0)),
        out_shape=jax.ShapeDtypeStruct((m, D_MODEL), F32),
        compiler_params=_params(("arbitrary",)),
        name="combine",
    )(x2, y_pairs, top_w)


def _moe(x2, g, w_router, wg, wu, wd):
    m = x2.shape[0]
    tm = MOE_TILE
    h, top_idx, top_w = _route(x2, g, w_router)
    e_flat = top_idx[:, :TOP_K].reshape(-1)
    onehot = (e_flat[:, None] == jnp.arange(N_EXPERTS, dtype=I32)[None, :]).astype(I32)
    ranks = jnp.cumsum(onehot, axis=0) - onehot
    rank = jnp.sum(ranks * onehot, axis=1)
    counts = jnp.sum(onehot, axis=0)
    padded = ((counts + tm - 1) // tm) * tm
    seg_end = jnp.cumsum(padded)
    seg_start = seg_end - padded
    pos = seg_start[e_flat] + rank
    step = math.lcm(tm, GATHER_ROWS)
    p_max = -(-(TOP_K * m + N_EXPERTS * tm) // step) * step
    n_tiles = p_max // tm
    src = (jnp.arange(p_max, dtype=I32) % m).at[pos].set(jnp.arange(TOP_K * m, dtype=I32) // TOP_K)
    tile_start = jnp.arange(n_tiles, dtype=I32) * tm
    tile_valid = (tile_start < seg_end[-1]).astype(I32)
    last_expert = jnp.max(jnp.where(counts > 0, jnp.arange(N_EXPERTS, dtype=I32), 0))
    tile_expert = jnp.sum((tile_start[:, None] >= seg_end[None, :]).astype(I32), axis=1)
    tile_expert = jnp.where(tile_valid > 0, tile_expert, last_expert)

    hs = _gather_rows(src, h, p_max)
    ys = _gffn(tile_expert, tile_valid, hs, wg, wu, wd)
    y_pairs = _gather_rows(pos, ys, TOP_K * m).reshape(m, TOP_K * D_MODEL)
    return _combine(x2, y_pairs, top_w)


def _pad_row(v, width):
    return jnp.pad(v.astype(F32), (0, width - v.shape[0])).reshape(1, width)


def _alibi_tables():
    h = jnp.arange(1, NSA_HEADS + 1, dtype=F32)
    slopes = (jnp.exp2(-8.0 * h / NSA_HEADS) * LOG2E).reshape(NSA_KV_HEADS, NSA_GROUP)
    lanes_tbl = jnp.broadcast_to(jnp.pad(slopes, ((0, 0), (0, 8 - NSA_GROUP)))[:, :, None], (NSA_KV_HEADS, 8, LANES))
    return lanes_tbl, slopes.reshape(-1)


def _mixer_layer(x2, bsz, seq, norm_mix, w_in, ssm_conv_w, ssm_conv_b, ssm_dt_bias, ssm_a_log, ssm_d, ssm_norm,
                 w_ssm_out, sc_conv_w, w_sc_out, q_norm, k_norm, cmp_pos, w_cmp_k, w_cmp_v, w_nsa_out, w_out):
    w_big = jnp.concatenate(
        [w_in[:, _SRC_XBC:_SRC_DT], w_in[:, _SRC_Z:_SRC_XBC], w_in[:, _SRC_SC:_SRC_Q], w_in[:, _SRC_MG:],
         w_in[:, _SRC_Q:_SRC_NG]], axis=1).astype(BF16)
    w_small = jnp.concatenate(
        [w_in[:, _SRC_DT:_SRC_SC], w_in[:, _SRC_NG:_SRC_MG],
         jnp.zeros((D_MODEL, SMALL_W - SSM_HEADS - N_BRANCH * NSA_HEADS), F32)], axis=1).astype(BF16)
    proj, small = _inproj(x2, norm_mix.reshape(1, D_MODEL), w_big, w_small)

    y_ssm = _ssd(proj, small, ssm_conv_w, ssm_conv_b.reshape(1, SSM_XBC), _pad_row(ssm_dt_bias, SMALL_W),
                 _pad_row(ssm_a_log, SMALL_W), jnp.repeat(ssm_d, SSM_HEAD_DIM).reshape(1, D_MODEL),
                 ssm_norm.reshape(1, D_MODEL), bsz, seq)

    G, Dh = NSA_KV_HEADS, NSA_HEAD_DIM
    nb = seq // CMP_STRIDE
    r5 = proj[:, KV_OFF:KV_OFF + 2 * G * Dh].reshape(bsz, nb, CMP_STRIDE, 2, G, Dh)
    r5 = r5.transpose(0, 4, 3, 1, 2, 5).reshape(bsz, G, 2, nb, CMP_STRIDE * Dh)
    lanes_tbl, slopes_flat = _alibi_tables()
    half_k = CMP_STRIDE * Dh
    o_cmp, qn, sel = _nsa_cmp(r5, proj, w_cmp_k.reshape(2, half_k, Dh).astype(BF16),
                              w_cmp_v.reshape(2, half_k, Dh).astype(BF16), cmp_pos.reshape(2, 2, half_k),
                              k_norm[0:1], q_norm.reshape(1, Dh), lanes_tbl, bsz, seq)
    sel_flat = sel[..., :SEL_TOPK].reshape(-1)
    y_nsa = _nsa_attn(sel_flat, slopes_flat, qn, proj, o_cmp, small, k_norm, bsz, seq)

    mixed = _merge(y_ssm, proj, y_nsa, sc_conv_w, w_ssm_out.astype(BF16), w_sc_out.astype(BF16),
                   w_nsa_out.astype(BF16), seq)
    return _outproj(mixed, w_out.astype(BF16), x2)


def kernel(x, norm_mix, w_in, ssm_conv_w, ssm_conv_b, ssm_dt_bias, ssm_a_log, ssm_d, ssm_norm, w_ssm_out,
           sc_conv_w, w_sc_out, q_norm, k_norm, cmp_pos, w_cmp_k, w_cmp_v, w_nsa_out, w_out, norm_ffn,
           ffn_w_gate, ffn_w_up, ffn_w_down, moe_router, moe_w_gate, moe_w_up, moe_w_down):
    bsz, seq, _ = x.shape
    depth = norm_mix.shape[0]
    x2 = x.reshape(bsz * seq, D_MODEL)
    for layer in range(depth):
        x2 = _mixer_layer(x2, bsz, seq, norm_mix[layer], w_in[layer], ssm_conv_w[layer], ssm_conv_b[layer],
                          ssm_dt_bias[layer], ssm_a_log[layer], ssm_d[layer], ssm_norm[layer], w_ssm_out[layer],
                          sc_conv_w[layer], w_sc_out[layer], q_norm[layer], k_norm[layer], cmp_pos[layer],
                          w_cmp_k[layer], w_cmp_v[layer], w_nsa_out[layer], w_out[layer])
        g = norm_ffn[layer].reshape(1, D_MODEL)
        i = layer // 2
        if layer % 2 == 0:
            x2 = _ffn(x2, g, ffn_w_gate[i].astype(BF16), ffn_w_up[i].astype(BF16), ffn_w_down[i].astype(BF16))
        else:
            w_router = jnp.pad(moe_router[i], ((0, 0), (0, LANES - N_EXPERTS)))
            x2 = _moe(x2, g, w_router, moe_w_gate[i], moe_w_up[i], moe_w_down[i])
    return x2.reshape(bsz, seq, D_MODEL)
```
